```python
import jax, jax.numpy as jnp
from jax import lax
import numpy as np

D_MODEL = 1024
BATCH = 4
SEQ = 8192
DEPTH = 1
DEC_BATCH = 128
DEC_SEQ = 4
PAST_LEN = 16384
PAGE_SIZE = 128

HEAD_DIM = 64
N_HEADS = D_MODEL // HEAD_DIM
N_CHUNK_HEADS = N_HEADS // 2
N_ATTN_HEADS = N_HEADS - N_CHUNK_HEADS
N_KV_HEADS = 2
Q_PER_KV = N_ATTN_HEADS // N_KV_HEADS
CHUNK_WIDTH = N_CHUNK_HEADS * HEAD_DIM
ATTN_WIDTH = N_ATTN_HEADS * HEAD_DIM
KV_WIDTH = N_KV_HEADS * HEAD_DIM
IN_WIDTH = 2 * CHUNK_WIDTH + ATTN_WIDTH + 2 * KV_WIDTH
SPLITS = (CHUNK_WIDTH, 2 * CHUNK_WIDTH, 2 * CHUNK_WIDTH + ATTN_WIDTH, 2 * CHUNK_WIDTH + ATTN_WIDTH + KV_WIDTH)
CHUNK = 128
WINDOW = 128
D_FF = 4 * D_MODEL
EPS = 1e-6
ATTN_SCALE = HEAD_DIM ** -0.5

kernel_name = 'hymba_chunkmlp_swa_sink_step'


def rms_norm(x, g):
    xf = x.astype(jnp.float32)
    y = xf * lax.rsqrt(jnp.mean(xf * xf, axis=-1, keepdims=True) + EPS)
    return (y * g.astype(jnp.float32)).astype(x.dtype)


def layer_norm(x, g, b):
    xf = x.astype(jnp.float32)
    xc = xf - jnp.mean(xf, axis=-1, keepdims=True)
    y = xc * lax.rsqrt(jnp.mean(xc * xc, axis=-1, keepdims=True) + EPS)
    return (y * g.astype(jnp.float32) + b.astype(jnp.float32)).astype(x.dtype)


def alibi_slopes():
    i = jnp.arange(1, N_ATTN_HEADS + 1, dtype=jnp.float32)
    return jnp.exp2(-8.0 * i / N_ATTN_HEADS).reshape(N_KV_HEADS, Q_PER_KV)


def mixer_projections(x, g_pre, w_in, ln_g, ln_b):
    h = rms_norm(x, g_pre)
    z = jnp.einsum('bsd,de->bse', h, w_in)
    u, v, q, k, val = jnp.split(z, SPLITS, axis=-1)
    u = jax.nn.gelu(u)
    v = layer_norm(jax.nn.gelu(v), ln_g, ln_b)
    return u, v, q, k, val


def spatial_gate(u, v, w_s, b_s):
    B, C, n, _ = v.shape
    causal = jnp.tril(jnp.ones((n, n), dtype=bool))
    w = jnp.where(causal, w_s[:, :n, :n], 0).astype(v.dtype)
    vh = v.reshape(B, C, n, N_CHUNK_HEADS, HEAD_DIM)
    s = jnp.einsum('gts,bcsgd->bctgd', w, vh) + b_s[:, :n].T[:, :, None].astype(v.dtype)
    return u * s.reshape(B, C, n, CHUNK_WIDTH)


def sink_attention(q, k, v, dist, valid, sinks):
    s = jnp.einsum('...qhrd,...khd->...hrqk', q, k, preferred_element_type=jnp.float32) * ATTN_SCALE
    s = s - alibi_slopes()[:, :, None, None] * dist.astype(jnp.float32)
    s = jnp.where(valid, s, -jnp.inf)
    sink = sinks.astype(jnp.float32)[:, :, None, None]
    m = jnp.maximum(jnp.max(s, axis=-1, keepdims=True), sink)
    p = jnp.exp(s - m)
    z = jnp.sum(p, axis=-1, keepdims=True) + jnp.exp(sink - m)
    p = (p / z).astype(v.dtype)
    return jnp.einsum('...hrqk,...khd->...qhrd', p, v)


def swa_prompt(q, k, v, sinks):
    B, S, _ = q.shape
    nb = S // WINDOW
    qb = q.reshape(B, nb, WINDOW, N_KV_HEADS, Q_PER_KV, HEAD_DIM)

    def band(t):
        t = t.reshape(B, S, N_KV_HEADS, HEAD_DIM)
        t = jnp.pad(t, ((0, 0), (WINDOW, 0), (0, 0), (0, 0)))
        t = t.reshape(B, nb + 1, WINDOW, N_KV_HEADS, HEAD_DIM)
        return jnp.concatenate([t[:, :-1], t[:, 1:]], axis=2)

    kb, vb = band(k), band(v)
    a = jnp.arange(WINDOW)[:, None]
    c = jnp.arange(2 * WINDOW)[None, :]
    dist = WINDOW + a - c
    kpos = (jnp.arange(nb)[:, None, None] - 1) * WINDOW + c[None]
    valid = (dist >= 0) & (dist <= WINDOW) & (kpos >= 0)
    out = sink_attention(qb, kb, vb, dist, valid[:, None, None], sinks)
    return out.reshape(B, S, ATTN_WIDTH)


def swa_sample(q, k, v, cache_k, cache_v, sinks):
    Bd, L, _ = q.shape
    Wc = cache_k.shape[1]
    qh = q.reshape(Bd, L, N_KV_HEADS, Q_PER_KV, HEAD_DIM)
    kc = jnp.concatenate([cache_k, k.reshape(Bd, L, N_KV_HEADS, HEAD_DIM)], axis=1)
    vc = jnp.concatenate([cache_v, v.reshape(Bd, L, N_KV_HEADS, HEAD_DIM)], axis=1)
    dist = (Wc + jnp.arange(L)[:, None]) - jnp.arange(Wc + L)[None, :]
    valid = (dist >= 0) & (dist <= WINDOW)
    out = sink_attention(qh, kc, vc, dist, valid, sinks)
    return out.reshape(Bd, L, ATTN_WIDTH), kc[:, L:], vc[:, L:]


def merge_and_channel_mix(x, a_out, b_out, g_out_chunk, g_out_attn, w_o, g_post_mix,
                          g_pre_ffn, w_up, w_down, g_post_ffn):
    merged = jnp.concatenate([rms_norm(a_out, g_out_chunk), rms_norm(b_out, g_out_attn)], axis=-1)
    o = jnp.einsum('bsc,cd->bsd', merged, w_o)
    x = x + rms_norm(o, g_post_mix)
    h = rms_norm(x, g_pre_ffn)
    f = jnp.einsum('bsf,fd->bsd', jnp.square(jax.nn.relu(jnp.einsum('bsd,df->bsf', h, w_up))), w_down)
    return x + rms_norm(f, g_post_ffn)


def setup_inputs(seed: int = 0) -> dict:
    key = jax.random.key(seed)
    ks = jax.random.split(key, 20)
    f32 = jnp.float32
    win = min(WINDOW, PAST_LEN)
    nrm = lambda k, shape: jax.random.normal(k, shape, f32)
    return {
        'x_prompt': nrm(ks[0], (BATCH, SEQ, D_MODEL)),
        'x_sample': nrm(ks[1], (DEC_BATCH, DEC_SEQ, D_MODEL)),
        'cache_win_k': nrm(ks[2], (DEPTH, DEC_BATCH, win, N_KV_HEADS, HEAD_DIM)),
        'cache_win_v': nrm(ks[3], (DEPTH, DEC_BATCH, win, N_KV_HEADS, HEAD_DIM)),
        'w_in': nrm(ks[4], (DEPTH, D_MODEL, IN_WIDTH)) * D_MODEL ** -0.5,
        'g_pre_mix': 1.0 + 0.05 * nrm(ks[5], (DEPTH, D_MODEL)),
        'ln_v_g': 1.0 + 0.05 * nrm(ks[6], (DEPTH, CHUNK_WIDTH)),
        'ln_v_b': 0.02 * nrm(ks[7], (DEPTH, CHUNK_WIDTH)),
        'w_spatial': nrm(ks[8], (DEPTH, N_CHUNK_HEADS, CHUNK, CHUNK)) * 0.5 * CHUNK ** -0.5,
        'b_spatial': 1.0 + 0.1 * nrm(ks[9], (DEPTH, N_CHUNK_HEADS, CHUNK)),
        'attn_sinks': 0.5 * nrm(ks[10], (DEPTH, N_KV_HEADS, Q_PER_KV)),
        'g_out_chunk': 1.0 + 0.05 * nrm(ks[11], (DEPTH, CHUNK_WIDTH)),
        'g_out_attn': 1.0 + 0.05 * nrm(ks[12], (DEPTH, ATTN_WIDTH)),
        'w_o': nrm(ks[13], (DEPTH, CHUNK_WIDTH + ATTN_WIDTH, D_MODEL)) * (CHUNK_WIDTH + ATTN_WIDTH) ** -0.5,
        'g_post_mix': 1.0 + 0.05 * nrm(ks[14], (DEPTH, D_MODEL)),
        'g_pre_ffn': 1.0 + 0.05 * nrm(ks[15], (DEPTH, D_MODEL)),
        'w_up': nrm(ks[16], (DEPTH, D_MODEL, D_FF)) * D_MODEL ** -0.5,
        'w_down': nrm(ks[17], (DEPTH, D_FF, D_MODEL)) * D_FF ** -0.5,
        'g_post_ffn': 1.0 + 0.05 * nrm(ks[18], (DEPTH, D_MODEL)),
    }


def reference(x_prompt, x_sample, cache_win_k, cache_win_v, w_in, g_pre_mix, ln_v_g, ln_v_b,
              w_spatial, b_spatial, attn_sinks, g_out_chunk, g_out_attn, w_o, g_post_mix,
              g_pre_ffn, w_up, w_down, g_post_ffn):
    yp, ys = x_prompt, x_sample
    B, S, _ = x_prompt.shape
    Bd, L, _ = x_sample.shape
    wk_p, wv_p, cv_p, wk_s, wv_s, cv_s = [], [], [], [], [], []
    for l in range(DEPTH):
        u, v, q, k, val = mixer_projections(yp, g_pre_mix[l], w_in[l], ln_v_g[l], ln_v_b[l])
        a_out = spatial_gate(u.reshape(B, S // CHUNK, CHUNK, CHUNK_WIDTH),
                             v.reshape(B, S // CHUNK, CHUNK, CHUNK_WIDTH),
                             w_spatial[l], b_spatial[l]).reshape(B, S, CHUNK_WIDTH)
        b_out = swa_prompt(q, k, val, attn_sinks[l])
        yp = merge_and_channel_mix(yp, a_out, b_out, g_out_chunk[l], g_out_attn[l], w_o[l],
                                   g_post_mix[l], g_pre_ffn[l], w_up[l], w_down[l], g_post_ffn[l])
        wk_p.append(k.reshape(B, S, N_KV_HEADS, HEAD_DIM)[:, S - WINDOW:])
        wv_p.append(val.reshape(B, S, N_KV_HEADS, HEAD_DIM)[:, S - WINDOW:])
        cv_p.append(v[:, S - CHUNK:])

        u, v, q, k, val = mixer_projections(ys, g_pre_mix[l], w_in[l], ln_v_g[l], ln_v_b[l])
        a_out = spatial_gate(u[:, None], v[:, None], w_spatial[l], b_spatial[l])[:, 0]
        b_out, new_k, new_v = swa_sample(q, k, val, cache_win_k[l], cache_win_v[l], attn_sinks[l])
        ys = merge_and_channel_mix(ys, a_out, b_out, g_out_chunk[l], g_out_attn[l], w_o[l],
                                   g_post_mix[l], g_pre_ffn[l], w_up[l], w_down[l], g_post_ffn[l])
        wk_s.append(new_k)
        wv_s.append(new_v)
        cv_s.append(v)
    return (yp, ys, jnp.stack(wk_p), jnp.stack(wv_p), jnp.stack(cv_p),
            jnp.stack(wk_s), jnp.stack(wv_s), jnp.stack(cv_s))
```

```python
import functools

import jax
import jax.numpy as jnp
from jax import lax
from jax.experimental import pallas as pl
from jax.experimental.pallas import tpu as pltpu

F32 = jnp.float32
BF16 = jnp.bfloat16

HEAD_DIM = 64
N_CHUNK_HEADS = 8
N_ATTN_HEADS = 8
N_KV_HEADS = 2
Q_PER_KV = N_ATTN_HEADS // N_KV_HEADS
CHUNK_WIDTH = N_CHUNK_HEADS * HEAD_DIM
ATTN_WIDTH = N_ATTN_HEADS * HEAD_DIM
KV_WIDTH = N_KV_HEADS * HEAD_DIM
CHUNK = 128
WINDOW = 128
EPS = 1e-6
ATTN_SCALE = HEAD_DIM ** -0.5

LANES = 128
HALF = LANES // 2
N_PAIRS = ATTN_WIDTH // LANES
VMEM_LIMIT = 56 * 1024 * 1024

PROMPT_TILE = 512
FF_CHUNK = 1024
SAMPLE_BB = 16

ROW_G_PRE, ROW_G_PM, ROW_G_PFF, ROW_G_PF, ROW_LN, ROW_GOUT = range(6)


def _dot(a, b):
    return jnp.dot(a, b, preferred_element_type=F32)


def _rms(x, g):
    return x * lax.rsqrt(jnp.mean(x * x, axis=-1, keepdims=True) + EPS) * g


def _layer_norm(x, g, b):
    xc = x - jnp.mean(x, axis=-1, keepdims=True)
    return xc * lax.rsqrt(jnp.mean(xc * xc, axis=-1, keepdims=True) + EPS) * g + b


def _alibi_slope(head):
    return 2.0 ** (-8.0 * (head + 1) / N_ATTN_HEADS)


def _project(x, w_in_ref, vec_ref):
    d = x.shape[-1]
    cw = CHUNK_WIDTH
    h = _rms(x, vec_ref[ROW_G_PRE:ROW_G_PRE + 1, :d]).astype(BF16)
    u = jax.nn.gelu(_dot(h, w_in_ref[:, 0:cw]))
    v = _layer_norm(jax.nn.gelu(_dot(h, w_in_ref[:, cw:2 * cw])),
                    vec_ref[ROW_LN:ROW_LN + 1, 0:cw], vec_ref[ROW_LN:ROW_LN + 1, cw:2 * cw])
    kv0 = 2 * cw + ATTN_WIDTH
    kv = _dot(h, w_in_ref[:, kv0:kv0 + 2 * KV_WIDTH])
    return h, u, v, kv


def _merge_ffn(x, a_out, b_out, w_o_ref, w_up_ref, w_down_ref, vec_ref):
    d = x.shape[-1]
    cw = CHUNK_WIDTH
    ra = _rms(a_out, vec_ref[ROW_GOUT:ROW_GOUT + 1, 0:cw]).astype(BF16)
    rb = _rms(b_out, vec_ref[ROW_GOUT:ROW_GOUT + 1, cw:cw + ATTN_WIDTH]).astype(BF16)
    o = _dot(ra, w_o_ref[0:cw, :]) + _dot(rb, w_o_ref[cw:cw + ATTN_WIDTH, :])
    x1 = x + _rms(o, vec_ref[ROW_G_PM:ROW_G_PM + 1, :d])
    h2 = _rms(x1, vec_ref[ROW_G_PFF:ROW_G_PFF + 1, :d]).astype(BF16)
    d_ff = w_up_ref.shape[1]
    f = jnp.zeros(x.shape, F32)
    for c in range(d_ff // FF_CHUNK):
        up = _dot(h2, w_up_ref[:, c * FF_CHUNK:(c + 1) * FF_CHUNK])
        r = jnp.maximum(up, 0.0)
        f = f + _dot((r * r).astype(BF16), w_down_ref[c * FF_CHUNK:(c + 1) * FF_CHUNK, :])
    return x1 + _rms(f, vec_ref[ROW_G_PF:ROW_G_PF + 1, :d])


def _prompt_kernel(sink_ref, x_ref, w_in_ref, w_o_ref, w_up_ref, w_down_ref, vec_ref,
                   wsp_ref, bsp_ref,
                   y_ref, wk_ref, wv_ref, cv_ref,
                   wcat_s, tab_s, kt_s, va_s, vb_s, a_s, bo_s):
    b = pl.program_id(0)
    j = pl.program_id(1)
    nj = pl.num_programs(1)
    T = x_ref.shape[1]
    nb = T // WINDOW
    cw = CHUNK_WIDTH

    lane = lax.broadcasted_iota(jnp.int32, (WINDOW, LANES), 1)
    lo = lane < HALF

    @pl.when((b == 0) & (j == 0))
    def _init_tables():
        t_idx = lax.broadcasted_iota(jnp.int32, (CHUNK, CHUNK), 0)
        s_idx = lax.broadcasted_iota(jnp.int32, (CHUNK, CHUNK), 1)
        for g in range(N_CHUNK_HEADS):
            wm = jnp.where(s_idx <= t_idx, wsp_ref[g], 0.0).astype(BF16)
            wcat_s[g // 2, :, (g % 2) * CHUNK:(g % 2 + 1) * CHUNK] = wm
        a_idx = lax.broadcasted_iota(jnp.int32, (WINDOW, 2 * WINDOW), 0)
        c_idx = lax.broadcasted_iota(jnp.int32, (WINDOW, 2 * WINDOW), 1)
        dist = WINDOW + a_idx - c_idx
        valid = (dist >= 0) & (dist <= WINDOW)
        for hd in range(N_ATTN_HEADS):
            bias = jnp.where(valid, -(_alibi_slope(hd) * dist.astype(F32)), -jnp.inf)
            tab_s[hd // 2, (hd % 2) * WINDOW:(hd % 2 + 1) * WINDOW, :] = bias

    @pl.when(j == 0)
    def _clear_carry():
        kt_s[:, 0:WINDOW] = jnp.zeros((2 * LANES, WINDOW), BF16)
        va_s[:, 0:WINDOW, :] = jnp.zeros((N_KV_HEADS, WINDOW, LANES), BF16)
        vb_s[:, 0:WINDOW, :] = jnp.zeros((N_KV_HEADS, WINDOW, LANES), BF16)

    x = x_ref[0]
    h, u, v, kv = _project(x, w_in_ref, vec_ref)
    k = kv[:, 0:KV_WIDTH]
    val = kv[:, KV_WIDTH:2 * KV_WIDTH]

    @pl.when(j == nj - 1)
    def _emit_windows():
        wk_ref[0] = k[T - WINDOW:, :]
        wv_ref[0] = val[T - WINDOW:, :]
        cv_ref[0] = v[T - CHUNK:, :]

    for c in range(nb):
        rows = slice(c * CHUNK, (c + 1) * CHUNK)
        for p in range(cw // LANES):
            cols = slice(p * LANES, (p + 1) * LANES)
            vcp = v[rows, cols]
            rhs = jnp.concatenate([jnp.where(lo, vcp, 0.0), jnp.where(lo, 0.0, vcp)],
                                  axis=0).astype(BF16)
            s_cp = _dot(wcat_s[p], rhs) + bsp_ref[:, cols]
            a_s[rows, cols] = u[rows, cols] * s_cp

    kt = k.T.astype(BF16)
    for hk in range(N_KV_HEADS):
        kth = kt[hk * HEAD_DIM:(hk + 1) * HEAD_DIM, :]
        kt_s[hk * LANES:hk * LANES + HEAD_DIM, WINDOW:] = kth
        kt_s[hk * LANES + HEAD_DIM:(hk + 1) * LANES, WINDOW:] = kth
    lane_t = lax.broadcasted_iota(jnp.int32, (T, LANES), 1)
    lo_t = lane_t < HALF
    val_sw = pltpu.roll(val, HALF, axis=1)
    va_s[0, WINDOW:, :] = jnp.where(lo_t, val, 0.0).astype(BF16)
    vb_s[0, WINDOW:, :] = jnp.where(lo_t, 0.0, val_sw).astype(BF16)
    va_s[1, WINDOW:, :] = jnp.where(lo_t, val_sw, 0.0).astype(BF16)
    vb_s[1, WINDOW:, :] = jnp.where(lo_t, 0.0, val).astype(BF16)

    q0 = 2 * cw
    qs = _dot(h, w_in_ref[:, q0:q0 + ATTN_WIDTH]) * ATTN_SCALE
    first_block_bias = jnp.where(j == 0, -jnp.inf, 0.0).astype(F32)
    row2 = lax.broadcasted_iota(jnp.int32, (2 * WINDOW, 1), 0)
    for i in range(nb):
        rows = slice(i * WINDOW, (i + 1) * WINDOW)
        keys = slice(i * WINDOW, (i + 2) * WINDOW)
        for m in range(N_PAIRS):
            hk = (2 * m) // Q_PER_KV
            cols = slice(m * LANES, (m + 1) * LANES)
            qb = qs[rows, cols]
            qq = jnp.concatenate([jnp.where(lo, qb, 0.0), jnp.where(lo, 0.0, qb)],
                                 axis=0).astype(BF16)
            s = _dot(qq, kt_s[hk * LANES:(hk + 1) * LANES, keys]) + tab_s[m]
            if i == 0:
                s = jnp.concatenate([s[:, 0:WINDOW] + first_block_bias, s[:, WINDOW:]], axis=1)
            sink = jnp.where(row2 < WINDOW, sink_ref[2 * m], sink_ref[2 * m + 1])
            mx = jnp.maximum(jnp.max(s, axis=-1, keepdims=True), sink)
            p_un = jnp.exp(s - mx)
            z = jnp.sum(p_un, axis=-1, keepdims=True) + jnp.exp(sink - mx)
            pb = p_un.astype(BF16)
            o = (_dot(pb[0:WINDOW], va_s[hk, keys, :]) +
                 _dot(pb[WINDOW:], vb_s[hk, keys, :]))
            rz = 1.0 / z
            bo_s[rows, cols] = o * jnp.where(lo, rz[0:WINDOW], rz[WINDOW:])

    kt_s[:, 0:WINDOW] = kt_s[:, T:T + WINDOW]
    va_s[:, 0:WINDOW, :] = va_s[:, T:T + WINDOW, :]
    vb_s[:, 0:WINDOW, :] = vb_s[:, T:T + WINDOW, :]

    y_ref[0] = _merge_ffn(x_ref[0], a_s[...], bo_s[...], w_o_ref, w_up_ref, w_down_ref, vec_ref)


def _resident(shape):
    return pl.BlockSpec(shape, lambda *_: (0,) * len(shape), pipeline_mode=pl.Buffered(1))


def _prompt_call(x, sinks, w_in, w_o, w_up, w_down, vecs, w_sp, bias_sp):
    B, S, D = x.shape
    T = PROMPT_TILE
    assert S % T == 0 and T % WINDOW == 0 and WINDOW == CHUNK
    grid = (B, S // T)
    tile = pl.BlockSpec((1, T, D), lambda b, j: (b, j, 0))
    last = lambda w: pl.BlockSpec((1, WINDOW, w), lambda b, j: (b, 0, 0))
    return pl.pallas_call(
        _prompt_kernel,
        grid=grid,
        in_specs=[pl.BlockSpec(memory_space=pltpu.SMEM), tile,
                  _resident(w_in.shape), _resident(w_o.shape), _resident(w_up.shape),
                  _resident(w_down.shape), _resident(vecs.shape), _resident(w_sp.shape),
                  _resident(bias_sp.shape)],
        out_specs=[tile, last(KV_WIDTH), last(KV_WIDTH), last(CHUNK_WIDTH)],
        out_shape=[jax.ShapeDtypeStruct((B, S, D), F32),
                   jax.ShapeDtypeStruct((B, WINDOW, KV_WIDTH), F32),
                   jax.ShapeDtypeStruct((B, WINDOW, KV_WIDTH), F32),
                   jax.ShapeDtypeStruct((B, CHUNK, CHUNK_WIDTH), F32)],
        scratch_shapes=[pltpu.VMEM((N_CHUNK_HEADS // 2, CHUNK, 2 * CHUNK), BF16),
                        pltpu.VMEM((N_PAIRS, 2 * WINDOW, 2 * WINDOW), F32),
                        pltpu.VMEM((2 * LANES, T + WINDOW), BF16),
                        pltpu.VMEM((N_KV_HEADS, T + WINDOW, LANES), BF16),
                        pltpu.VMEM((N_KV_HEADS, T + WINDOW, LANES), BF16),
                        pltpu.VMEM((T, CHUNK_WIDTH), F32),
                        pltpu.VMEM((T, ATTN_WIDTH), F32)],
        compiler_params=pltpu.CompilerParams(
            dimension_semantics=("arbitrary", "arbitrary"), vmem_limit_bytes=VMEM_LIMIT),
        name="prompt_layer",
    )(sinks, x, w_in, w_o, w_up, w_down, vecs, w_sp, bias_sp)


def _sample_proj_kernel(L, x_ref, w_in_ref, wq_ref, vec_ref, tab_ref, bs_ref,
                        a_ref, v_ref, q_ref, kv_ref):
    x = x_ref[...]
    n = x.shape[0]
    h, u, v, kv = _project(x, w_in_ref, vec_ref)
    v_ref[...] = v
    kv_ref[...] = kv
    q_ref[...] = _dot(h, wq_ref[...]) * ATTN_SCALE
    t_of_row = lax.broadcasted_iota(jnp.int32, (n, 1), 0) % L
    s = jnp.zeros(v.shape, F32)
    for t in range(L):
        s = jnp.where(t_of_row == t, bs_ref[t:t + 1, :], s)
    for delta in range(L):
        coef = jnp.zeros(v.shape, F32)
        for t in range(delta, L):
            coef = jnp.where(t_of_row == t, tab_ref[delta * L + t:delta * L + t + 1, :], coef)
        vs = v if delta == 0 else pltpu.roll(v, delta, axis=0)
        s = s + coef * vs
    a_ref[...] = u * s


def _sample_attn_kernel(q_ref, ck_ref, cv_ref, kn_ref, vn_ref, info_ref,
                        o_ref, wk_ref, wv_ref):
    bb_n, W, _ = ck_ref.shape
    L = q_ref.shape[1] // N_ATTN_HEADS
    npad = kn_ref.shape[1]
    nk = 2 * W
    r_idx = lax.broadcasted_iota(jnp.int32, (N_ATTN_HEADS * L, nk), 0)
    c_idx = lax.broadcasted_iota(jnp.int32, (N_ATTN_HEADS * L, nk), 1)
    i_idx = r_idx % L
    dist = jnp.where(c_idx < W, W + i_idx - c_idx, i_idx - (c_idx - W))
    valid = (dist >= 0) & (dist <= WINDOW) & (c_idx < W + L)
    sink = info_ref[:, 0:1]
    slope = info_ref[:, 1:2]
    tab = jnp.where(valid, -(slope * dist.astype(F32)), -jnp.inf)
    pad = jnp.zeros((nk - W - npad, LANES), F32)
    lane = lax.broadcasted_iota(jnp.int32, (N_ATTN_HEADS * L, LANES), 1)
    row = lax.broadcasted_iota(jnp.int32, (N_ATTN_HEADS * L, LANES), 0)
    own_half = (lane < HALF) == (row < Q_PER_KV * L)

    def body(bb, carry):
        kf = jnp.concatenate([ck_ref[bb], kn_ref[bb], pad], axis=0)
        vf = jnp.concatenate([cv_ref[bb], vn_ref[bb], pad], axis=0)
        s = lax.dot_general(q_ref[bb].astype(BF16), kf.astype(BF16),
                            (((1,), (1,)), ((), ())), preferred_element_type=F32) + tab
        mx = jnp.maximum(jnp.max(s, axis=-1, keepdims=True), sink)
        p_un = jnp.exp(s - mx)
        z = jnp.sum(p_un, axis=-1, keepdims=True) + jnp.exp(sink - mx)
        o = _dot(p_un.astype(BF16), vf.astype(BF16)) * (1.0 / z)
        o_ref[bb] = jnp.where(own_half, o, 0.0)
        wk_ref[bb] = pltpu.roll(kf, nk - L, axis=0)[0:W]
        wv_ref[bb] = pltpu.roll(vf, nk - L, axis=0)[0:W]
        return carry

    lax.fori_loop(0, bb_n, body, 0)


def _sample_ffn_kernel(x_ref, a_ref, b_ref, w_o_ref, w_up_ref, w_down_ref, vec_ref, y_ref):
    y_ref[...] = _merge_ffn(x_ref[...], a_ref[...], b_ref[...], w_o_ref, w_up_ref,
                            w_down_ref, vec_ref)


def _whole(shape):
    return pl.BlockSpec(shape, lambda *_: (0,) * len(shape))


def _sample_calls(x, cache_k, cache_v, sinks, w_in, wq_pad, w_o, w_up, w_down, vecs,
                  gate_tab, gate_bias):
    Bd, L, D = x.shape
    W = cache_k.shape[1]
    n = Bd * L
    xs = x.reshape(n, D)
    params = pltpu.CompilerParams(vmem_limit_bytes=VMEM_LIMIT)

    outs = [jax.ShapeDtypeStruct((n, CHUNK_WIDTH), F32), jax.ShapeDtypeStruct((n, CHUNK_WIDTH), F32),
            jax.ShapeDtypeStruct((n, N_ATTN_HEADS * LANES), F32),
            jax.ShapeDtypeStruct((n, 2 * KV_WIDTH), F32)]
    ins = (xs, w_in, wq_pad, vecs, gate_tab, gate_bias)
    a_out, v_n, q_pad, kv = pl.pallas_call(
        functools.partial(_sample_proj_kernel, L), grid=(1,),
        in_specs=[_whole(t.shape) for t in ins],
        out_specs=[_whole(o.shape) for o in outs], out_shape=outs,
        compiler_params=params, name="sample_proj")(*ins)

    q_st = q_pad.reshape(Bd, L, N_ATTN_HEADS, LANES).transpose(0, 2, 1, 3)
    q_st = q_st.reshape(Bd, N_ATTN_HEADS * L, LANES)
    npad = 8
    new_k = jnp.pad(kv[:, 0:KV_WIDTH].reshape(Bd, L, KV_WIDTH), ((0, 0), (0, npad - L), (0, 0)))
    new_v = jnp.pad(kv[:, KV_WIDTH:].reshape(Bd, L, KV_WIDTH), ((0, 0), (0, npad - L), (0, 0)))
    ck = cache_k.reshape(Bd, W, KV_WIDTH)
    cv = cache_v.reshape(Bd, W, KV_WIDTH)
    slopes = jnp.asarray([_alibi_slope(hd) for hd in range(N_ATTN_HEADS)], F32)
    info = jnp.zeros((N_ATTN_HEADS * L, LANES), F32)
    info = info.at[:, 0].set(jnp.repeat(sinks, L)).at[:, 1].set(jnp.repeat(slopes, L))

    bb = SAMPLE_BB
    blk = lambda s1, s2: pl.BlockSpec((bb, s1, s2), lambda g: (g, 0, 0))
    o_st, wk_s, wv_s = pl.pallas_call(
        _sample_attn_kernel, grid=(Bd // bb,),
        in_specs=[blk(N_ATTN_HEADS * L, LANES), blk(W, KV_WIDTH), blk(W, KV_WIDTH),
                  blk(npad, KV_WIDTH), blk(npad, KV_WIDTH), _whole(info.shape)],
        out_specs=[blk(N_ATTN_HEADS * L, LANES), blk(W, KV_WIDTH), blk(W, KV_WIDTH)],
        out_shape=[jax.ShapeDtypeStruct((Bd, N_ATTN_HEADS * L, LANES), F32),
                   jax.ShapeDtypeStruct((Bd, W, KV_WIDTH), F32),
                   jax.ShapeDtypeStruct((Bd, W, KV_WIDTH), F32)],
        compiler_params=pltpu.CompilerParams(dimension_semantics=("arbitrary",),
                                             vmem_limit_bytes=VMEM_LIMIT),
        name="sample_attn")(q_st, ck, cv, new_k, new_v, info)

    o5 = o_st.reshape(Bd, N_ATTN_HEADS, L, N_KV_HEADS, HEAD_DIM)
    b_out = (o5[:, :, :, 0, :] + o5[:, :, :, 1, :]).transpose(0, 2, 1, 3).reshape(n, ATTN_WIDTH)

    ins = (xs, a_out, b_out, w_o, w_up, w_down, vecs)
    ys = pl.pallas_call(
        _sample_ffn_kernel, grid=(1,),
        in_specs=[_whole(t.shape) for t in ins],
        out_specs=_whole((n, D)), out_shape=jax.ShapeDtypeStruct((n, D), F32),
        compiler_params=params, name="sample_ffn")(*ins)

    return (ys.reshape(Bd, L, D), wk_s.reshape(cache_k.shape), wv_s.reshape(cache_v.shape),
            v_n.reshape(Bd, L, CHUNK_WIDTH))


def _pack_vectors(d, g_pre, g_pm, g_pff, g_pf, ln_g, ln_b, g_oc, g_oa):
    vecs = jnp.zeros((8, d), F32)
    vecs = vecs.at[ROW_G_PRE].set(g_pre).at[ROW_G_PM].set(g_pm)
    vecs = vecs.at[ROW_G_PFF].set(g_pff).at[ROW_G_PF].set(g_pf)
    vecs = vecs.at[ROW_LN, 0:2 * CHUNK_WIDTH].set(jnp.concatenate([ln_g, ln_b]))
    vecs = vecs.at[ROW_GOUT, 0:CHUNK_WIDTH + ATTN_WIDTH].set(jnp.concatenate([g_oc, g_oa]))
    return vecs


def _pad_q_weights(w_in):
    d = w_in.shape[0]
    wq = w_in[:, 2 * CHUNK_WIDTH:2 * CHUNK_WIDTH + ATTN_WIDTH].reshape(d, N_ATTN_HEADS, HEAD_DIM)
    half = (jnp.arange(N_ATTN_HEADS) // Q_PER_KV)[None, :, None, None]
    sel = (half == jnp.arange(N_KV_HEADS)[None, None, :, None])
    padded = jnp.where(sel, wq[:, :, None, :], 0).astype(w_in.dtype)
    return padded.reshape(d, N_ATTN_HEADS * LANES)


def _sample_gate_tables(w_sp, b_sp, L):
    rows = []
    for delta in range(L):
        for t in range(L):
            if t >= delta:
                rows.append(jnp.repeat(w_sp[:, t, t - delta], HEAD_DIM))
            else:
                rows.append(jnp.zeros((CHUNK_WIDTH,), F32))
    tab = jnp.stack(rows)
    bias = jnp.repeat(b_sp[:, :L].T, HEAD_DIM, axis=1)
    pad = (-L) % 8
    return tab, jnp.pad(bias, ((0, pad), (0, 0)))


def kernel(x_prompt, x_sample, cache_win_k, cache_win_v, w_in, g_pre_mix, ln_v_g, ln_v_b,
           w_spatial, b_spatial, attn_sinks, g_out_chunk, g_out_attn, w_o, g_post_mix,
           g_pre_ffn, w_up, w_down, g_post_ffn):
    depth = w_in.shape[0]
    B, S, D = x_prompt.shape
    Bd, L, _ = x_sample.shape
    yp, ys = x_prompt, x_sample
    wk_p, wv_p, cv_p, wk_s, wv_s, cv_s = [], [], [], [], [], []
    for l in range(depth):
        vecs = _pack_vectors(D, g_pre_mix[l], g_post_mix[l], g_pre_ffn[l], g_post_ffn[l],
                             ln_v_g[l], ln_v_b[l], g_out_chunk[l], g_out_attn[l])
        w_in_b = w_in[l].astype(BF16)
        w_o_b = w_o[l].astype(BF16)
        w_up_b = w_up[l].astype(BF16)
        w_down_b = w_down[l].astype(BF16)
        sinks = attn_sinks[l].reshape(N_ATTN_HEADS).astype(F32)
        bias_sp = jnp.repeat(b_spatial[l].T, HEAD_DIM, axis=1)

        yp, wk, wv, cv = _prompt_call(yp, sinks, w_in_b, w_o_b, w_up_b, w_down_b, vecs,
                                      w_spatial[l], bias_sp)
        wk_p.append(wk.reshape(B, WINDOW, N_KV_HEADS, HEAD_DIM))
        wv_p.append(wv.reshape(B, WINDOW, N_KV_HEADS, HEAD_DIM))
        cv_p.append(cv)

        gate_tab, gate_bias = _sample_gate_tables(w_spatial[l], b_spatial[l], L)
        ys, wk, wv, cv = _sample_calls(ys, cache_win_k[l], cache_win_v[l], sinks, w_in_b,
                                       _pad_q_weights(w_in_b), w_o_b, w_up_b, w_down_b, vecs,
                                       gate_tab, gate_bias)
        wk_s.append(wk)
        wv_s.append(wv)
        cv_s.append(cv)
    return (yp, ys, jnp.stack(wk_p), jnp.stack(wv_p), jnp.stack(cv_p),
            jnp.stack(wk_s), jnp.stack(wv_s), jnp.stack(cv_s))
```

```python
import functools

import jax
import jax.numpy as jnp
from jax import lax
from jax.experimental import pallas as pl
from jax.experimental.pallas import tpu as pltpu

F32 = jnp.float32
BF16 = jnp.bfloat16

HEAD_DIM = 64
N_CHUNK_HEADS = 8
N_ATTN_HEADS = 8
N_KV_HEADS = 2
Q_PER_KV = N_ATTN_HEADS // N_KV_HEADS
CHUNK_WIDTH = N_CHUNK_HEADS * HEAD_DIM
ATTN_WIDTH = N_ATTN_HEADS * HEAD_DIM
KV_WIDTH = N_KV_HEADS * HEAD_DIM
CHUNK = 128
WINDOW = 128
EPS = 1e-6
ATTN_SCALE = HEAD_DIM ** -0.5

LANES = 128
HALF = LANES // 2
N_PAIRS = ATTN_WIDTH // LANES
VMEM_LIMIT = 56 * 1024 * 1024

PROMPT_TILE = 512
FF_CHUNK = 512
SAMPLE_BB = 16

ROW_G_PRE, ROW_G_PM, ROW_G_PFF, ROW_G_PF, ROW_LN, ROW_GOUT = range(6)


def _dot(a, b):
    return jnp.dot(a, b, preferred_element_type=F32)


def _rms(x, g):
    return x * lax.rsqrt(jnp.mean(x * x, axis=-1, keepdims=True) + EPS) * g


def _layer_norm(x, g, b):
    xc = x - jnp.mean(x, axis=-1, keepdims=True)
    return xc * lax.rsqrt(jnp.mean(xc * xc, axis=-1, keepdims=True) + EPS) * g + b


def _alibi_slope(head):
    return 2.0 ** (-8.0 * (head + 1) / N_ATTN_HEADS)


def _project(x, w_in_ref, vec_ref):
    d = x.shape[-1]
    cw = CHUNK_WIDTH
    h = _rms(x, vec_ref[ROW_G_PRE:ROW_G_PRE + 1, :d]).astype(BF16)
    u = jax.nn.gelu(_dot(h, w_in_ref[:, 0:cw]))
    v = _layer_norm(jax.nn.gelu(_dot(h, w_in_ref[:, cw:2 * cw])),
                    vec_ref[ROW_LN:ROW_LN + 1, 0:cw], vec_ref[ROW_LN:ROW_LN + 1, cw:2 * cw])
    kv0 = 2 * cw + ATTN_WIDTH
    kv = _dot(h, w_in_ref[:, kv0:kv0 + 2 * KV_WIDTH])
    return h, u, v, kv


def _merge(x, a_out, b_out, w_o_ref, vec_ref):
    d = x.shape[-1]
    cw = CHUNK_WIDTH
    ra = _rms(a_out, vec_ref[ROW_GOUT:ROW_GOUT + 1, 0:cw]).astype(BF16)
    rb = _rms(b_out, vec_ref[ROW_GOUT:ROW_GOUT + 1, cw:cw + ATTN_WIDTH]).astype(BF16)
    o = _dot(ra, w_o_ref[0:cw, :]) + _dot(rb, w_o_ref[cw:cw + ATTN_WIDTH, :])
    return x + _rms(o, vec_ref[ROW_G_PM:ROW_G_PM + 1, :d])


def _ffn(x1, w_up_ref, w_down_ref, vec_ref):
    d = x1.shape[-1]
    h2 = _rms(x1, vec_ref[ROW_G_PFF:ROW_G_PFF + 1, :d]).astype(BF16)
    d_ff = w_up_ref.shape[1]
    f = jnp.zeros(x1.shape, F32)
    for c in range(d_ff // FF_CHUNK):
        up = _dot(h2, w_up_ref[:, c * FF_CHUNK:(c + 1) * FF_CHUNK])
        r = jnp.maximum(up, 0.0)
        f = f + _dot((r * r).astype(BF16), w_down_ref[c * FF_CHUNK:(c + 1) * FF_CHUNK, :])
    return x1 + _rms(f, vec_ref[ROW_G_PF:ROW_G_PF + 1, :d])


def _prompt_kernel(n_tiles, nj, sink_ref, x_ref, w_in_ref, w_o_ref, w_up_ref, w_down_ref,
                   vec_ref, wsp_ref, bsp_ref,
                   y_ref, wk_ref, wv_ref, cv_ref,
                   wcat_s, tab_s, kt_s, va_s, vb_s, a_s, bo_s, m_s, x1_s, x1n_s, h_s, q_s, h2_s,
                   act_s, f_s):
    step = pl.program_id(0)
    j = jnp.minimum(step, n_tiles - 1) % nj
    T, d = x_ref.shape[1], x_ref.shape[2]
    nb = T // WINDOW
    cw = CHUNK_WIDTH
    n_chunks = w_up_ref.shape[1] // FF_CHUNK

    lane = lax.broadcasted_iota(jnp.int32, (WINDOW, LANES), 1)
    lo = lane < HALF

    @pl.when(step == 0)
    def _init_tables():
        x1_s[...] = jnp.zeros(x1_s.shape, F32)
        h2_s[...] = jnp.zeros(h2_s.shape, BF16)
        f_s[...] = jnp.zeros(f_s.shape, F32)
        t_idx = lax.broadcasted_iota(jnp.int32, (CHUNK, CHUNK), 0)
        s_idx = lax.broadcasted_iota(jnp.int32, (CHUNK, CHUNK), 1)
        for g in range(N_CHUNK_HEADS):
            wm = jnp.where(s_idx <= t_idx, wsp_ref[g], 0.0).astype(BF16)
            wcat_s[g // 2, :, (g % 2) * CHUNK:(g % 2 + 1) * CHUNK] = wm
        a_idx = lax.broadcasted_iota(jnp.int32, (WINDOW, 2 * WINDOW), 0)
        c_idx = lax.broadcasted_iota(jnp.int32, (WINDOW, 2 * WINDOW), 1)
        dist = WINDOW + a_idx - c_idx
        valid = (dist >= 0) & (dist <= WINDOW)
        for hd in range(N_ATTN_HEADS):
            bias = jnp.where(valid, -(_alibi_slope(hd) * dist.astype(F32)), -jnp.inf)
            tab_s[hd // 2, (hd % 2) * WINDOW:(hd % 2 + 1) * WINDOW, :] = bias

    @pl.when(j == 0)
    def _clear_carry():
        kt_s[:, 0:WINDOW] = jnp.zeros((2 * LANES, WINDOW), BF16)
        va_s[:, 0:WINDOW, :] = jnp.zeros((N_KV_HEADS, WINDOW, LANES), BF16)
        vb_s[:, 0:WINDOW, :] = jnp.zeros((N_KV_HEADS, WINDOW, LANES), BF16)

    def ffn_up(c):
        r = jnp.maximum(_dot(h2_s[...], w_up_ref[:, c * FF_CHUNK:(c + 1) * FF_CHUNK]), 0.0)
        act_s[...] = (r * r).astype(BF16)

    def ffn_down(c):
        part = _dot(act_s[...], w_down_ref[c * FF_CHUNK:(c + 1) * FF_CHUNK, :])
        if c == 0:
            f_s[...] = part
        else:
            f_s[...] += part

    late = iter([functools.partial(fn, c) for c in range(n_chunks // 2, n_chunks)
                 for fn in (ffn_up, ffn_down)])

    def late_stage():
        next(late)()

    late_stage()
    h_s[...] = _rms(x_ref[0], vec_ref[ROW_G_PRE:ROW_G_PRE + 1, :d]).astype(BF16)
    u_pre = _dot(h_s[...], w_in_ref[:, 0:cw])
    v_pre = _dot(h_s[...], w_in_ref[:, cw:2 * cw])
    late_stage()
    u = jax.nn.gelu(u_pre)
    late_stage()
    v = _layer_norm(jax.nn.gelu(v_pre), vec_ref[ROW_LN:ROW_LN + 1, 0:cw],
                    vec_ref[ROW_LN:ROW_LN + 1, cw:2 * cw])
    kv0 = 2 * cw + ATTN_WIDTH
    kv = _dot(h_s[...], w_in_ref[:, kv0:kv0 + 2 * KV_WIDTH])
    k = kv[:, 0:KV_WIDTH]
    val = kv[:, KV_WIDTH:2 * KV_WIDTH]
    q0 = 2 * cw
    q_s[...] = _dot(h_s[...], w_in_ref[:, q0:q0 + ATTN_WIDTH]) * ATTN_SCALE
    late_stage()

    wk_ref[0] = k[T - WINDOW:, :]
    wv_ref[0] = val[T - WINDOW:, :]
    cv_ref[0] = v[T - CHUNK:, :]

    for c in range(nb):
        rows = slice(c * CHUNK, (c + 1) * CHUNK)
        for p in range(cw // LANES):
            cols = slice(p * LANES, (p + 1) * LANES)
            vcp = v[rows, cols]
            rhs = jnp.concatenate([jnp.where(lo, vcp, 0.0), jnp.where(lo, 0.0, vcp)],
                                  axis=0).astype(BF16)
            s_cp = _dot(wcat_s[p], rhs) + bsp_ref[:, cols]
            a_s[rows, cols] = u[rows, cols] * s_cp
    m_s[:, 0:cw] = _rms(a_s[...], vec_ref[ROW_GOUT:ROW_GOUT + 1, 0:cw]).astype(BF16)

    kt = k.T.astype(BF16)
    for hk in range(N_KV_HEADS):
        kth = kt[hk * HEAD_DIM:(hk + 1) * HEAD_DIM, :]
        kt_s[hk * LANES:hk * LANES + HEAD_DIM, WINDOW:] = kth
        kt_s[hk * LANES + HEAD_DIM:(hk + 1) * LANES, WINDOW:] = kth
    lane_t = lax.broadcasted_iota(jnp.int32, (T, LANES), 1)
    lo_t = lane_t < HALF
    val_sw = pltpu.roll(val, HALF, axis=1)
    va_s[0, WINDOW:, :] = jnp.where(lo_t, val, 0.0).astype(BF16)
    vb_s[0, WINDOW:, :] = jnp.where(lo_t, 0.0, val_sw).astype(BF16)
    va_s[1, WINDOW:, :] = jnp.where(lo_t, val_sw, 0.0).astype(BF16)
    vb_s[1, WINDOW:, :] = jnp.where(lo_t, 0.0, val).astype(BF16)

    first_block_bias = jnp.where(j == 0, -jnp.inf, 0.0).astype(F32)
    row2 = lax.broadcasted_iota(jnp.int32, (2 * WINDOW, 1), 0)
    late_after_scores = (5,)
    for n, (i, m) in enumerate((i, m) for i in range(nb) for m in range(N_PAIRS)):
        hk = (2 * m) // Q_PER_KV
        rows = slice(i * WINDOW, (i + 1) * WINDOW)
        keys = slice(i * WINDOW, (i + 2) * WINDOW)
        cols = slice(m * LANES, (m + 1) * LANES)
        qb = q_s[rows, cols]
        qq = jnp.concatenate([jnp.where(lo, qb, 0.0), jnp.where(lo, 0.0, qb)],
                             axis=0).astype(BF16)
        s = _dot(qq, kt_s[hk * LANES:(hk + 1) * LANES, keys]) + tab_s[m]
        if i == 0:
            s = jnp.concatenate([s[:, 0:WINDOW] + first_block_bias, s[:, WINDOW:]], axis=1)
        if n in late_after_scores:
            late_stage()
        sink = jnp.where(row2 < WINDOW, sink_ref[2 * m], sink_ref[2 * m + 1])
        mx = jnp.maximum(jnp.max(s, axis=-1, keepdims=True), sink)
        p_un = jnp.exp(s - mx)
        z = jnp.sum(p_un, axis=-1, keepdims=True) + jnp.exp(sink - mx)
        pb = p_un.astype(BF16)
        o = (_dot(pb[0:WINDOW], va_s[hk, keys, :]) +
             _dot(pb[WINDOW:], vb_s[hk, keys, :]))
        rz = 1.0 / z
        bo_s[rows, cols] = o * jnp.where(lo, rz[0:WINDOW], rz[WINDOW:])

    kt_s[:, 0:WINDOW] = kt_s[:, T:T + WINDOW]
    va_s[:, 0:WINDOW, :] = va_s[:, T:T + WINDOW, :]
    vb_s[:, 0:WINDOW, :] = vb_s[:, T:T + WINDOW, :]

    late_stage()
    m_s[:, cw:cw + ATTN_WIDTH] = _rms(
        bo_s[...], vec_ref[ROW_GOUT:ROW_GOUT + 1, cw:cw + ATTN_WIDTH]).astype(BF16)
    o_proj = _dot(m_s[...], w_o_ref[...])
    late_stage()
    x1 = x_ref[0] + _rms(o_proj, vec_ref[ROW_G_PM:ROW_G_PM + 1, :d])
    x1n_s[...] = x1
    h2_s[...] = _rms(x1, vec_ref[ROW_G_PFF:ROW_G_PFF + 1, :d]).astype(BF16)
    late_stage()

    ffn_up(0)
    y_ref[0] = x1_s[...] + _rms(f_s[...], vec_ref[ROW_G_PF:ROW_G_PF + 1, :d])
    x1_s[...] = x1n_s[...]
    ffn_down(0)
    for c in range(1, n_chunks // 2):
        ffn_up(c)
        ffn_down(c)


def _resident(shape):
    return pl.BlockSpec(shape, lambda *_: (0,) * len(shape), pipeline_mode=pl.Buffered(1))


def _prompt_call(x, sinks, w_in, w_o, w_up, w_down, vecs, w_sp, bias_sp):
    B, S, D = x.shape
    T = PROMPT_TILE
    assert S % T == 0 and T % WINDOW == 0 and WINDOW == CHUNK
    assert (w_up.shape[1] // FF_CHUNK) % 2 == 0
    nj = S // T
    n_tiles = B * nj
    cur = lambda s: jnp.minimum(s, n_tiles - 1)
    prev = lambda s: jnp.maximum(s - 1, 0)
    tile_in = pl.BlockSpec((1, T, D), lambda s: (cur(s) // nj, cur(s) % nj, 0))
    tile_out = pl.BlockSpec((1, T, D), lambda s: (prev(s) // nj, prev(s) % nj, 0))
    last = lambda w: pl.BlockSpec((1, WINDOW, w), lambda s: (cur(s) // nj, 0, 0))
    return pl.pallas_call(
        functools.partial(_prompt_kernel, n_tiles, nj),
        grid=(n_tiles + 1,),
        in_specs=[pl.BlockSpec(memory_space=pltpu.SMEM), tile_in,
                  _resident(w_in.shape), _resident(w_o.shape), _resident(w_up.shape),
                  _resident(w_down.shape), _resident(vecs.shape), _resident(w_sp.shape),
                  _resident(bias_sp.shape)],
        out_specs=[tile_out, last(KV_WIDTH), last(KV_WIDTH), last(CHUNK_WIDTH)],
        out_shape=[jax.ShapeDtypeStruct((B, S, D), F32),
                   jax.ShapeDtypeStruct((B, WINDOW, KV_WIDTH), F32),
                   jax.ShapeDtypeStruct((B, WINDOW, KV_WIDTH), F32),
                   jax.ShapeDtypeStruct((B, CHUNK, CHUNK_WIDTH), F32)],
        scratch_shapes=[pltpu.VMEM((N_CHUNK_HEADS // 2, CHUNK, 2 * CHUNK), BF16),
                        pltpu.VMEM((N_PAIRS, 2 * WINDOW, 2 * WINDOW), F32),
                        pltpu.VMEM((2 * LANES, T + WINDOW), BF16),
                        pltpu.VMEM((N_KV_HEADS, T + WINDOW, LANES), BF16),
                        pltpu.VMEM((N_KV_HEADS, T + WINDOW, LANES), BF16),
                        pltpu.VMEM((T, CHUNK_WIDTH), F32),
                        pltpu.VMEM((T, ATTN_WIDTH), F32),
                        pltpu.VMEM((T, CHUNK_WIDTH + ATTN_WIDTH), BF16),
                        pltpu.VMEM((T, D), F32),
                        pltpu.VMEM((T, D), F32),
                        pltpu.VMEM((T, D), BF16),
                        pltpu.VMEM((T, ATTN_WIDTH), F32),
                        pltpu.VMEM((T, D), BF16),
                        pltpu.VMEM((T, FF_CHUNK), BF16),
                        pltpu.VMEM((T, D), F32)],
        compiler_params=pltpu.CompilerParams(
            dimension_semantics=("arbitrary",), vmem_limit_bytes=VMEM_LIMIT),
        name="prompt_layer",
    )(sinks, x, w_in, w_o, w_up, w_down, vecs, w_sp, bias_sp)


def _sample_proj_kernel(L, x_ref, w_in_ref, wq_ref, vec_ref, tab_ref, bs_ref,
                        a_ref, v_ref, q_ref, kv_ref):
    x = x_ref[...]
    n = x.shape[0]
    h, u, v, kv = _project(x, w_in_ref, vec_ref)
    v_ref[...] = v
    kv_ref[...] = kv
    q_ref[...] = _dot(h, wq_ref[...]) * ATTN_SCALE
    t_of_row = lax.broadcasted_iota(jnp.int32, (n, 1), 0) % L
    s = jnp.zeros(v.shape, F32)
    for t in range(L):
        s = jnp.where(t_of_row == t, bs_ref[t:t + 1, :], s)
    for delta in range(L):
        coef = jnp.zeros(v.shape, F32)
        for t in range(delta, L):
            coef = jnp.where(t_of_row == t, tab_ref[delta * L + t:delta * L + t + 1, :], coef)
        vs = v if delta == 0 else pltpu.roll(v, delta, axis=0)
        s = s + coef * vs
    a_ref[...] = u * s


def _sample_attn_kernel(q_ref, ck_ref, cv_ref, kn_ref, vn_ref, info_ref,
                        o_ref, wk_ref, wv_ref):
    bb_n, W, _ = ck_ref.shape
    L = q_ref.shape[1] // N_ATTN_HEADS
    npad = kn_ref.shape[1]
    nk = 2 * W
    r_idx = lax.broadcasted_iota(jnp.int32, (N_ATTN_HEADS * L, nk), 0)
    c_idx = lax.broadcasted_iota(jnp.int32, (N_ATTN_HEADS * L, nk), 1)
    i_idx = r_idx % L
    dist = jnp.where(c_idx < W, W + i_idx - c_idx, i_idx - (c_idx - W))
    valid = (dist >= 0) & (dist <= WINDOW) & (c_idx < W + L)
    sink = info_ref[:, 0:1]
    slope = info_ref[:, 1:2]
    tab = jnp.where(valid, -(slope * dist.astype(F32)), -jnp.inf)
    pad = jnp.zeros((nk - W - npad, LANES), F32)
    lane = lax.broadcasted_iota(jnp.int32, (N_ATTN_HEADS * L, LANES), 1)
    row = lax.broadcasted_iota(jnp.int32, (N_ATTN_HEADS * L, LANES), 0)
    own_half = (lane < HALF) == (row < Q_PER_KV * L)

    def body(bb, carry):
        kf = jnp.concatenate([ck_ref[bb], kn_ref[bb], pad], axis=0)
        vf = jnp.concatenate([cv_ref[bb], vn_ref[bb], pad], axis=0)
        s = lax.dot_general(q_ref[bb].astype(BF16), kf.astype(BF16),
                            (((1,), (1,)), ((), ())), preferred_element_type=F32) + tab
        mx = jnp.maximum(jnp.max(s, axis=-1, keepdims=True), sink)
        p_un = jnp.exp(s - mx)
        z = jnp.sum(p_un, axis=-1, keepdims=True) + jnp.exp(sink - mx)
        o = _dot(p_un.astype(BF16), vf.astype(BF16)) * (1.0 / z)
        o_ref[bb] = jnp.where(own_half, o, 0.0)
        wk_ref[bb] = pltpu.roll(kf, nk - L, axis=0)[0:W]
        wv_ref[bb] = pltpu.roll(vf, nk - L, axis=0)[0:W]
        return carry

    lax.fori_loop(0, bb_n, body, 0)


def _sample_ffn_kernel(x_ref, a_ref, b_ref, w_o_ref, w_up_ref, w_down_ref, vec_ref, y_ref):
    x1 = _merge(x_ref[...], a_ref[...], b_ref[...], w_o_ref, vec_ref)
    y_ref[...] = _ffn(x1, w_up_ref, w_down_ref, vec_ref)


def _whole(shape):
    return pl.BlockSpec(shape, lambda *_: (0,) * len(shape))


def _sample_calls(x, cache_k, cache_v, sinks, w_in, wq_pad, w_o, w_up, w_down, vecs,
                  gate_tab, gate_bias):
    Bd, L, D = x.shape
    W = cache_k.shape[1]
    n = Bd * L
    xs = x.reshape(n, D)
    params = pltpu.CompilerParams(vmem_limit_bytes=VMEM_LIMIT)

    outs = [jax.ShapeDtypeStruct((n, CHUNK_WIDTH), F32), jax.ShapeDtypeStruct((n, CHUNK_WIDTH), F32),
            jax.ShapeDtypeStruct((n, N_ATTN_HEADS * LANES), F32),
            jax.ShapeDtypeStruct((n, 2 * KV_WIDTH), F32)]
    ins = (xs, w_in, wq_pad, vecs, gate_tab, gate_bias)
    a_out, v_n, q_pad, kv = pl.pallas_call(
        functools.partial(_sample_proj_kernel, L), grid=(1,),
        in_specs=[_whole(t.shape) for t in ins],
        out_specs=[_whole(o.shape) for o in outs], out_shape=outs,
        compiler_params=params, name="sample_proj")(*ins)

    q_st = q_pad.reshape(Bd, L, N_ATTN_HEADS, LANES).transpose(0, 2, 1, 3)
    q_st = q_st.reshape(Bd, N_ATTN_HEADS * L, LANES)
    npad = 8
    new_k = jnp.pad(kv[:, 0:KV_WIDTH].reshape(Bd, L, KV_WIDTH), ((0, 0), (0, npad - L), (0, 0)))
    new_v = jnp.pad(kv[:, KV_WIDTH:].reshape(Bd, L, KV_WIDTH), ((0, 0), (0, npad - L), (0, 0)))
    ck = cache_k.reshape(Bd, W, KV_WIDTH)
    cv = cache_v.reshape(Bd, W, KV_WIDTH)
    slopes = jnp.asarray([_alibi_slope(hd) for hd in range(N_ATTN_HEADS)], F32)
    info = jnp.zeros((N_ATTN_HEADS * L, LANES), F32)
    info = info.at[:, 0].set(jnp.repeat(sinks, L)).at[:, 1].set(jnp.repeat(slopes, L))

    bb = SAMPLE_BB
    blk = lambda s1, s2: pl.BlockSpec((bb, s1, s2), lambda g: (g, 0, 0))
    o_st, wk_s, wv_s = pl.pallas_call(
        _sample_attn_kernel, grid=(Bd // bb,),
        in_specs=[blk(N_ATTN_HEADS * L, LANES), blk(W, KV_WIDTH), blk(W, KV_WIDTH),
                  blk(npad, KV_WIDTH), blk(npad, KV_WIDTH), _whole(info.shape)],
        out_specs=[blk(N_ATTN_HEADS * L, LANES), blk(W, KV_WIDTH), blk(W, KV_WIDTH)],
        out_shape=[jax.ShapeDtypeStruct((Bd, N_ATTN_HEADS * L, LANES), F32),
                   jax.ShapeDtypeStruct((Bd, W, KV_WIDTH), F32),
                   jax.ShapeDtypeStruct((Bd, W, KV_WIDTH), F32)],
        compiler_params=pltpu.CompilerParams(dimension_semantics=("arbitrary",),
                                             vmem_limit_bytes=VMEM_LIMIT),
        name="sample_attn")(q_st, ck, cv, new_k, new_v, info)

    o5 = o_st.reshape(Bd, N_ATTN_HEADS, L, N_KV_HEADS, HEAD_DIM)
    b_out = (o5[:, :, :, 0, :] + o5[:, :, :, 1, :]).transpose(0, 2, 1, 3).reshape(n, ATTN_WIDTH)

    ins = (xs, a_out, b_out, w_o, w_up, w_down, vecs)
    ys = pl.pallas_call(
        _sample_ffn_kernel, grid=(1,),
        in_specs=[_whole(t.shape) for t in ins],
        out_specs=_whole((n, D)), out_shape=jax.ShapeDtypeStruct((n, D), F32),
        compiler_params=params, name="sample_ffn")(*ins)

    return (ys.reshape(Bd, L, D), wk_s.reshape(cache_k.shape), wv_s.reshape(cache_v.shape),
            v_n.reshape(Bd, L, CHUNK_WIDTH))


def _pack_vectors(d, g_pre, g_pm, g_pff, g_pf, ln_g, ln_b, g_oc, g_oa):
    vecs = jnp.zeros((8, d), F32)
    vecs = vecs.at[ROW_G_PRE].set(g_pre).at[ROW_G_PM].set(g_pm)
    vecs = vecs.at[ROW_G_PFF].set(g_pff).at[ROW_G_PF].set(g_pf)
    vecs = vecs.at[ROW_LN, 0:2 * CHUNK_WIDTH].set(jnp.concatenate([ln_g, ln_b]))
    vecs = vecs.at[ROW_GOUT, 0:CHUNK_WIDTH + ATTN_WIDTH].set(jnp.concatenate([g_oc, g_oa]))
    return vecs


def _pad_q_weights(w_in):
    d = w_in.shape[0]
    wq = w_in[:, 2 * CHUNK_WIDTH:2 * CHUNK_WIDTH + ATTN_WIDTH].reshape(d, N_ATTN_HEADS, HEAD_DIM)
    half = (jnp.arange(N_ATTN_HEADS) // Q_PER_KV)[None, :, None, None]
    sel = (half == jnp.arange(N_KV_HEADS)[None, None, :, None])
    padded = jnp.where(sel, wq[:, :, None, :], 0).astype(w_in.dtype)
    return padded.reshape(d, N_ATTN_HEADS * LANES)


def _sample_gate_tables(w_sp, b_sp, L):
    rows = []
    for delta in range(L):
        for t in range(L):
            if t >= delta:
                rows.append(jnp.repeat(w_sp[:, t, t - delta], HEAD_DIM))
            else:
                rows.append(jnp.zeros((CHUNK_WIDTH,), F32))
    tab = jnp.stack(rows)
    bias = jnp.repeat(b_sp[:, :L].T, HEAD_DIM, axis=1)
    pad = (-L) % 8
    return tab, jnp.pad(bias, ((0, pad), (0, 0)))


def kernel(x_prompt, x_sample, cache_win_k, cache_win_v, w_in, g_pre_mix, ln_v_g, ln_v_b,
           w_spatial, b_spatial, attn_sinks, g_out_chunk, g_out_attn, w_o, g_post_mix,
           g_pre_ffn, w_up, w_down, g_post_ffn):
    depth = w_in.shape[0]
    B, S, D = x_prompt.shape
    Bd, L, _ = x_sample.shape
    yp, ys = x_prompt, x_sample
    wk_p, wv_p, cv_p, wk_s, wv_s, cv_s = [], [], [], [], [], []
    for l in range(depth):
        vecs = _pack_vectors(D, g_pre_mix[l], g_post_mix[l], g_pre_ffn[l], g_post_ffn[l],
                             ln_v_g[l], ln_v_b[l], g_out_chunk[l], g_out_attn[l])
        w_in_b = w_in[l].astype(BF16)
        w_o_b = w_o[l].astype(BF16)
        w_up_b = w_up[l].astype(BF16)
        w_down_b = w_down[l].astype(BF16)
        sinks = attn_sinks[l].reshape(N_ATTN_HEADS).astype(F32)
        bias_sp = jnp.repeat(b_spatial[l].T, HEAD_DIM, axis=1)

        yp, wk, wv, cv = _prompt_call(yp, sinks, w_in_b, w_o_b, w_up_b, w_down_b, vecs,
                                      w_spatial[l], bias_sp)
        wk_p.append(wk.reshape(B, WINDOW, N_KV_HEADS, HEAD_DIM))
        wv_p.append(wv.reshape(B, WINDOW, N_KV_HEADS, HEAD_DIM))
        cv_p.append(cv)

        gate_tab, gate_bias = _sample_gate_tables(w_spatial[l], b_spatial[l], L)
        ys, wk, wv, cv = _sample_calls(ys, cache_win_k[l], cache_win_v[l], sinks, w_in_b,
                                       _pad_q_weights(w_in_b), w_o_b, w_up_b, w_down_b, vecs,
                                       gate_tab, gate_bias)
        wk_s.append(wk)
        wv_s.append(wv)
        cv_s.append(cv)
    return (yp, ys, jnp.stack(wk_p), jnp.stack(wv_p), jnp.stack(cv_p),
            jnp.stack(wk_s), jnp.stack(wv_s), jnp.stack(cv_s))
```

```python
import functools

import jax
import jax.numpy as jnp
from jax import lax
from jax.experimental import pallas as pl
from jax.experimental.pallas import tpu as pltpu

F32 = jnp.float32
BF16 = jnp.bfloat16

HEAD_DIM = 64
N_CHUNK_HEADS = 8
N_ATTN_HEADS = 8
N_KV_HEADS = 2
Q_PER_KV = N_ATTN_HEADS // N_KV_HEADS
CHUNK_WIDTH = N_CHUNK_HEADS * HEAD_DIM
ATTN_WIDTH = N_ATTN_HEADS * HEAD_DIM
KV_WIDTH = N_KV_HEADS * HEAD_DIM
CHUNK = 128
WINDOW = 128
EPS = 1e-6
ATTN_SCALE = HEAD_DIM ** -0.5

LANES = 128
HALF = LANES // 2
N_PAIRS = ATTN_WIDTH // LANES
VMEM_LIMIT = 56 * 1024 * 1024

PROMPT_TILE = 512
FF_CHUNK = 512
SAMPLE_BB = 16

ROW_G_PRE, ROW_G_PM, ROW_G_PFF, ROW_G_PF, ROW_LN, ROW_GOUT = range(6)


def _dot(a, b):
    return jnp.dot(a, b, preferred_element_type=F32)


def _rms(x, g):
    return x * lax.rsqrt(jnp.mean(x * x, axis=-1, keepdims=True) + EPS) * g


def _layer_norm(x, g, b):
    xc = x - jnp.mean(x, axis=-1, keepdims=True)
    return xc * lax.rsqrt(jnp.mean(xc * xc, axis=-1, keepdims=True) + EPS) * g + b


def _alibi_slope(head):
    return 2.0 ** (-8.0 * (head + 1) / N_ATTN_HEADS)


def _merge(x, a_out, b_out, w_o_ref, vec_ref):
    d = x.shape[-1]
    cw = CHUNK_WIDTH
    ra = _rms(a_out, vec_ref[ROW_GOUT:ROW_GOUT + 1, 0:cw]).astype(BF16)
    rb = _rms(b_out, vec_ref[ROW_GOUT:ROW_GOUT + 1, cw:cw + ATTN_WIDTH]).astype(BF16)
    o = _dot(ra, w_o_ref[0:cw, :]) + _dot(rb, w_o_ref[cw:cw + ATTN_WIDTH, :])
    return x + _rms(o, vec_ref[ROW_G_PM:ROW_G_PM + 1, :d])


def _ffn(x1, w_up_ref, w_down_ref, vec_ref):
    d = x1.shape[-1]
    h2 = _rms(x1, vec_ref[ROW_G_PFF:ROW_G_PFF + 1, :d]).astype(BF16)
    d_ff = w_up_ref.shape[1]
    f = jnp.zeros(x1.shape, F32)
    for c in range(d_ff // FF_CHUNK):
        up = _dot(h2, w_up_ref[:, c * FF_CHUNK:(c + 1) * FF_CHUNK])
        r = jnp.maximum(up, 0.0)
        f = f + _dot((r * r).astype(BF16), w_down_ref[c * FF_CHUNK:(c + 1) * FF_CHUNK, :])
    return x1 + _rms(f, vec_ref[ROW_G_PF:ROW_G_PF + 1, :d])


def _prompt_kernel(n_tiles, nj, sink_ref, x_ref, w_in_ref, w_o_ref, w_up_ref, w_down_ref,
                   vec_ref, wsp_ref, bsp_ref,
                   y_ref, wk_ref, wv_ref, cv_ref,
                   wcat_s, tab_s, kt_s, va_s, vb_s, a_s, bo_s, m_s, x1_s, x1n_s, h_s, q_s, h2_s,
                   act_s, f_s):
    step = pl.program_id(0)
    j = jnp.minimum(step, n_tiles - 1) % nj
    T, d = x_ref.shape[1], x_ref.shape[2]
    nb = T // WINDOW
    cw = CHUNK_WIDTH
    n_chunks = w_up_ref.shape[1] // FF_CHUNK

    lane = lax.broadcasted_iota(jnp.int32, (WINDOW, LANES), 1)
    lo = lane < HALF

    @pl.when(step == 0)
    def _init_tables():
        x1_s[...] = jnp.zeros(x1_s.shape, F32)
        h2_s[...] = jnp.zeros(h2_s.shape, BF16)
        f_s[...] = jnp.zeros(f_s.shape, F32)
        t_idx = lax.broadcasted_iota(jnp.int32, (CHUNK, CHUNK), 0)
        s_idx = lax.broadcasted_iota(jnp.int32, (CHUNK, CHUNK), 1)
        for g in range(N_CHUNK_HEADS):
            wm = jnp.where(s_idx <= t_idx, wsp_ref[g], 0.0).astype(BF16)
            wcat_s[g // 2, :, (g % 2) * CHUNK:(g % 2 + 1) * CHUNK] = wm
        a_idx = lax.broadcasted_iota(jnp.int32, (WINDOW, 2 * WINDOW), 0)
        c_idx = lax.broadcasted_iota(jnp.int32, (WINDOW, 2 * WINDOW), 1)
        dist = WINDOW + a_idx - c_idx
        valid = (dist >= 0) & (dist <= WINDOW)
        for hd in range(N_ATTN_HEADS):
            bias = jnp.where(valid, -(_alibi_slope(hd) * dist.astype(F32)), -jnp.inf)
            tab_s[hd // 2, (hd % 2) * WINDOW:(hd % 2 + 1) * WINDOW, :] = bias

    @pl.when(j == 0)
    def _clear_carry():
        kt_s[:, 0:WINDOW] = jnp.zeros((2 * LANES, WINDOW), BF16)
        va_s[:, 0:WINDOW, :] = jnp.zeros((N_KV_HEADS, WINDOW, LANES), BF16)
        vb_s[:, 0:WINDOW, :] = jnp.zeros((N_KV_HEADS, WINDOW, LANES), BF16)

    def ffn_up(c):
        r = jnp.maximum(_dot(h2_s[...], w_up_ref[:, c * FF_CHUNK:(c + 1) * FF_CHUNK]), 0.0)
        act_s[...] = (r * r).astype(BF16)

    def ffn_down(c):
        part = _dot(act_s[...], w_down_ref[c * FF_CHUNK:(c + 1) * FF_CHUNK, :])
        if c == 0:
            f_s[...] = part
        else:
            f_s[...] += part

    late = iter([functools.partial(fn, c) for c in range(n_chunks // 2, n_chunks)
                 for fn in (ffn_up, ffn_down)])

    def late_stage():
        next(late)()

    late_stage()
    h_s[...] = _rms(x_ref[0], vec_ref[ROW_G_PRE:ROW_G_PRE + 1, :d]).astype(BF16)
    u_pre = _dot(h_s[...], w_in_ref[:, 0:cw])
    v_pre = _dot(h_s[...], w_in_ref[:, cw:2 * cw])
    late_stage()
    u = jax.nn.gelu(u_pre)
    late_stage()
    v = _layer_norm(jax.nn.gelu(v_pre), vec_ref[ROW_LN:ROW_LN + 1, 0:cw],
                    vec_ref[ROW_LN:ROW_LN + 1, cw:2 * cw])
    kv0 = 2 * cw + ATTN_WIDTH
    kv = _dot(h_s[...], w_in_ref[:, kv0:kv0 + 2 * KV_WIDTH])
    k = kv[:, 0:KV_WIDTH]
    val = kv[:, KV_WIDTH:2 * KV_WIDTH]
    q0 = 2 * cw
    q_s[...] = _dot(h_s[...], w_in_ref[:, q0:q0 + ATTN_WIDTH]) * ATTN_SCALE
    late_stage()

    kt_f = k.T
    wk_ref[0] = kt_f[:, T - WINDOW:]
    wv_ref[0] = val[T - WINDOW:, :].T
    cv_ref[0] = v[T - CHUNK:, :]

    for c in range(nb):
        rows = slice(c * CHUNK, (c + 1) * CHUNK)
        for p in range(cw // LANES):
            cols = slice(p * LANES, (p + 1) * LANES)
            vcp = v[rows, cols]
            rhs = jnp.concatenate([jnp.where(lo, vcp, 0.0), jnp.where(lo, 0.0, vcp)],
                                  axis=0).astype(BF16)
            s_cp = _dot(wcat_s[p], rhs) + bsp_ref[:, cols]
            a_s[rows, cols] = u[rows, cols] * s_cp
    m_s[:, 0:cw] = _rms(a_s[...], vec_ref[ROW_GOUT:ROW_GOUT + 1, 0:cw]).astype(BF16)

    kt = kt_f.astype(BF16)
    for hk in range(N_KV_HEADS):
        kth = kt[hk * HEAD_DIM:(hk + 1) * HEAD_DIM, :]
        kt_s[hk * LANES:hk * LANES + HEAD_DIM, WINDOW:] = kth
        kt_s[hk * LANES + HEAD_DIM:(hk + 1) * LANES, WINDOW:] = kth
    lane_t = lax.broadcasted_iota(jnp.int32, (T, LANES), 1)
    lo_t = lane_t < HALF
    val_sw = pltpu.roll(val, HALF, axis=1)
    va_s[0, WINDOW:, :] = jnp.where(lo_t, val, 0.0).astype(BF16)
    vb_s[0, WINDOW:, :] = jnp.where(lo_t, 0.0, val_sw).astype(BF16)
    va_s[1, WINDOW:, :] = jnp.where(lo_t, val_sw, 0.0).astype(BF16)
    vb_s[1, WINDOW:, :] = jnp.where(lo_t, 0.0, val).astype(BF16)

    first_block_bias = jnp.where(j == 0, -jnp.inf, 0.0).astype(F32)
    row2 = lax.broadcasted_iota(jnp.int32, (2 * WINDOW, 1), 0)
    late_after_scores = (5,)
    for n, (i, m) in enumerate((i, m) for i in range(nb) for m in range(N_PAIRS)):
        hk = (2 * m) // Q_PER_KV
        rows = slice(i * WINDOW, (i + 1) * WINDOW)
        keys = slice(i * WINDOW, (i + 2) * WINDOW)
        cols = slice(m * LANES, (m + 1) * LANES)
        qb = q_s[rows, cols]
        qq = jnp.concatenate([jnp.where(lo, qb, 0.0), jnp.where(lo, 0.0, qb)],
                             axis=0).astype(BF16)
        s = _dot(qq, kt_s[hk * LANES:(hk + 1) * LANES, keys]) + tab_s[m]
        if i == 0:
            s = jnp.concatenate([s[:, 0:WINDOW] + first_block_bias, s[:, WINDOW:]], axis=1)
        if n in late_after_scores:
            late_stage()
        sink = jnp.where(row2 < WINDOW, sink_ref[2 * m], sink_ref[2 * m + 1])
        mx = jnp.maximum(jnp.max(s, axis=-1, keepdims=True), sink)
        p_un = jnp.exp(s - mx)
        z = jnp.sum(p_un, axis=-1, keepdims=True) + jnp.exp(sink - mx)
        pb = p_un.astype(BF16)
        o = (_dot(pb[0:WINDOW], va_s[hk, keys, :]) +
             _dot(pb[WINDOW:], vb_s[hk, keys, :]))
        rz = 1.0 / z
        bo_s[rows, cols] = o * jnp.where(lo, rz[0:WINDOW], rz[WINDOW:])

    kt_s[:, 0:WINDOW] = kt_s[:, T:T + WINDOW]
    va_s[:, 0:WINDOW, :] = va_s[:, T:T + WINDOW, :]
    vb_s[:, 0:WINDOW, :] = vb_s[:, T:T + WINDOW, :]

    late_stage()
    m_s[:, cw:cw + ATTN_WIDTH] = _rms(
        bo_s[...], vec_ref[ROW_GOUT:ROW_GOUT + 1, cw:cw + ATTN_WIDTH]).astype(BF16)
    o_proj = _dot(m_s[...], w_o_ref[...])
    late_stage()
    x1 = x_ref[0] + _rms(o_proj, vec_ref[ROW_G_PM:ROW_G_PM + 1, :d])
    x1n_s[...] = x1
    h2_s[...] = _rms(x1, vec_ref[ROW_G_PFF:ROW_G_PFF + 1, :d]).astype(BF16)
    late_stage()

    ffn_up(0)
    y_ref[0] = x1_s[...] + _rms(f_s[...], vec_ref[ROW_G_PF:ROW_G_PF + 1, :d])
    x1_s[...] = x1n_s[...]
    ffn_down(0)
    for c in range(1, n_chunks // 2):
        ffn_up(c)
        ffn_down(c)


def _resident(shape):
    return pl.BlockSpec(shape, lambda *_: (0,) * len(shape), pipeline_mode=pl.Buffered(1))


def _prompt_call(x, sinks, w_in, w_o, w_up, w_down, vecs, w_sp, bias_sp):
    B, S, D = x.shape
    T = PROMPT_TILE
    assert S % T == 0 and T % WINDOW == 0 and WINDOW == CHUNK
    assert (w_up.shape[1] // FF_CHUNK) % 2 == 0
    nj = S // T
    n_tiles = B * nj
    cur = lambda s: jnp.minimum(s, n_tiles - 1)
    prev = lambda s: jnp.maximum(s - 1, 0)
    tile_in = pl.BlockSpec((1, T, D), lambda s: (cur(s) // nj, cur(s) % nj, 0))
    tile_out = pl.BlockSpec((1, T, D), lambda s: (prev(s) // nj, prev(s) % nj, 0))
    last = lambda r, c: pl.BlockSpec((1, r, c), lambda s: (cur(s) // nj, 0, 0))
    return pl.pallas_call(
        functools.partial(_prompt_kernel, n_tiles, nj),
        grid=(n_tiles + 1,),
        in_specs=[pl.BlockSpec(memory_space=pltpu.SMEM), tile_in,
                  _resident(w_in.shape), _resident(w_o.shape), _resident(w_up.shape),
                  _resident(w_down.shape), _resident(vecs.shape), _resident(w_sp.shape),
                  _resident(bias_sp.shape)],
        out_specs=[tile_out, last(KV_WIDTH, WINDOW), last(KV_WIDTH, WINDOW),
                   last(CHUNK, CHUNK_WIDTH)],
        out_shape=[jax.ShapeDtypeStruct((B, S, D), F32),
                   jax.ShapeDtypeStruct((B, KV_WIDTH, WINDOW), F32),
                   jax.ShapeDtypeStruct((B, KV_WIDTH, WINDOW), F32),
                   jax.ShapeDtypeStruct((B, CHUNK, CHUNK_WIDTH), F32)],
        scratch_shapes=[pltpu.VMEM((N_CHUNK_HEADS // 2, CHUNK, 2 * CHUNK), BF16),
                        pltpu.VMEM((N_PAIRS, 2 * WINDOW, 2 * WINDOW), F32),
                        pltpu.VMEM((2 * LANES, T + WINDOW), BF16),
                        pltpu.VMEM((N_KV_HEADS, T + WINDOW, LANES), BF16),
                        pltpu.VMEM((N_KV_HEADS, T + WINDOW, LANES), BF16),
                        pltpu.VMEM((T, CHUNK_WIDTH), F32),
                        pltpu.VMEM((T, ATTN_WIDTH), F32),
                        pltpu.VMEM((T, CHUNK_WIDTH + ATTN_WIDTH), BF16),
                        pltpu.VMEM((T, D), F32),
                        pltpu.VMEM((T, D), F32),
                        pltpu.VMEM((T, D), BF16),
                        pltpu.VMEM((T, ATTN_WIDTH), F32),
                        pltpu.VMEM((T, D), BF16),
                        pltpu.VMEM((T, FF_CHUNK), BF16),
                        pltpu.VMEM((T, D), F32)],
        compiler_params=pltpu.CompilerParams(
            dimension_semantics=("arbitrary",), vmem_limit_bytes=VMEM_LIMIT),
        name="prompt_layer",
    )(sinks, x, w_in, w_o, w_up, w_down, vecs, w_sp, bias_sp)


def _sample_kernel(L, x_ref, ckt_ref, cvt_ref, w_in_ref, wq_ref, wkvt_ref, w_o_ref, w_up_ref,
                   w_down_ref, vec_ref, tab_ref, bs_ref, info_ref,
                   y_ref, vn_ref, wkt_ref, wvt_ref,
                   q_s, kvt_s, a_s, b_s):
    g = pl.program_id(0)
    n_steps = pl.num_programs(0)
    bb, hd, W = ckt_ref.shape
    n, d = x_ref.shape
    n_heads = N_ATTN_HEADS
    pair_rows = 2 * L
    assert pair_rows == 8 and hd == LANES and W == LANES
    batch_per_block = LANES // L

    @pl.when(g == 0)
    def _project_and_gate():
        cw = CHUNK_WIDTH
        h = _rms(x_ref[...], vec_ref[ROW_G_PRE:ROW_G_PRE + 1, :d]).astype(BF16)
        u = jax.nn.gelu(_dot(h, w_in_ref[:, 0:cw]))
        v = _layer_norm(jax.nn.gelu(_dot(h, w_in_ref[:, cw:2 * cw])),
                        vec_ref[ROW_LN:ROW_LN + 1, 0:cw], vec_ref[ROW_LN:ROW_LN + 1, cw:2 * cw])
        vn_ref[...] = v
        q_s[...] = _dot(h, wq_ref[...]) * ATTN_SCALE
        kvt_s[...] = lax.dot_general(wkvt_ref[...], h, (((1,), (1,)), ((), ())),
                                     preferred_element_type=F32)
        t_of_row = lax.broadcasted_iota(jnp.int32, (n, 1), 0) % L
        s = jnp.zeros(v.shape, F32)
        for t in range(L):
            s = jnp.where(t_of_row == t, bs_ref[t:t + 1, :], s)
        for delta in range(L):
            coef = jnp.zeros(v.shape, F32)
            for t in range(delta, L):
                coef = jnp.where(t_of_row == t, tab_ref[delta * L + t:delta * L + t + 1, :], coef)
            vs = v if delta == 0 else pltpu.roll(v, delta, axis=0)
            s = s + coef * vs
        a_s[...] = u * s

    rows_all = n_heads * pair_rows
    r_idx = lax.broadcasted_iota(jnp.int32, (rows_all, LANES), 0)
    l_idx = lax.broadcasted_iota(jnp.int32, (rows_all, LANES), 1)
    t_idx = r_idx % L
    odd = (r_idx % pair_rows) >= L
    sink = info_ref[:, 0:1]
    slope = info_ref[:, 1:2]
    dist_old = W + t_idx - l_idx
    bias_old = jnp.where(dist_old <= WINDOW, -(slope * dist_old.astype(F32)), -jnp.inf)
    lane8 = lax.broadcasted_iota(jnp.int32, (pair_rows, LANES), 1)
    lo8 = lane8 < HALF
    lane_w = lax.broadcasted_iota(jnp.int32, (hd, W), 1)

    first_batch = g * bb
    new_block = pl.multiple_of((first_batch // batch_per_block) * LANES, LANES)
    kt_new = kvt_s[0:KV_WIDTH, pl.ds(new_block, LANES)]
    vt_new = kvt_s[KV_WIDTH:2 * KV_WIDTH, pl.ds(new_block, LANES)]
    kt_new_b = kt_new.astype(BF16)
    vt_new_b = vt_new.astype(BF16)
    for p in range(bb // 2):
        off = ((first_batch + 2 * p) % batch_per_block) * L
        row0 = pl.multiple_of((first_batch + 2 * p) * L, pair_rows)
        lhs = jnp.concatenate(
            [q_s[pl.ds(row0, pair_rows), hh * LANES:(hh + 1) * LANES] for hh in range(n_heads)],
            axis=0).astype(BF16)
        s_pair = []
        for e in range(2):
            rhs = jnp.concatenate([ckt_ref[2 * p + e].astype(BF16), kt_new_b], axis=1)
            s_pair.append(_dot(lhs, rhs))
        sc = jnp.where(jnp.concatenate([odd, odd], axis=1), s_pair[1], s_pair[0])
        rel = l_idx - off - jnp.where(odd, L, 0)
        bias_new = jnp.where((rel >= 0) & (rel <= t_idx),
                             -(slope * (t_idx - rel).astype(F32)), -jnp.inf)
        sc = sc + jnp.concatenate([bias_old, bias_new], axis=1)
        mx = jnp.maximum(jnp.max(sc, axis=-1, keepdims=True), sink)
        p_un = jnp.exp(sc - mx)
        z = jnp.sum(p_un, axis=-1, keepdims=True) + jnp.exp(sink - mx)
        pb = p_un.astype(BF16)
        o_pair = []
        for e in range(2):
            vt = jnp.concatenate([cvt_ref[2 * p + e].astype(BF16), vt_new_b], axis=1)
            o_pair.append(lax.dot_general(pb, vt, (((1,), (1,)), ((), ())),
                                          preferred_element_type=F32))
        o = jnp.where(odd, o_pair[1], o_pair[0]) * (1.0 / z)
        for m in range(N_PAIRS):
            kv_head = (2 * m) // Q_PER_KV
            o_even = o[(2 * m) * pair_rows:(2 * m + 1) * pair_rows]
            o_odd = o[(2 * m + 1) * pair_rows:(2 * m + 2) * pair_rows]
            if kv_head == 0:
                o_odd = pltpu.roll(o_odd, HALF, axis=1)
            else:
                o_even = pltpu.roll(o_even, HALF, axis=1)
            b_s[pl.ds(row0, pair_rows), m * LANES:(m + 1) * LANES] = jnp.where(lo8, o_even, o_odd)
        for e in range(2):
            shift_new = (W - L) - (off + e * L)
            keep = lane_w < W - L
            wkt_ref[2 * p + e] = jnp.where(keep, pltpu.roll(ckt_ref[2 * p + e], W - L, axis=1),
                                           pltpu.roll(kt_new, shift_new, axis=1))
            wvt_ref[2 * p + e] = jnp.where(keep, pltpu.roll(cvt_ref[2 * p + e], W - L, axis=1),
                                           pltpu.roll(vt_new, shift_new, axis=1))

    @pl.when(g == n_steps - 1)
    def _merge_and_ffn():
        x1 = _merge(x_ref[...], a_s[...], b_s[...], w_o_ref, vec_ref)
        y_ref[...] = _ffn(x1, w_up_ref, w_down_ref, vec_ref)


def _sample_call(x, cache_kt, cache_vt, w_in, wq_pad, wkv_t, w_o, w_up, w_down, vecs,
                 gate_tab, gate_bias, info):
    Bd, L, D = x.shape
    hd, W = cache_kt.shape[1:]
    n = Bd * L
    bb = SAMPLE_BB
    assert Bd % bb == 0 and bb % 2 == 0 and (LANES // L) % bb == 0
    xs = x.reshape(n, D)
    blk = pl.BlockSpec((bb, hd, W), lambda g: (g, 0, 0))
    const = lambda shape: pl.BlockSpec(shape, lambda g: (0,) * len(shape))
    weights = (w_in, wq_pad, wkv_t, w_o, w_up, w_down, vecs, gate_tab, gate_bias, info)
    return pl.pallas_call(
        functools.partial(_sample_kernel, L),
        grid=(Bd // bb,),
        in_specs=[_resident(xs.shape), blk, blk] + [_resident(w.shape) for w in weights],
        out_specs=[const((n, D)), const((n, CHUNK_WIDTH)), blk, blk],
        out_shape=[jax.ShapeDtypeStruct((n, D), F32),
                   jax.ShapeDtypeStruct((n, CHUNK_WIDTH), F32),
                   jax.ShapeDtypeStruct(cache_kt.shape, F32),
                   jax.ShapeDtypeStruct(cache_vt.shape, F32)],
        scratch_shapes=[pltpu.VMEM((n, N_ATTN_HEADS * LANES), F32),
                        pltpu.VMEM((2 * KV_WIDTH, n), F32),
                        pltpu.VMEM((n, CHUNK_WIDTH), F32),
                        pltpu.VMEM((n, ATTN_WIDTH), F32)],
        compiler_params=pltpu.CompilerParams(
            dimension_semantics=("arbitrary",), vmem_limit_bytes=VMEM_LIMIT),
        name="sample_layer",
    )(xs, cache_kt, cache_vt, *weights)


def _pack_vectors(d, g_pre, g_pm, g_pff, g_pf, ln_g, ln_b, g_oc, g_oa):
    vecs = jnp.zeros((8, d), F32)
    vecs = vecs.at[ROW_G_PRE].set(g_pre).at[ROW_G_PM].set(g_pm)
    vecs = vecs.at[ROW_G_PFF].set(g_pff).at[ROW_G_PF].set(g_pf)
    vecs = vecs.at[ROW_LN, 0:2 * CHUNK_WIDTH].set(jnp.concatenate([ln_g, ln_b]))
    vecs = vecs.at[ROW_GOUT, 0:CHUNK_WIDTH + ATTN_WIDTH].set(jnp.concatenate([g_oc, g_oa]))
    return vecs


def _pad_q_weights(w_in):
    d = w_in.shape[0]
    wq = w_in[:, 2 * CHUNK_WIDTH:2 * CHUNK_WIDTH + ATTN_WIDTH].reshape(d, N_ATTN_HEADS, HEAD_DIM)
    half = (jnp.arange(N_ATTN_HEADS) // Q_PER_KV)[None, :, None, None]
    sel = (half == jnp.arange(N_KV_HEADS)[None, None, :, None])
    padded = jnp.where(sel, wq[:, :, None, :], 0).astype(w_in.dtype)
    return padded.reshape(d, N_ATTN_HEADS * LANES)


def _sample_gate_tables(w_sp, b_sp, L):
    rows = []
    for delta in range(L):
        for t in range(L):
            if t >= delta:
                rows.append(jnp.repeat(w_sp[:, t, t - delta], HEAD_DIM))
            else:
                rows.append(jnp.zeros((CHUNK_WIDTH,), F32))
    tab = jnp.stack(rows)
    bias = jnp.repeat(b_sp[:, :L].T, HEAD_DIM, axis=1)
    pad = (-L) % 8
    return tab, jnp.pad(bias, ((0, pad), (0, 0)))


def _heads_last(t):
    b, _, w = t.shape
    return t.reshape(b, N_KV_HEADS, HEAD_DIM, w).transpose(0, 3, 1, 2)


def kernel(x_prompt, x_sample, cache_win_k, cache_win_v, w_in, g_pre_mix, ln_v_g, ln_v_b,
           w_spatial, b_spatial, attn_sinks, g_out_chunk, g_out_attn, w_o, g_post_mix,
           g_pre_ffn, w_up, w_down, g_post_ffn):
    depth = w_in.shape[0]
    B, S, D = x_prompt.shape
    Bd, L, _ = x_sample.shape
    W = cache_win_k.shape[2]
    yp, ys = x_prompt, x_sample
    wk_p, wv_p, cv_p, wk_s, wv_s, cv_s = [], [], [], [], [], []
    for l in range(depth):
        vecs = _pack_vectors(D, g_pre_mix[l], g_post_mix[l], g_pre_ffn[l], g_post_ffn[l],
                             ln_v_g[l], ln_v_b[l], g_out_chunk[l], g_out_attn[l])
        w_in_b = w_in[l].astype(BF16)
        w_o_b = w_o[l].astype(BF16)
        w_up_b = w_up[l].astype(BF16)
        w_down_b = w_down[l].astype(BF16)
        sinks = attn_sinks[l].reshape(N_ATTN_HEADS).astype(F32)
        bias_sp = jnp.repeat(b_spatial[l].T, HEAD_DIM, axis=1)

        yp, wk, wv, cv = _prompt_call(yp, sinks, w_in_b, w_o_b, w_up_b, w_down_b, vecs,
                                      w_spatial[l], bias_sp)
        wk_p.append(_heads_last(wk))
        wv_p.append(_heads_last(wv))
        cv_p.append(cv)

        gate_tab, gate_bias = _sample_gate_tables(w_spatial[l], b_spatial[l], L)
        kv0 = 2 * CHUNK_WIDTH + ATTN_WIDTH
        wkv_t = w_in_b[:, kv0:kv0 + 2 * KV_WIDTH].T
        slopes = jnp.asarray([_alibi_slope(hd) for hd in range(N_ATTN_HEADS)], F32)
        info = jnp.zeros((N_ATTN_HEADS * 2 * L, LANES), F32)
        info = info.at[:, 0].set(jnp.repeat(sinks, 2 * L)).at[:, 1].set(jnp.repeat(slopes, 2 * L))
        cache_kt = cache_win_k[l].transpose(0, 2, 3, 1).reshape(Bd, KV_WIDTH, W)
        cache_vt = cache_win_v[l].transpose(0, 2, 3, 1).reshape(Bd, KV_WIDTH, W)
        ys, cv, wk, wv = _sample_call(ys, cache_kt, cache_vt, w_in_b, _pad_q_weights(w_in_b),
                                      wkv_t, w_o_b, w_up_b, w_down_b, vecs, gate_tab, gate_bias,
                                      info)
        ys = ys.reshape(Bd, L, D)
        wk_s.append(_heads_last(wk))
        wv_s.append(_heads_last(wv))
        cv_s.append(cv.reshape(Bd, L, CHUNK_WIDTH))
    return (yp, ys, jnp.stack(wk_p), jnp.stack(wv_p), jnp.stack(cv_p),
            jnp.stack(wk_s), jnp.stack(wv_s), jnp.stack(cv_s))
```

```python
import functools

import jax
import jax.numpy as jnp
from jax import lax
from jax.experimental import pallas as pl
from jax.experimental.pallas import tpu as pltpu

F32 = jnp.float32
BF16 = jnp.bfloat16

HEAD_DIM = 64
N_CHUNK_HEADS = 8
N_ATTN_HEADS = 8
N_KV_HEADS = 2
Q_PER_KV = N_ATTN_HEADS // N_KV_HEADS
CHUNK_WIDTH = N_CHUNK_HEADS * HEAD_DIM
ATTN_WIDTH = N_ATTN_HEADS * HEAD_DIM
KV_WIDTH = N_KV_HEADS * HEAD_DIM
CHUNK = 128
WINDOW = 128
EPS = 1e-6
ATTN_SCALE = HEAD_DIM ** -0.5

LANES = 128
HALF = LANES // 2
N_PAIRS = ATTN_WIDTH // LANES
VMEM_LIMIT = 56 * 1024 * 1024

PROMPT_TILE = 512
FF_CHUNK = 512
SAMPLE_BB = 16

ROW_G_PRE, ROW_G_PM, ROW_G_PFF, ROW_G_PF, ROW_LN, ROW_GOUT = range(6)


def _dot(a, b):
    return jnp.dot(a, b, preferred_element_type=F32)


def _rms(x, g):
    return x * lax.rsqrt(jnp.mean(x * x, axis=-1, keepdims=True) + EPS) * g


def _layer_norm(x, g, b):
    xc = x - jnp.mean(x, axis=-1, keepdims=True)
    return xc * lax.rsqrt(jnp.mean(xc * xc, axis=-1, keepdims=True) + EPS) * g + b


def _alibi_slope(head):
    return 2.0 ** (-8.0 * (head + 1) / N_ATTN_HEADS)


def _merge(x, a_out, b_out, w_o_ref, vec_ref):
    d = x.shape[-1]
    cw = CHUNK_WIDTH
    ra = _rms(a_out, vec_ref[ROW_GOUT:ROW_GOUT + 1, 0:cw]).astype(BF16)
    rb = _rms(b_out, vec_ref[ROW_GOUT:ROW_GOUT + 1, cw:cw + ATTN_WIDTH]).astype(BF16)
    o = _dot(ra, w_o_ref[0:cw, :]) + _dot(rb, w_o_ref[cw:cw + ATTN_WIDTH, :])
    return x + _rms(o, vec_ref[ROW_G_PM:ROW_G_PM + 1, :d])


def _ffn(x1, w_up_ref, w_down_ref, vec_ref):
    d = x1.shape[-1]
    h2 = _rms(x1, vec_ref[ROW_G_PFF:ROW_G_PFF + 1, :d]).astype(BF16)
    d_ff = w_up_ref.shape[1]
    f = jnp.zeros(x1.shape, F32)
    for c in range(d_ff // FF_CHUNK):
        up = _dot(h2, w_up_ref[:, c * FF_CHUNK:(c + 1) * FF_CHUNK])
        r = jnp.maximum(up, 0.0)
        f = f + _dot((r * r).astype(BF16), w_down_ref[c * FF_CHUNK:(c + 1) * FF_CHUNK, :])
    return x1 + _rms(f, vec_ref[ROW_G_PF:ROW_G_PF + 1, :d])


def _fill_vector_table(vec_s, g_pre, g_pm, g_pff, g_pf, ln_g, ln_b, g_oc, g_oa):
    d = g_pre.shape[-1]
    cw = CHUNK_WIDTH
    vec_s[ROW_G_PRE:ROW_G_PRE + 1, 0:d] = g_pre[...]
    vec_s[ROW_G_PM:ROW_G_PM + 1, 0:d] = g_pm[...]
    vec_s[ROW_G_PFF:ROW_G_PFF + 1, 0:d] = g_pff[...]
    vec_s[ROW_G_PF:ROW_G_PF + 1, 0:d] = g_pf[...]
    vec_s[ROW_LN:ROW_LN + 1, 0:cw] = ln_g[...]
    vec_s[ROW_LN:ROW_LN + 1, cw:2 * cw] = ln_b[...]
    vec_s[ROW_GOUT:ROW_GOUT + 1, 0:cw] = g_oc[...]
    vec_s[ROW_GOUT:ROW_GOUT + 1, cw:cw + ATTN_WIDTH] = g_oa[...]


def _prompt_kernel(n_tiles, nj, layer, sink_ref, x_ref, w_in_ref, w_o_ref, w_up_ref, w_down_ref,
                   g_pre, g_pm, g_pff, g_pf, ln_g, ln_b, g_oc, g_oa, wsp_ref, bsp_ref,
                   y_ref, wk_ref, wv_ref, cv_ref,
                   vec_ref, bias_s, wcat_s, tab_s, kt_s, va_s, vb_s, a_s, bo_s, m_s, x1_s, x1n_s,
                   h_s, q_s, h2_s, act_s, f_s):
    step = pl.program_id(0)
    j = jnp.minimum(step, n_tiles - 1) % nj
    T, d = x_ref.shape[1], x_ref.shape[2]
    nb = T // WINDOW
    cw = CHUNK_WIDTH
    n_chunks = w_up_ref.shape[1] // FF_CHUNK

    lane = lax.broadcasted_iota(jnp.int32, (WINDOW, LANES), 1)
    lo = lane < HALF

    @pl.when(step == 0)
    def _init_tables():
        x1_s[...] = jnp.zeros(x1_s.shape, F32)
        h2_s[...] = jnp.zeros(h2_s.shape, BF16)
        f_s[...] = jnp.zeros(f_s.shape, F32)
        _fill_vector_table(vec_ref, g_pre, g_pm, g_pff, g_pf, ln_g, ln_b, g_oc, g_oa)
        t_idx = lax.broadcasted_iota(jnp.int32, (CHUNK, CHUNK), 0)
        s_idx = lax.broadcasted_iota(jnp.int32, (CHUNK, CHUNK), 1)
        for g in range(N_CHUNK_HEADS):
            wm = jnp.where(s_idx <= t_idx, wsp_ref[0, g], 0.0).astype(BF16)
            wcat_s[g // 2, :, (g % 2) * CHUNK:(g % 2 + 1) * CHUNK] = wm
        b_rows = jnp.concatenate(
            [bsp_ref[0], jnp.zeros((CHUNK - N_CHUNK_HEADS, CHUNK), F32)], axis=0)
        b_cols = b_rows.T
        for p in range(cw // LANES):
            bias_s[:, p * LANES:(p + 1) * LANES] = jnp.where(
                lo, jnp.broadcast_to(b_cols[:, 2 * p:2 * p + 1], (CHUNK, LANES)),
                jnp.broadcast_to(b_cols[:, 2 * p + 1:2 * p + 2], (CHUNK, LANES)))
        a_idx = lax.broadcasted_iota(jnp.int32, (WINDOW, 2 * WINDOW), 0)
        c_idx = lax.broadcasted_iota(jnp.int32, (WINDOW, 2 * WINDOW), 1)
        dist = WINDOW + a_idx - c_idx
        valid = (dist >= 0) & (dist <= WINDOW)
        for hd in range(N_ATTN_HEADS):
            bias = jnp.where(valid, -(_alibi_slope(hd) * dist.astype(F32)), -jnp.inf)
            tab_s[hd // 2, (hd % 2) * WINDOW:(hd % 2 + 1) * WINDOW, :] = bias

    @pl.when(j == 0)
    def _clear_carry():
        kt_s[:, 0:WINDOW] = jnp.zeros((2 * LANES, WINDOW), BF16)
        va_s[:, 0:WINDOW, :] = jnp.zeros((N_KV_HEADS, WINDOW, LANES), BF16)
        vb_s[:, 0:WINDOW, :] = jnp.zeros((N_KV_HEADS, WINDOW, LANES), BF16)

    def ffn_up(c):
        r = jnp.maximum(_dot(h2_s[...], w_up_ref[:, c * FF_CHUNK:(c + 1) * FF_CHUNK]), 0.0)
        act_s[...] = (r * r).astype(BF16)

    def ffn_down(c):
        part = _dot(act_s[...], w_down_ref[c * FF_CHUNK:(c + 1) * FF_CHUNK, :])
        if c == 0:
            f_s[...] = part
        else:
            f_s[...] += part

    late = iter([functools.partial(fn, c) for c in range(n_chunks // 2, n_chunks)
                 for fn in (ffn_up, ffn_down)])

    def late_stage():
        next(late)()

    late_stage()
    h_s[...] = _rms(x_ref[0], vec_ref[ROW_G_PRE:ROW_G_PRE + 1, :d]).astype(BF16)
    u_pre = _dot(h_s[...], w_in_ref[:, 0:cw])
    v_pre = _dot(h_s[...], w_in_ref[:, cw:2 * cw])
    late_stage()
    u = jax.nn.gelu(u_pre)
    late_stage()
    v = _layer_norm(jax.nn.gelu(v_pre), vec_ref[ROW_LN:ROW_LN + 1, 0:cw],
                    vec_ref[ROW_LN:ROW_LN + 1, cw:2 * cw])
    kv0 = 2 * cw + ATTN_WIDTH
    kv = _dot(h_s[...], w_in_ref[:, kv0:kv0 + 2 * KV_WIDTH])
    k = kv[:, 0:KV_WIDTH]
    val = kv[:, KV_WIDTH:2 * KV_WIDTH]
    q0 = 2 * cw
    q_s[...] = _dot(h_s[...], w_in_ref[:, q0:q0 + ATTN_WIDTH]) * ATTN_SCALE
    late_stage()

    kt_f = k.T
    wk_ref[0] = kt_f[:, T - WINDOW:]
    wv_ref[0] = val[T - WINDOW:, :].T
    cv_ref[0] = v[T - CHUNK:, :]

    for c in range(nb):
        rows = slice(c * CHUNK, (c + 1) * CHUNK)
        for p in range(cw // LANES):
            cols = slice(p * LANES, (p + 1) * LANES)
            vcp = v[rows, cols]
            rhs = jnp.concatenate([jnp.where(lo, vcp, 0.0), jnp.where(lo, 0.0, vcp)],
                                  axis=0).astype(BF16)
            s_cp = _dot(wcat_s[p], rhs) + bias_s[:, cols]
            a_s[rows, cols] = u[rows, cols] * s_cp
    m_s[:, 0:cw] = _rms(a_s[...], vec_ref[ROW_GOUT:ROW_GOUT + 1, 0:cw]).astype(BF16)

    kt = kt_f.astype(BF16)
    for hk in range(N_KV_HEADS):
        kth = kt[hk * HEAD_DIM:(hk + 1) * HEAD_DIM, :]
        kt_s[hk * LANES:hk * LANES + HEAD_DIM, WINDOW:] = kth
        kt_s[hk * LANES + HEAD_DIM:(hk + 1) * LANES, WINDOW:] = kth
    lane_t = lax.broadcasted_iota(jnp.int32, (T, LANES), 1)
    lo_t = lane_t < HALF
    val_sw = pltpu.roll(val, HALF, axis=1)
    va_s[0, WINDOW:, :] = jnp.where(lo_t, val, 0.0).astype(BF16)
    vb_s[0, WINDOW:, :] = jnp.where(lo_t, 0.0, val_sw).astype(BF16)
    va_s[1, WINDOW:, :] = jnp.where(lo_t, val_sw, 0.0).astype(BF16)
    vb_s[1, WINDOW:, :] = jnp.where(lo_t, 0.0, val).astype(BF16)

    first_block_bias = jnp.where(j == 0, -jnp.inf, 0.0).astype(F32)
    row2 = lax.broadcasted_iota(jnp.int32, (2 * WINDOW, 1), 0)
    late_after_scores = (5,)
    for n, (i, m) in enumerate((i, m) for i in range(nb) for m in range(N_PAIRS)):
        hk = (2 * m) // Q_PER_KV
        rows = slice(i * WINDOW, (i + 1) * WINDOW)
        keys = slice(i * WINDOW, (i + 2) * WINDOW)
        cols = slice(m * LANES, (m + 1) * LANES)
        qb = q_s[rows, cols]
        qq = jnp.concatenate([jnp.where(lo, qb, 0.0), jnp.where(lo, 0.0, qb)],
                             axis=0).astype(BF16)
        s = _dot(qq, kt_s[hk * LANES:(hk + 1) * LANES, keys]) + tab_s[m]
        if i == 0:
            s = jnp.concatenate([s[:, 0:WINDOW] + first_block_bias, s[:, WINDOW:]], axis=1)
        if n in late_after_scores:
            late_stage()
        sink = jnp.where(row2 < WINDOW, sink_ref[layer, hk, (2 * m) % Q_PER_KV],
                         sink_ref[layer, hk, (2 * m + 1) % Q_PER_KV])
        mx = jnp.maximum(jnp.max(s, axis=-1, keepdims=True), sink)
        p_un = jnp.exp(s - mx)
        z = jnp.sum(p_un, axis=-1, keepdims=True) + jnp.exp(sink - mx)
        pb = p_un.astype(BF16)
        o = (_dot(pb[0:WINDOW], va_s[hk, keys, :]) +
             _dot(pb[WINDOW:], vb_s[hk, keys, :]))
        rz = 1.0 / z
        bo_s[rows, cols] = o * jnp.where(lo, rz[0:WINDOW], rz[WINDOW:])

    kt_s[:, 0:WINDOW] = kt_s[:, T:T + WINDOW]
    va_s[:, 0:WINDOW, :] = va_s[:, T:T + WINDOW, :]
    vb_s[:, 0:WINDOW, :] = vb_s[:, T:T + WINDOW, :]

    late_stage()
    m_s[:, cw:cw + ATTN_WIDTH] = _rms(
        bo_s[...], vec_ref[ROW_GOUT:ROW_GOUT + 1, cw:cw + ATTN_WIDTH]).astype(BF16)
    o_proj = _dot(m_s[...], w_o_ref[...])
    late_stage()
    x1 = x_ref[0] + _rms(o_proj, vec_ref[ROW_G_PM:ROW_G_PM + 1, :d])
    x1n_s[...] = x1
    h2_s[...] = _rms(x1, vec_ref[ROW_G_PFF:ROW_G_PFF + 1, :d]).astype(BF16)
    late_stage()

    ffn_up(0)
    y_ref[0] = x1_s[...] + _rms(f_s[...], vec_ref[ROW_G_PF:ROW_G_PF + 1, :d])
    x1_s[...] = x1n_s[...]
    ffn_down(0)
    for c in range(1, n_chunks // 2):
        ffn_up(c)
        ffn_down(c)


def _resident(shape):
    return pl.BlockSpec(shape, lambda *_: (0,) * len(shape), pipeline_mode=pl.Buffered(1))


def _layer_row(arr, layer):
    block = (1,) + arr.shape[1:]
    return pl.BlockSpec(block, lambda *_: (layer,) + (0,) * (arr.ndim - 1),
                        pipeline_mode=pl.Buffered(1))


def _prompt_call(x, layer, sinks, w_in, w_o, w_up, w_down, norms, w_sp, b_sp):
    B, S, D = x.shape
    T = PROMPT_TILE
    assert S % T == 0 and T % WINDOW == 0 and WINDOW == CHUNK
    assert (w_up.shape[1] // FF_CHUNK) % 2 == 0
    nj = S // T
    n_tiles = B * nj
    cur = lambda s: jnp.minimum(s, n_tiles - 1)
    prev = lambda s: jnp.maximum(s - 1, 0)
    tile_in = pl.BlockSpec((1, T, D), lambda s: (cur(s) // nj, cur(s) % nj, 0))
    tile_out = pl.BlockSpec((1, T, D), lambda s: (prev(s) // nj, prev(s) % nj, 0))
    last = lambda r, c: pl.BlockSpec((1, r, c), lambda s: (cur(s) // nj, 0, 0))
    return pl.pallas_call(
        functools.partial(_prompt_kernel, n_tiles, nj, layer),
        grid=(n_tiles + 1,),
        in_specs=[pl.BlockSpec(memory_space=pltpu.SMEM), tile_in,
                  _resident(w_in.shape), _resident(w_o.shape), _resident(w_up.shape),
                  _resident(w_down.shape)] + [_layer_row(a, layer) for a in norms] +
                 [_layer_row(w_sp, layer), _layer_row(b_sp, layer)],
        out_specs=[tile_out, last(KV_WIDTH, WINDOW), last(KV_WIDTH, WINDOW),
                   last(CHUNK, CHUNK_WIDTH)],
        out_shape=[jax.ShapeDtypeStruct((B, S, D), F32),
                   jax.ShapeDtypeStruct((B, KV_WIDTH, WINDOW), F32),
                   jax.ShapeDtypeStruct((B, KV_WIDTH, WINDOW), F32),
                   jax.ShapeDtypeStruct((B, CHUNK, CHUNK_WIDTH), F32)],
        scratch_shapes=[pltpu.VMEM((8, D), F32),
                        pltpu.VMEM((CHUNK, CHUNK_WIDTH), F32),
                        pltpu.VMEM((N_CHUNK_HEADS // 2, CHUNK, 2 * CHUNK), BF16),
                        pltpu.VMEM((N_PAIRS, 2 * WINDOW, 2 * WINDOW), F32),
                        pltpu.VMEM((2 * LANES, T + WINDOW), BF16),
                        pltpu.VMEM((N_KV_HEADS, T + WINDOW, LANES), BF16),
                        pltpu.VMEM((N_KV_HEADS, T + WINDOW, LANES), BF16),
                        pltpu.VMEM((T, CHUNK_WIDTH), F32),
                        pltpu.VMEM((T, ATTN_WIDTH), F32),
                        pltpu.VMEM((T, CHUNK_WIDTH + ATTN_WIDTH), BF16),
                        pltpu.VMEM((T, D), F32),
                        pltpu.VMEM((T, D), F32),
                        pltpu.VMEM((T, D), BF16),
                        pltpu.VMEM((T, ATTN_WIDTH), F32),
                        pltpu.VMEM((T, D), BF16),
                        pltpu.VMEM((T, FF_CHUNK), BF16),
                        pltpu.VMEM((T, D), F32)],
        compiler_params=pltpu.CompilerParams(
            dimension_semantics=("arbitrary",), vmem_limit_bytes=VMEM_LIMIT),
        name="prompt_layer",
    )(sinks, x, w_in, w_o, w_up, w_down, *norms, w_sp, b_sp)


def _sample_kernel(L, layer, sink_ref, w4_ref, b4_ref, x_ref, ckt_ref, cvt_ref, w_in_ref,
                   wkvt_ref, w_o_ref, w_up_ref, w_down_ref,
                   g_pre, g_pm, g_pff, g_pf, ln_g, ln_b, g_oc, g_oa,
                   y_ref, vn_ref, wkt_ref, wvt_ref,
                   vec_ref, q_s, kvt_s, a_s, b_s):
    g = pl.program_id(0)
    n_steps = pl.num_programs(0)
    bb, hd, W = ckt_ref.shape
    n, d = x_ref.shape
    n_heads = N_ATTN_HEADS
    pair_rows = 2 * L
    assert pair_rows == 8 and hd == LANES and W == LANES
    batch_per_block = LANES // L

    @pl.when(g == 0)
    def _project_and_gate():
        cw = CHUNK_WIDTH
        _fill_vector_table(vec_ref, g_pre, g_pm, g_pff, g_pf, ln_g, ln_b, g_oc, g_oa)
        h = _rms(x_ref[...], vec_ref[ROW_G_PRE:ROW_G_PRE + 1, :d]).astype(BF16)
        u = jax.nn.gelu(_dot(h, w_in_ref[:, 0:cw]))
        v = _layer_norm(jax.nn.gelu(_dot(h, w_in_ref[:, cw:2 * cw])),
                        vec_ref[ROW_LN:ROW_LN + 1, 0:cw], vec_ref[ROW_LN:ROW_LN + 1, cw:2 * cw])
        vn_ref[...] = v
        q_s[...] = _dot(h, w_in_ref[:, 2 * cw:2 * cw + ATTN_WIDTH]) * ATTN_SCALE
        kvt_s[...] = lax.dot_general(wkvt_ref[...], h, (((1,), (1,)), ((), ())),
                                     preferred_element_type=F32)
        t_of_row = lax.broadcasted_iota(jnp.int32, (n, 1), 0) % L
        group_of_lane = lax.broadcasted_iota(jnp.int32, (1, cw), 1) // HEAD_DIM

        def per_group(value_of_group):
            row = jnp.zeros((1, cw), F32)
            for grp in range(N_CHUNK_HEADS):
                row = jnp.where(group_of_lane == grp, value_of_group(grp), row)
            return row

        s = jnp.zeros(v.shape, F32)
        for t in range(L):
            s = jnp.where(t_of_row == t, per_group(lambda grp: b4_ref[layer, grp, t]), s)
        for delta in range(L):
            coef = jnp.zeros(v.shape, F32)
            for t in range(delta, L):
                w_row = per_group(lambda grp: w4_ref[grp, t, t - delta])
                coef = jnp.where(t_of_row == t, w_row, coef)
            vs = v if delta == 0 else pltpu.roll(v, delta, axis=0)
            s = s + coef * vs
        a_s[...] = u * s

    rows_all = n_heads * pair_rows
    r_idx = lax.broadcasted_iota(jnp.int32, (rows_all, LANES), 0)
    l_idx = lax.broadcasted_iota(jnp.int32, (rows_all, LANES), 1)
    t_idx = r_idx % L
    odd = (r_idx % pair_rows) >= L
    head_of_row = lax.broadcasted_iota(jnp.int32, (rows_all, 1), 0) // pair_rows
    sink = jnp.zeros((rows_all, 1), F32)
    slope = jnp.zeros((rows_all, 1), F32)
    for hh in range(n_heads):
        sink = jnp.where(head_of_row == hh, sink_ref[layer, hh // Q_PER_KV, hh % Q_PER_KV], sink)
        slope = jnp.where(head_of_row == hh, _alibi_slope(hh), slope)
    dist_old = W + t_idx - l_idx
    bias_old = jnp.where(dist_old <= WINDOW, -(slope * dist_old.astype(F32)), -jnp.inf)
    lane8 = lax.broadcasted_iota(jnp.int32, (pair_rows, LANES), 1)
    lo8 = lane8 < HALF
    lane_w = lax.broadcasted_iota(jnp.int32, (hd, W), 1)

    first_batch = g * bb
    new_block = pl.multiple_of((first_batch // batch_per_block) * LANES, LANES)
    kt_new = kvt_s[0:KV_WIDTH, pl.ds(new_block, LANES)]
    vt_new = kvt_s[KV_WIDTH:2 * KV_WIDTH, pl.ds(new_block, LANES)]
    kt_new_b = kt_new.astype(BF16)
    vt_new_b = vt_new.astype(BF16)
    for p in range(bb // 2):
        off = ((first_batch + 2 * p) % batch_per_block) * L
        row0 = pl.multiple_of((first_batch + 2 * p) * L, pair_rows)
        tiles = []
        for m in range(N_PAIRS):
            qb = q_s[pl.ds(row0, pair_rows), m * LANES:(m + 1) * LANES]
            qb_sw = pltpu.roll(qb, HALF, axis=1)
            if (2 * m) // Q_PER_KV == 0:
                tiles += [jnp.where(lo8, qb, 0.0), jnp.where(lo8, qb_sw, 0.0)]
            else:
                tiles += [jnp.where(lo8, 0.0, qb_sw), jnp.where(lo8, 0.0, qb)]
        lhs = jnp.concatenate(tiles, axis=0).astype(BF16)
        s_pair = []
        for e in range(2):
            rhs = jnp.concatenate([ckt_ref[2 * p + e].astype(BF16), kt_new_b], axis=1)
            s_pair.append(_dot(lhs, rhs))
        sc = jnp.where(jnp.concatenate([odd, odd], axis=1), s_pair[1], s_pair[0])
        rel = l_idx - off - jnp.where(odd, L, 0)
        bias_new = jnp.where((rel >= 0) & (rel <= t_idx),
                             -(slope * (t_idx - rel).astype(F32)), -jnp.inf)
        sc = sc + jnp.concatenate([bias_old, bias_new], axis=1)
        mx = jnp.maximum(jnp.max(sc, axis=-1, keepdims=True), sink)
        p_un = jnp.exp(sc - mx)
        z = jnp.sum(p_un, axis=-1, keepdims=True) + jnp.exp(sink - mx)
        pb = p_un.astype(BF16)
        o_pair = []
        for e in range(2):
            vt = jnp.concatenate([cvt_ref[2 * p + e].astype(BF16), vt_new_b], axis=1)
            o_pair.append(lax.dot_general(pb, vt, (((1,), (1,)), ((), ())),
                                          preferred_element_type=F32))
        o = jnp.where(odd, o_pair[1], o_pair[0]) * (1.0 / z)
        for m in range(N_PAIRS):
            kv_head = (2 * m) // Q_PER_KV
            o_even = o[(2 * m) * pair_rows:(2 * m + 1) * pair_rows]
            o_odd = o[(2 * m + 1) * pair_rows:(2 * m + 2) * pair_rows]
            if kv_head == 0:
                o_odd = pltpu.roll(o_odd, HALF, axis=1)
            else:
                o_even = pltpu.roll(o_even, HALF, axis=1)
            b_s[pl.ds(row0, pair_rows), m * LANES:(m + 1) * LANES] = jnp.where(lo8, o_even, o_odd)
        for e in range(2):
            shift_new = (W - L) - (off + e * L)
            keep = lane_w < W - L
            wkt_ref[2 * p + e] = jnp.where(keep, pltpu.roll(ckt_ref[2 * p + e], W - L, axis=1),
                                           pltpu.roll(kt_new, shift_new, axis=1))
            wvt_ref[2 * p + e] = jnp.where(keep, pltpu.roll(cvt_ref[2 * p + e], W - L, axis=1),
                                           pltpu.roll(vt_new, shift_new, axis=1))

    @pl.when(g == n_steps - 1)
    def _merge_and_ffn():
        x1 = _merge(x_ref[...], a_s[...], b_s[...], w_o_ref, vec_ref)
        y_ref[...] = _ffn(x1, w_up_ref, w_down_ref, vec_ref)


def _sample_call(x, layer, sinks, w4, b_sp, cache_kt, cache_vt, w_in, wkv_t, w_o, w_up, w_down,
                 norms):
    Bd, L, D = x.shape
    hd, W = cache_kt.shape[1:]
    n = Bd * L
    bb = SAMPLE_BB
    assert Bd % bb == 0 and bb % 2 == 0 and (LANES // L) % bb == 0
    xs = x.reshape(n, D)
    smem = pl.BlockSpec(memory_space=pltpu.SMEM)
    blk = pl.BlockSpec((bb, hd, W), lambda g: (g, 0, 0))
    const = lambda shape: pl.BlockSpec(shape, lambda g: (0,) * len(shape))
    weights = (w_in, wkv_t, w_o, w_up, w_down)
    return pl.pallas_call(
        functools.partial(_sample_kernel, L, layer),
        grid=(Bd // bb,),
        in_specs=[smem, smem, smem, _resident(xs.shape), blk, blk] +
                 [_resident(w.shape) for w in weights] + [_layer_row(a, layer) for a in norms],
        out_specs=[const((n, D)), const((n, CHUNK_WIDTH)), blk, blk],
        out_shape=[jax.ShapeDtypeStruct((n, D), F32),
                   jax.ShapeDtypeStruct((n, CHUNK_WIDTH), F32),
                   jax.ShapeDtypeStruct(cache_kt.shape, F32),
                   jax.ShapeDtypeStruct(cache_vt.shape, F32)],
        scratch_shapes=[pltpu.VMEM((8, D), F32),
                        pltpu.VMEM((n, ATTN_WIDTH), F32),
                        pltpu.VMEM((2 * KV_WIDTH, n), F32),
                        pltpu.VMEM((n, CHUNK_WIDTH), F32),
                        pltpu.VMEM((n, ATTN_WIDTH), F32)],
        compiler_params=pltpu.CompilerParams(
            dimension_semantics=("arbitrary",), vmem_limit_bytes=VMEM_LIMIT),
        name="sample_layer",
    )(sinks, w4, b_sp, xs, cache_kt, cache_vt, *weights, *norms)


def _heads_last(t):
    b, _, w = t.shape
    return t.reshape(b, N_KV_HEADS, HEAD_DIM, w).transpose(0, 3, 1, 2)


def kernel(x_prompt, x_sample, cache_win_k, cache_win_v, w_in, g_pre_mix, ln_v_g, ln_v_b,
           w_spatial, b_spatial, attn_sinks, g_out_chunk, g_out_attn, w_o, g_post_mix,
           g_pre_ffn, w_up, w_down, g_post_ffn):
    depth = w_in.shape[0]
    Bd, L, D = x_sample.shape
    W = cache_win_k.shape[2]
    norms = (g_pre_mix, g_post_mix, g_pre_ffn, g_post_ffn, ln_v_g, ln_v_b, g_out_chunk,
             g_out_attn)
    yp, ys = x_prompt, x_sample
    wk_p, wv_p, cv_p, wk_s, wv_s, cv_s = [], [], [], [], [], []
    for l in range(depth):
        w_in_b = w_in[l].astype(BF16)
        w_o_b = w_o[l].astype(BF16)
        w_up_b = w_up[l].astype(BF16)
        w_down_b = w_down[l].astype(BF16)

        yp, wk, wv, cv = _prompt_call(yp, l, attn_sinks, w_in_b, w_o_b, w_up_b, w_down_b, norms,
                                      w_spatial, b_spatial)
        wk_p.append(_heads_last(wk))
        wv_p.append(_heads_last(wv))
        cv_p.append(cv)

        kv0 = 2 * CHUNK_WIDTH + ATTN_WIDTH
        wkv_t = w_in[l, :, kv0:kv0 + 2 * KV_WIDTH].T.astype(BF16)
        w4 = w_spatial[l, :, :L, :L]
        cache_kt = cache_win_k[l].transpose(0, 2, 3, 1).reshape(Bd, KV_WIDTH, W)
        cache_vt = cache_win_v[l].transpose(0, 2, 3, 1).reshape(Bd, KV_WIDTH, W)
        ys, cv, wk, wv = _sample_call(ys, l, attn_sinks, w4, b_spatial, cache_kt, cache_vt,
                                      w_in_b, wkv_t, w_o_b, w_up_b, w_down_b, norms)
        ys = ys.reshape(Bd, L, D)
        wk_s.append(_heads_last(wk))
        wv_s.append(_heads_last(wv))
        cv_s.append(cv.reshape(Bd, L, CHUNK_WIDTH))
    return (yp, ys, jnp.stack(wk_p), jnp.stack(wv_p), jnp.stack(cv_p),
            jnp.stack(wk_s), jnp.stack(wv_s), jnp.stack(cv_s))
```

```python
import functools

import jax
import jax.numpy as jnp
from jax import lax
from jax.experimental import pallas as pl
from jax.experimental.pallas import tpu as pltpu

F32 = jnp.float32
BF16 = jnp.bfloat16

HEAD_DIM = 64
N_CHUNK_HEADS = 8
N_ATTN_HEADS = 8
N_KV_HEADS = 2
Q_PER_KV = N_ATTN_HEADS // N_KV_HEADS
CHUNK_WIDTH = N_CHUNK_HEADS * HEAD_DIM
ATTN_WIDTH = N_ATTN_HEADS * HEAD_DIM
KV_WIDTH = N_KV_HEADS * HEAD_DIM
CHUNK = 128
WINDOW = 128
EPS = 1e-6
ATTN_SCALE = HEAD_DIM ** -0.5

LANES = 128
HALF = LANES // 2
N_PAIRS = ATTN_WIDTH // LANES
VMEM_LIMIT = 56 * 1024 * 1024

PROMPT_TILE = 512
FF_CHUNK = 512
SAMPLE_BB = 16

ROW_G_PRE, ROW_G_PM, ROW_G_PFF, ROW_G_PF, ROW_LN, ROW_GOUT = range(6)


def _dot(a, b):
    return jnp.dot(a, b, preferred_element_type=F32)


def _rms(x, g):
    return x * lax.rsqrt(jnp.mean(x * x, axis=-1, keepdims=True) + EPS) * g


def _layer_norm(x, g, b):
    xc = x - jnp.mean(x, axis=-1, keepdims=True)
    return xc * lax.rsqrt(jnp.mean(xc * xc, axis=-1, keepdims=True) + EPS) * g + b


def _alibi_slope(head):
    return 2.0 ** (-8.0 * (head + 1) / N_ATTN_HEADS)


def _merge(x, a_out, b_out, w_o_ref, vec_ref):
    d = x.shape[-1]
    cw = CHUNK_WIDTH
    ra = _rms(a_out, vec_ref[ROW_GOUT:ROW_GOUT + 1, 0:cw]).astype(BF16)
    rb = _rms(b_out, vec_ref[ROW_GOUT:ROW_GOUT + 1, cw:cw + ATTN_WIDTH]).astype(BF16)
    o = _dot(ra, w_o_ref[0:cw, :]) + _dot(rb, w_o_ref[cw:cw + ATTN_WIDTH, :])
    return x + _rms(o, vec_ref[ROW_G_PM:ROW_G_PM + 1, :d])


def _ffn(x1, w_up_ref, w_down_ref, vec_ref):
    d = x1.shape[-1]
    h2 = _rms(x1, vec_ref[ROW_G_PFF:ROW_G_PFF + 1, :d]).astype(BF16)
    d_ff = w_up_ref.shape[1]
    f = jnp.zeros(x1.shape, F32)
    for c in range(d_ff // FF_CHUNK):
        up = _dot(h2, w_up_ref[:, c * FF_CHUNK:(c + 1) * FF_CHUNK])
        r = jnp.maximum(up, 0.0)
        f = f + _dot((r * r).astype(BF16), w_down_ref[c * FF_CHUNK:(c + 1) * FF_CHUNK, :])
    return x1 + _rms(f, vec_ref[ROW_G_PF:ROW_G_PF + 1, :d])


def _fill_vector_table(vec_s, g_pre, g_pm, g_pff, g_pf, ln_g, ln_b, g_oc, g_oa):
    d = g_pre.shape[-1]
    cw = CHUNK_WIDTH
    vec_s[ROW_G_PRE:ROW_G_PRE + 1, 0:d] = g_pre[...]
    vec_s[ROW_G_PM:ROW_G_PM + 1, 0:d] = g_pm[...]
    vec_s[ROW_G_PFF:ROW_G_PFF + 1, 0:d] = g_pff[...]
    vec_s[ROW_G_PF:ROW_G_PF + 1, 0:d] = g_pf[...]
    vec_s[ROW_LN:ROW_LN + 1, 0:cw] = ln_g[...]
    vec_s[ROW_LN:ROW_LN + 1, cw:2 * cw] = ln_b[...]
    vec_s[ROW_GOUT:ROW_GOUT + 1, 0:cw] = g_oc[...]
    vec_s[ROW_GOUT:ROW_GOUT + 1, cw:cw + ATTN_WIDTH] = g_oa[...]


def _prompt_kernel(n_tiles, nj, layer, sink_ref, x_ref, w_in_ref, w_o_ref, w_up_ref, w_down_ref,
                   g_pre, g_pm, g_pff, g_pf, ln_g, ln_b, g_oc, g_oa, wsp_ref, bsp_ref,
                   y_ref, wk_ref, wv_ref, cv_ref,
                   vec_ref, bias_s, wcat_s, tab_s, kt_s, va_s, vb_s, a_s, bo_s, m_s, x1_s, x1n_s,
                   h_s, q_s, h2_s, act_s, f_s):
    step = pl.program_id(0)
    j = jnp.minimum(step, n_tiles - 1) % nj
    T, d = x_ref.shape[1], x_ref.shape[2]
    nb = T // WINDOW
    cw = CHUNK_WIDTH
    n_chunks = w_up_ref.shape[1] // FF_CHUNK

    lane = lax.broadcasted_iota(jnp.int32, (WINDOW, LANES), 1)
    lo = lane < HALF

    @pl.when(step == 0)
    def _init_tables():
        x1_s[...] = jnp.zeros(x1_s.shape, F32)
        h2_s[...] = jnp.zeros(h2_s.shape, BF16)
        f_s[...] = jnp.zeros(f_s.shape, F32)
        _fill_vector_table(vec_ref, g_pre, g_pm, g_pff, g_pf, ln_g, ln_b, g_oc, g_oa)
        t_idx = lax.broadcasted_iota(jnp.int32, (CHUNK, CHUNK), 0)
        s_idx = lax.broadcasted_iota(jnp.int32, (CHUNK, CHUNK), 1)
        for g in range(N_CHUNK_HEADS):
            wm = jnp.where(s_idx <= t_idx, wsp_ref[0, g], 0.0).astype(BF16)
            wcat_s[g // 2, :, (g % 2) * CHUNK:(g % 2 + 1) * CHUNK] = wm
        b_rows = jnp.concatenate(
            [bsp_ref[0], jnp.zeros((CHUNK - N_CHUNK_HEADS, CHUNK), F32)], axis=0)
        b_cols = b_rows.T
        for p in range(cw // LANES):
            bias_s[:, p * LANES:(p + 1) * LANES] = jnp.where(
                lo, jnp.broadcast_to(b_cols[:, 2 * p:2 * p + 1], (CHUNK, LANES)),
                jnp.broadcast_to(b_cols[:, 2 * p + 1:2 * p + 2], (CHUNK, LANES)))
        a_idx = lax.broadcasted_iota(jnp.int32, (WINDOW, 2 * WINDOW), 0)
        c_idx = lax.broadcasted_iota(jnp.int32, (WINDOW, 2 * WINDOW), 1)
        dist = WINDOW + a_idx - c_idx
        valid = (dist >= 0) & (dist <= WINDOW)
        for hd in range(N_ATTN_HEADS):
            bias = jnp.where(valid, -(_alibi_slope(hd) * dist.astype(F32)), -jnp.inf)
            tab_s[hd // 2, (hd % 2) * WINDOW:(hd % 2 + 1) * WINDOW, :] = bias

    @pl.when(j == 0)
    def _clear_carry():
        kt_s[:, 0:WINDOW] = jnp.zeros((2 * LANES, WINDOW), BF16)
        va_s[:, 0:WINDOW, :] = jnp.zeros((N_KV_HEADS, WINDOW, LANES), BF16)
        vb_s[:, 0:WINDOW, :] = jnp.zeros((N_KV_HEADS, WINDOW, LANES), BF16)

    def ffn_up(c):
        r = jnp.maximum(_dot(h2_s[...], w_up_ref[:, c * FF_CHUNK:(c + 1) * FF_CHUNK]), 0.0)
        act_s[...] = (r * r).astype(BF16)

    def ffn_down(c):
        part = _dot(act_s[...], w_down_ref[c * FF_CHUNK:(c + 1) * FF_CHUNK, :])
        if c == 0:
            f_s[...] = part
        else:
            f_s[...] += part

    late = iter([functools.partial(fn, c) for c in range(n_chunks // 2, n_chunks)
                 for fn in (ffn_up, ffn_down)])

    def late_stage():
        next(late)()

    late_stage()
    h_s[...] = _rms(x_ref[0], vec_ref[ROW_G_PRE:ROW_G_PRE + 1, :d]).astype(BF16)
    u_pre = _dot(h_s[...], w_in_ref[:, 0:cw])
    v_pre = _dot(h_s[...], w_in_ref[:, cw:2 * cw])
    late_stage()
    u = jax.nn.gelu(u_pre)
    late_stage()
    v = _layer_norm(jax.nn.gelu(v_pre), vec_ref[ROW_LN:ROW_LN + 1, 0:cw],
                    vec_ref[ROW_LN:ROW_LN + 1, cw:2 * cw])
    kv0 = 2 * cw + ATTN_WIDTH
    kv = _dot(h_s[...], w_in_ref[:, kv0:kv0 + 2 * KV_WIDTH])
    k = kv[:, 0:KV_WIDTH]
    val = kv[:, KV_WIDTH:2 * KV_WIDTH]
    q0 = 2 * cw
    q_s[...] = _dot(h_s[...], w_in_ref[:, q0:q0 + ATTN_WIDTH]) * ATTN_SCALE
    late_stage()

    kt_f = k.T
    wk_ref[0] = kt_f[:, T - WINDOW:]
    wv_ref[0] = val[T - WINDOW:, :].T
    cv_ref[0] = v[T - CHUNK:, :]

    for c in range(nb):
        rows = slice(c * CHUNK, (c + 1) * CHUNK)
        for p in range(cw // LANES):
            cols = slice(p * LANES, (p + 1) * LANES)
            vcp = v[rows, cols]
            rhs = jnp.concatenate([jnp.where(lo, vcp, 0.0), jnp.where(lo, 0.0, vcp)],
                                  axis=0).astype(BF16)
            s_cp = _dot(wcat_s[p], rhs) + bias_s[:, cols]
            a_s[rows, cols] = u[rows, cols] * s_cp
    m_s[:, 0:cw] = _rms(a_s[...], vec_ref[ROW_GOUT:ROW_GOUT + 1, 0:cw]).astype(BF16)

    kt = kt_f.astype(BF16)
    for hk in range(N_KV_HEADS):
        kth = kt[hk * HEAD_DIM:(hk + 1) * HEAD_DIM, :]
        kt_s[hk * LANES:hk * LANES + HEAD_DIM, WINDOW:] = kth
        kt_s[hk * LANES + HEAD_DIM:(hk + 1) * LANES, WINDOW:] = kth
    lane_t = lax.broadcasted_iota(jnp.int32, (T, LANES), 1)
    lo_t = lane_t < HALF
    val_sw = pltpu.roll(val, HALF, axis=1)
    va_s[0, WINDOW:, :] = jnp.where(lo_t, val, 0.0).astype(BF16)
    vb_s[0, WINDOW:, :] = jnp.where(lo_t, 0.0, val_sw).astype(BF16)
    va_s[1, WINDOW:, :] = jnp.where(lo_t, val_sw, 0.0).astype(BF16)
    vb_s[1, WINDOW:, :] = jnp.where(lo_t, 0.0, val).astype(BF16)

    first_block_bias = jnp.where(j == 0, -jnp.inf, 0.0).astype(F32)
    row2 = lax.broadcasted_iota(jnp.int32, (2 * WINDOW, 1), 0)
    late_after_scores = (5,)
    for n, (i, m) in enumerate((i, m) for i in range(nb) for m in range(N_PAIRS)):
        hk = (2 * m) // Q_PER_KV
        rows = slice(i * WINDOW, (i + 1) * WINDOW)
        keys = slice(i * WINDOW, (i + 2) * WINDOW)
        cols = slice(m * LANES, (m + 1) * LANES)
        qb = q_s[rows, cols]
        qq = jnp.concatenate([jnp.where(lo, qb, 0.0), jnp.where(lo, 0.0, qb)],
                             axis=0).astype(BF16)
        s = _dot(qq, kt_s[hk * LANES:(hk + 1) * LANES, keys]) + tab_s[m]
        if i == 0:
            s = jnp.concatenate([s[:, 0:WINDOW] + first_block_bias, s[:, WINDOW:]], axis=1)
        if n in late_after_scores:
            late_stage()
        sink = jnp.where(row2 < WINDOW, sink_ref[layer, hk, (2 * m) % Q_PER_KV],
                         sink_ref[layer, hk, (2 * m + 1) % Q_PER_KV])
        mx = jnp.maximum(jnp.max(s, axis=-1, keepdims=True), sink)
        p_un = jnp.exp(s - mx)
        z = jnp.sum(p_un, axis=-1, keepdims=True) + jnp.exp(sink - mx)
        pb = p_un.astype(BF16)
        o = (_dot(pb[0:WINDOW], va_s[hk, keys, :]) +
             _dot(pb[WINDOW:], vb_s[hk, keys, :]))
        rz = 1.0 / z
        bo_s[rows, cols] = o * jnp.where(lo, rz[0:WINDOW], rz[WINDOW:])

    kt_s[:, 0:WINDOW] = kt_s[:, T:T + WINDOW]
    va_s[:, 0:WINDOW, :] = va_s[:, T:T + WINDOW, :]
    vb_s[:, 0:WINDOW, :] = vb_s[:, T:T + WINDOW, :]

    late_stage()
    m_s[:, cw:cw + ATTN_WIDTH] = _rms(
        bo_s[...], vec_ref[ROW_GOUT:ROW_GOUT + 1, cw:cw + ATTN_WIDTH]).astype(BF16)
    o_proj = _dot(m_s[...], w_o_ref[...])
    late_stage()
    x1 = x_ref[0] + _rms(o_proj, vec_ref[ROW_G_PM:ROW_G_PM + 1, :d])
    x1n_s[...] = x1
    h2_s[...] = _rms(x1, vec_ref[ROW_G_PFF:ROW_G_PFF + 1, :d]).astype(BF16)
    late_stage()

    ffn_up(0)
    y_ref[0] = x1_s[...] + _rms(f_s[...], vec_ref[ROW_G_PF:ROW_G_PF + 1, :d])
    x1_s[...] = x1n_s[...]
    ffn_down(0)
    for c in range(1, n_chunks // 2):
        ffn_up(c)
        ffn_down(c)


def _resident(shape):
    return pl.BlockSpec(shape, lambda *_: (0,) * len(shape), pipeline_mode=pl.Buffered(1))


def _layer_row(arr, layer):
    block = (1,) + arr.shape[1:]
    return pl.BlockSpec(block, lambda *_: (layer,) + (0,) * (arr.ndim - 1),
                        pipeline_mode=pl.Buffered(1))


def _prompt_call(x, layer, sinks, w_in, w_o, w_up, w_down, norms, w_sp, b_sp):
    B, S, D = x.shape
    T = PROMPT_TILE
    assert S % T == 0 and T % WINDOW == 0 and WINDOW == CHUNK
    assert (w_up.shape[1] // FF_CHUNK) % 2 == 0
    nj = S // T
    n_tiles = B * nj
    cur = lambda s: jnp.minimum(s, n_tiles - 1)
    prev = lambda s: jnp.maximum(s - 1, 0)
    tile_in = pl.BlockSpec((1, T, D), lambda s: (cur(s) // nj, cur(s) % nj, 0))
    tile_out = pl.BlockSpec((1, T, D), lambda s: (prev(s) // nj, prev(s) % nj, 0))
    last = lambda r, c: pl.BlockSpec((1, r, c), lambda s: (cur(s) // nj, 0, 0))
    return pl.pallas_call(
        functools.partial(_prompt_kernel, n_tiles, nj, layer),
        grid=(n_tiles + 1,),
        in_specs=[pl.BlockSpec(memory_space=pltpu.SMEM), tile_in,
                  _resident(w_in.shape), _resident(w_o.shape), _resident(w_up.shape),
                  _resident(w_down.shape)] + [_layer_row(a, layer) for a in norms] +
                 [_layer_row(w_sp, layer), _layer_row(b_sp, layer)],
        out_specs=[tile_out, last(KV_WIDTH, WINDOW), last(KV_WIDTH, WINDOW),
                   last(CHUNK, CHUNK_WIDTH)],
        out_shape=[jax.ShapeDtypeStruct((B, S, D), F32),
                   jax.ShapeDtypeStruct((B, KV_WIDTH, WINDOW), F32),
                   jax.ShapeDtypeStruct((B, KV_WIDTH, WINDOW), F32),
                   jax.ShapeDtypeStruct((B, CHUNK, CHUNK_WIDTH), F32)],
        scratch_shapes=[pltpu.VMEM((8, D), F32),
                        pltpu.VMEM((CHUNK, CHUNK_WIDTH), F32),
                        pltpu.VMEM((N_CHUNK_HEADS // 2, CHUNK, 2 * CHUNK), BF16),
                        pltpu.VMEM((N_PAIRS, 2 * WINDOW, 2 * WINDOW), F32),
                        pltpu.VMEM((2 * LANES, T + WINDOW), BF16),
                        pltpu.VMEM((N_KV_HEADS, T + WINDOW, LANES), BF16),
                        pltpu.VMEM((N_KV_HEADS, T + WINDOW, LANES), BF16),
                        pltpu.VMEM((T, CHUNK_WIDTH), F32),
                        pltpu.VMEM((T, ATTN_WIDTH), F32),
                        pltpu.VMEM((T, CHUNK_WIDTH + ATTN_WIDTH), BF16),
                        pltpu.VMEM((T, D), F32),
                        pltpu.VMEM((T, D), F32),
                        pltpu.VMEM((T, D), BF16),
                        pltpu.VMEM((T, ATTN_WIDTH), F32),
                        pltpu.VMEM((T, D), BF16),
                        pltpu.VMEM((T, FF_CHUNK), BF16),
                        pltpu.VMEM((T, D), F32)],
        compiler_params=pltpu.CompilerParams(
            dimension_semantics=("arbitrary",), vmem_limit_bytes=VMEM_LIMIT),
        name="prompt_layer",
    )(sinks, x, w_in, w_o, w_up, w_down, *norms, w_sp, b_sp)


def _sample_kernel(L, layer, sink_ref, w4_ref, b4_ref, x_ref, ckt_ref, cvt_ref, w_in_ref,
                   w_o_ref, w_up_ref, w_down_ref,
                   g_pre, g_pm, g_pff, g_pf, ln_g, ln_b, g_oc, g_oa,
                   y_ref, vn_ref, wkt_ref, wvt_ref,
                   vec_ref, q_s, kvt_s, a_s, b_s):
    g = pl.program_id(0)
    n_steps = pl.num_programs(0)
    bb, hd, W = ckt_ref.shape
    n, d = q_s.shape[0], x_ref.shape[-1]
    n_heads = N_ATTN_HEADS
    pair_rows = 2 * L
    assert pair_rows == 8 and hd == LANES and W == LANES
    batch_per_block = LANES // L

    @pl.when(g == 0)
    def _project_and_gate():
        cw = CHUNK_WIDTH
        _fill_vector_table(vec_ref, g_pre, g_pm, g_pff, g_pf, ln_g, ln_b, g_oc, g_oa)
        h = _rms(x_ref[...].reshape(n, d), vec_ref[ROW_G_PRE:ROW_G_PRE + 1, :d]).astype(BF16)
        u = jax.nn.gelu(_dot(h, w_in_ref[:, 0:cw]))
        v = _layer_norm(jax.nn.gelu(_dot(h, w_in_ref[:, cw:2 * cw])),
                        vec_ref[ROW_LN:ROW_LN + 1, 0:cw], vec_ref[ROW_LN:ROW_LN + 1, cw:2 * cw])
        vn_ref[...] = v.reshape(vn_ref.shape)
        q_s[...] = _dot(h, w_in_ref[:, 2 * cw:2 * cw + ATTN_WIDTH]) * ATTN_SCALE
        kv0 = 2 * cw + ATTN_WIDTH
        kvt_s[...] = _dot(h, w_in_ref[:, kv0:kv0 + 2 * KV_WIDTH]).T
        t_of_row = lax.broadcasted_iota(jnp.int32, (n, 1), 0) % L
        group_of_lane = lax.broadcasted_iota(jnp.int32, (1, cw), 1) // HEAD_DIM

        def per_group(value_of_group):
            row = jnp.zeros((1, cw), F32)
            for grp in range(N_CHUNK_HEADS):
                row = jnp.where(group_of_lane == grp, value_of_group(grp), row)
            return row

        s = jnp.zeros(v.shape, F32)
        for t in range(L):
            s = jnp.where(t_of_row == t, per_group(lambda grp: b4_ref[layer, grp, t]), s)
        for delta in range(L):
            coef = jnp.zeros(v.shape, F32)
            for t in range(delta, L):
                w_row = per_group(lambda grp: w4_ref[grp, t, t - delta])
                coef = jnp.where(t_of_row == t, w_row, coef)
            vs = v if delta == 0 else pltpu.roll(v, delta, axis=0)
            s = s + coef * vs
        a_s[...] = u * s

    rows_all = n_heads * pair_rows
    r_idx = lax.broadcasted_iota(jnp.int32, (rows_all, LANES), 0)
    l_idx = lax.broadcasted_iota(jnp.int32, (rows_all, LANES), 1)
    t_idx = r_idx % L
    odd = (r_idx % pair_rows) >= L
    head_of_row = lax.broadcasted_iota(jnp.int32, (rows_all, 1), 0) // pair_rows
    sink = jnp.zeros((rows_all, 1), F32)
    slope = jnp.zeros((rows_all, 1), F32)
    for hh in range(n_heads):
        sink = jnp.where(head_of_row == hh, sink_ref[layer, hh // Q_PER_KV, hh % Q_PER_KV], sink)
        slope = jnp.where(head_of_row == hh, _alibi_slope(hh), slope)
    dist_old = W + t_idx - l_idx
    bias_old = jnp.where(dist_old <= WINDOW, -(slope * dist_old.astype(F32)), -jnp.inf)
    lane8 = lax.broadcasted_iota(jnp.int32, (pair_rows, LANES), 1)
    lo8 = lane8 < HALF
    lane_w = lax.broadcasted_iota(jnp.int32, (hd, W), 1)

    first_batch = g * bb
    new_block = pl.multiple_of((first_batch // batch_per_block) * LANES, LANES)
    kt_new = kvt_s[0:KV_WIDTH, pl.ds(new_block, LANES)]
    vt_new = kvt_s[KV_WIDTH:2 * KV_WIDTH, pl.ds(new_block, LANES)]
    kt_new_b = kt_new.astype(BF16)
    vt_new_b = vt_new.astype(BF16)
    for p in range(bb // 2):
        off = ((first_batch + 2 * p) % batch_per_block) * L
        row0 = pl.multiple_of((first_batch + 2 * p) * L, pair_rows)
        tiles = []
        for m in range(N_PAIRS):
            qb = q_s[pl.ds(row0, pair_rows), m * LANES:(m + 1) * LANES]
            qb_sw = pltpu.roll(qb, HALF, axis=1)
            if (2 * m) // Q_PER_KV == 0:
                tiles += [jnp.where(lo8, qb, 0.0), jnp.where(lo8, qb_sw, 0.0)]
            else:
                tiles += [jnp.where(lo8, 0.0, qb_sw), jnp.where(lo8, 0.0, qb)]
        lhs = jnp.concatenate(tiles, axis=0).astype(BF16)
        s_pair = []
        for e in range(2):
            rhs = jnp.concatenate([ckt_ref[2 * p + e].astype(BF16), kt_new_b], axis=1)
            s_pair.append(_dot(lhs, rhs))
        sc = jnp.where(jnp.concatenate([odd, odd], axis=1), s_pair[1], s_pair[0])
        rel = l_idx - off - jnp.where(odd, L, 0)
        bias_new = jnp.where((rel >= 0) & (rel <= t_idx),
                             -(slope * (t_idx - rel).astype(F32)), -jnp.inf)
        sc = sc + jnp.concatenate([bias_old, bias_new], axis=1)
        mx = jnp.maximum(jnp.max(sc, axis=-1, keepdims=True), sink)
        p_un = jnp.exp(sc - mx)
        z = jnp.sum(p_un, axis=-1, keepdims=True) + jnp.exp(sink - mx)
        pb = p_un.astype(BF16)
        o_pair = []
        for e in range(2):
            vt = jnp.concatenate([cvt_ref[2 * p + e].astype(BF16), vt_new_b], axis=1)
            o_pair.append(lax.dot_general(pb, vt, (((1,), (1,)), ((), ())),
                                          preferred_element_type=F32))
        o = jnp.where(odd, o_pair[1], o_pair[0]) * (1.0 / z)
        for m in range(N_PAIRS):
            kv_head = (2 * m) // Q_PER_KV
            o_even = o[(2 * m) * pair_rows:(2 * m + 1) * pair_rows]
            o_odd = o[(2 * m + 1) * pair_rows:(2 * m + 2) * pair_rows]
            if kv_head == 0:
                o_odd = pltpu.roll(o_odd, HALF, axis=1)
            else:
                o_even = pltpu.roll(o_even, HALF, axis=1)
            b_s[pl.ds(row0, pair_rows), m * LANES:(m + 1) * LANES] = jnp.where(lo8, o_even, o_odd)
        for e in range(2):
            shift_new = (W - L) - (off + e * L)
            keep = lane_w < W - L
            wkt_ref[2 * p + e] = jnp.where(keep, pltpu.roll(ckt_ref[2 * p + e], W - L, axis=1),
                                           pltpu.roll(kt_new, shift_new, axis=1))
            wvt_ref[2 * p + e] = jnp.where(keep, pltpu.roll(cvt_ref[2 * p + e], W - L, axis=1),
                                           pltpu.roll(vt_new, shift_new, axis=1))

    @pl.when(g == n_steps - 1)
    def _merge_and_ffn():
        x1 = _merge(x_ref[...].reshape(n, d), a_s[...], b_s[...], w_o_ref, vec_ref)
        y_ref[...] = _ffn(x1, w_up_ref, w_down_ref, vec_ref).reshape(y_ref.shape)


def _sample_call(x, layer, sinks, w4, b_sp, cache_kt, cache_vt, w_in, w_o, w_up, w_down,
                 norms):
    Bd, L, D = x.shape
    hd, W = cache_kt.shape[1:]
    n = Bd * L
    bb = SAMPLE_BB
    assert Bd % bb == 0 and bb % 2 == 0 and (LANES // L) % bb == 0
    smem = pl.BlockSpec(memory_space=pltpu.SMEM)
    blk = pl.BlockSpec((bb, hd, W), lambda g: (g, 0, 0))
    const = lambda shape: pl.BlockSpec(shape, lambda g: (0,) * len(shape))
    weights = (w_in, w_o, w_up, w_down)
    return pl.pallas_call(
        functools.partial(_sample_kernel, L, layer),
        grid=(Bd // bb,),
        in_specs=[smem, smem, smem, _resident(x.shape), blk, blk] +
                 [_resident(w.shape) for w in weights] + [_layer_row(a, layer) for a in norms],
        out_specs=[const((Bd, L, D)), const((Bd, L, CHUNK_WIDTH)), blk, blk],
        out_shape=[jax.ShapeDtypeStruct((Bd, L, D), F32),
                   jax.ShapeDtypeStruct((Bd, L, CHUNK_WIDTH), F32),
                   jax.ShapeDtypeStruct(cache_kt.shape, F32),
                   jax.ShapeDtypeStruct(cache_vt.shape, F32)],
        scratch_shapes=[pltpu.VMEM((8, D), F32),
                        pltpu.VMEM((n, ATTN_WIDTH), F32),
                        pltpu.VMEM((2 * KV_WIDTH, n), F32),
                        pltpu.VMEM((n, CHUNK_WIDTH), F32),
                        pltpu.VMEM((n, ATTN_WIDTH), F32)],
        compiler_params=pltpu.CompilerParams(
            dimension_semantics=("arbitrary",), vmem_limit_bytes=VMEM_LIMIT),
        name="sample_layer",
    )(sinks, w4, b_sp, x, cache_kt, cache_vt, *weights, *norms)


def _heads_last(t):
    b, _, w = t.shape
    return t.reshape(b, N_KV_HEADS, HEAD_DIM, w).transpose(0, 3, 1, 2)


def kernel(x_prompt, x_sample, cache_win_k, cache_win_v, w_in, g_pre_mix, ln_v_g, ln_v_b,
           w_spatial, b_spatial, attn_sinks, g_out_chunk, g_out_attn, w_o, g_post_mix,
           g_pre_ffn, w_up, w_down, g_post_ffn):
    depth = w_in.shape[0]
    Bd, L, D = x_sample.shape
    W = cache_win_k.shape[2]
    norms = (g_pre_mix, g_post_mix, g_pre_ffn, g_post_ffn, ln_v_g, ln_v_b, g_out_chunk,
             g_out_attn)
    yp, ys = x_prompt, x_sample
    wk_p, wv_p, cv_p, wk_s, wv_s, cv_s = [], [], [], [], [], []
    for l in range(depth):
        w_in_b = w_in[l].astype(BF16)
        w_o_b = w_o[l].astype(BF16)
        w_up_b = w_up[l].astype(BF16)
        w_down_b = w_down[l].astype(BF16)

        yp, wk, wv, cv = _prompt_call(yp, l, attn_sinks, w_in_b, w_o_b, w_up_b, w_down_b, norms,
                                      w_spatial, b_spatial)
        wk_p.append(_heads_last(wk))
        wv_p.append(_heads_last(wv))
        cv_p.append(cv)

        w4 = w_spatial[l, :, :L, :L]
        cache_kt = cache_win_k[l].transpose(0, 2, 3, 1).reshape(Bd, KV_WIDTH, W)
        cache_vt = cache_win_v[l].transpose(0, 2, 3, 1).reshape(Bd, KV_WIDTH, W)
        ys, cv, wk, wv = _sample_call(ys, l, attn_sinks, w4, b_spatial, cache_kt, cache_vt,
                                      w_in_b, w_o_b, w_up_b, w_down_b, norms)
        wk_s.append(_heads_last(wk))
        wv_s.append(_heads_last(wv))
        cv_s.append(cv)
    return (yp, ys, jnp.stack(wk_p), jnp.stack(wv_p), jnp.stack(cv_p),
            jnp.stack(wk_s), jnp.stack(wv_s), jnp.stack(cv_s))
```

```python
import functools

import jax
import jax.numpy as jnp
from jax import lax
from jax.experimental import pallas as pl
from jax.experimental.pallas import tpu as pltpu

F32 = jnp.float32
BF16 = jnp.bfloat16

HEAD_DIM = 64
N_CHUNK_HEADS = 8
N_ATTN_HEADS = 8
N_KV_HEADS = 2
Q_PER_KV = N_ATTN_HEADS // N_KV_HEADS
CHUNK_WIDTH = N_CHUNK_HEADS * HEAD_DIM
ATTN_WIDTH = N_ATTN_HEADS * HEAD_DIM
KV_WIDTH = N_KV_HEADS * HEAD_DIM
CHUNK = 128
WINDOW = 128
EPS = 1e-6
ATTN_SCALE = HEAD_DIM ** -0.5

LANES = 128
HALF = LANES // 2
N_PAIRS = ATTN_WIDTH // LANES
VMEM_LIMIT = 56 * 1024 * 1024

PROMPT_TILE = 512
FF_CHUNK = 512
EARLY_CHUNKS = 1
SAMPLE_BB = 16

ROW_G_PRE, ROW_G_PM, ROW_G_PFF, ROW_G_PF, ROW_LN, ROW_GOUT = range(6)


def _dot(a, b):
    return jnp.dot(a, b, preferred_element_type=F32)


def _rms(x, g):
    return x * lax.rsqrt(jnp.mean(x * x, axis=-1, keepdims=True) + EPS) * g


def _layer_norm(x, g, b):
    xc = x - jnp.mean(x, axis=-1, keepdims=True)
    return xc * lax.rsqrt(jnp.mean(xc * xc, axis=-1, keepdims=True) + EPS) * g + b


def _alibi_slope(head):
    return 2.0 ** (-8.0 * (head + 1) / N_ATTN_HEADS)


def _merge(x, a_out, b_out, w_o_ref, vec_ref):
    d = x.shape[-1]
    cw = CHUNK_WIDTH
    ra = _rms(a_out, vec_ref[ROW_GOUT:ROW_GOUT + 1, 0:cw]).astype(BF16)
    rb = _rms(b_out, vec_ref[ROW_GOUT:ROW_GOUT + 1, cw:cw + ATTN_WIDTH]).astype(BF16)
    o = _dot(ra, w_o_ref[0:cw, :]) + _dot(rb, w_o_ref[cw:cw + ATTN_WIDTH, :])
    return x + _rms(o, vec_ref[ROW_G_PM:ROW_G_PM + 1, :d])


def _ffn(x1, w_up_ref, w_down_ref, vec_ref):
    d = x1.shape[-1]
    h2 = _rms(x1, vec_ref[ROW_G_PFF:ROW_G_PFF + 1, :d]).astype(BF16)
    d_ff = w_up_ref.shape[1]
    f = jnp.zeros(x1.shape, F32)
    for c in range(d_ff // FF_CHUNK):
        up = _dot(h2, w_up_ref[:, c * FF_CHUNK:(c + 1) * FF_CHUNK])
        r = jnp.maximum(up, 0.0)
        f = f + _dot((r * r).astype(BF16), w_down_ref[c * FF_CHUNK:(c + 1) * FF_CHUNK, :])
    return x1 + _rms(f, vec_ref[ROW_G_PF:ROW_G_PF + 1, :d])


def _fill_vector_table(vec_s, g_pre, g_pm, g_pff, g_pf, ln_g, ln_b, g_oc, g_oa):
    d = g_pre.shape[-1]
    cw = CHUNK_WIDTH
    vec_s[ROW_G_PRE:ROW_G_PRE + 1, 0:d] = g_pre[...]
    vec_s[ROW_G_PM:ROW_G_PM + 1, 0:d] = g_pm[...]
    vec_s[ROW_G_PFF:ROW_G_PFF + 1, 0:d] = g_pff[...]
    vec_s[ROW_G_PF:ROW_G_PF + 1, 0:d] = g_pf[...]
    vec_s[ROW_LN:ROW_LN + 1, 0:cw] = ln_g[...]
    vec_s[ROW_LN:ROW_LN + 1, cw:2 * cw] = ln_b[...]
    vec_s[ROW_GOUT:ROW_GOUT + 1, 0:cw] = g_oc[...]
    vec_s[ROW_GOUT:ROW_GOUT + 1, cw:cw + ATTN_WIDTH] = g_oa[...]


def _prompt_kernel(n_tiles, nj, layer, sink_ref, x_ref, w_in_ref, w_o_ref, w_up_ref, w_down_ref,
                   g_pre, g_pm, g_pff, g_pf, ln_g, ln_b, g_oc, g_oa, wsp_ref, bsp_ref,
                   y_ref, wk_ref, wv_ref, cv_ref,
                   vec_ref, bias_s, wcat_s, tab_s, kt_s, va_s, vb_s, a_s, bo_s, m_s, x1_s, x1n_s,
                   h_s, q_s, h2_s, act_s, f_s):
    step = pl.program_id(0)
    j = jnp.minimum(step, n_tiles - 1) % nj
    T, d = x_ref.shape[1], x_ref.shape[2]
    nb = T // WINDOW
    cw = CHUNK_WIDTH
    n_chunks = w_up_ref.shape[1] // FF_CHUNK

    lane = lax.broadcasted_iota(jnp.int32, (WINDOW, LANES), 1)
    lo = lane < HALF

    @pl.when(step == 0)
    def _init_tables():
        x1_s[...] = jnp.zeros(x1_s.shape, F32)
        h2_s[...] = jnp.zeros(h2_s.shape, BF16)
        f_s[...] = jnp.zeros(f_s.shape, F32)
        _fill_vector_table(vec_ref, g_pre, g_pm, g_pff, g_pf, ln_g, ln_b, g_oc, g_oa)
        t_idx = lax.broadcasted_iota(jnp.int32, (CHUNK, CHUNK), 0)
        s_idx = lax.broadcasted_iota(jnp.int32, (CHUNK, CHUNK), 1)
        for g in range(N_CHUNK_HEADS):
            wm = jnp.where(s_idx <= t_idx, wsp_ref[0, g], 0.0).astype(BF16)
            wcat_s[g // 2, :, (g % 2) * CHUNK:(g % 2 + 1) * CHUNK] = wm
        b_rows = jnp.concatenate(
            [bsp_ref[0], jnp.zeros((CHUNK - N_CHUNK_HEADS, CHUNK), F32)], axis=0)
        b_cols = b_rows.T
        for p in range(cw // LANES):
            bias_s[:, p * LANES:(p + 1) * LANES] = jnp.where(
                lo, jnp.broadcast_to(b_cols[:, 2 * p:2 * p + 1], (CHUNK, LANES)),
                jnp.broadcast_to(b_cols[:, 2 * p + 1:2 * p + 2], (CHUNK, LANES)))
        a_idx = lax.broadcasted_iota(jnp.int32, (WINDOW, 2 * WINDOW), 0)
        c_idx = lax.broadcasted_iota(jnp.int32, (WINDOW, 2 * WINDOW), 1)
        dist = WINDOW + a_idx - c_idx
        valid = (dist >= 0) & (dist <= WINDOW)
        for hd in range(N_ATTN_HEADS):
            bias = jnp.where(valid, -(_alibi_slope(hd) * dist.astype(F32)), -jnp.inf)
            tab_s[hd // 2, (hd % 2) * WINDOW:(hd % 2 + 1) * WINDOW, :] = bias

    @pl.when(j == 0)
    def _clear_carry():
        kt_s[:, 0:WINDOW] = jnp.zeros((2 * LANES, WINDOW), BF16)
        va_s[:, 0:WINDOW, :] = jnp.zeros((N_KV_HEADS, WINDOW, LANES), BF16)
        vb_s[:, 0:WINDOW, :] = jnp.zeros((N_KV_HEADS, WINDOW, LANES), BF16)

    def ffn_up(c):
        r = jnp.maximum(_dot(h2_s[...], w_up_ref[:, c * FF_CHUNK:(c + 1) * FF_CHUNK]), 0.0)
        act_s[...] = (r * r).astype(BF16)

    def ffn_down(c):
        part = _dot(act_s[...], w_down_ref[c * FF_CHUNK:(c + 1) * FF_CHUNK, :])
        if c == 0:
            f_s[...] = part
        else:
            f_s[...] += part

    late = iter([functools.partial(fn, c) for c in range(EARLY_CHUNKS, n_chunks)
                 for fn in (ffn_up, ffn_down)])

    def late_stage(count=1):
        for _ in range(count):
            next(late)()

    late_stage()
    h_s[...] = _rms(x_ref[0], vec_ref[ROW_G_PRE:ROW_G_PRE + 1, :d]).astype(BF16)
    u_pre = _dot(h_s[...], w_in_ref[:, 0:cw])
    v_pre = _dot(h_s[...], w_in_ref[:, cw:2 * cw])
    late_stage()
    u = jax.nn.gelu(u_pre)
    late_stage()
    v = _layer_norm(jax.nn.gelu(v_pre), vec_ref[ROW_LN:ROW_LN + 1, 0:cw],
                    vec_ref[ROW_LN:ROW_LN + 1, cw:2 * cw])
    kv0 = 2 * cw + ATTN_WIDTH
    kv = _dot(h_s[...], w_in_ref[:, kv0:kv0 + 2 * KV_WIDTH])
    k = kv[:, 0:KV_WIDTH]
    val = kv[:, KV_WIDTH:2 * KV_WIDTH]
    q0 = 2 * cw
    q_s[...] = _dot(h_s[...], w_in_ref[:, q0:q0 + ATTN_WIDTH]) * ATTN_SCALE
    late_stage()

    kt_f = k.T
    wk_ref[0] = kt_f[:, T - WINDOW:]
    wv_ref[0] = val[T - WINDOW:, :].T
    cv_ref[0] = v[T - CHUNK:, :]

    for c in range(nb):
        rows = slice(c * CHUNK, (c + 1) * CHUNK)
        for p in range(cw // LANES):
            cols = slice(p * LANES, (p + 1) * LANES)
            vcp = v[rows, cols]
            rhs = jnp.concatenate([jnp.where(lo, vcp, 0.0), jnp.where(lo, 0.0, vcp)],
                                  axis=0).astype(BF16)
            s_cp = _dot(wcat_s[p], rhs) + bias_s[:, cols]
            a_s[rows, cols] = u[rows, cols] * s_cp
    m_s[:, 0:cw] = _rms(a_s[...], vec_ref[ROW_GOUT:ROW_GOUT + 1, 0:cw]).astype(BF16)

    kt = kt_f.astype(BF16)
    for hk in range(N_KV_HEADS):
        kth = kt[hk * HEAD_DIM:(hk + 1) * HEAD_DIM, :]
        kt_s[hk * LANES:hk * LANES + HEAD_DIM, WINDOW:] = kth
        kt_s[hk * LANES + HEAD_DIM:(hk + 1) * LANES, WINDOW:] = kth
    lane_t = lax.broadcasted_iota(jnp.int32, (T, LANES), 1)
    lo_t = lane_t < HALF
    val_sw = pltpu.roll(val, HALF, axis=1)
    va_s[0, WINDOW:, :] = jnp.where(lo_t, val, 0.0).astype(BF16)
    vb_s[0, WINDOW:, :] = jnp.where(lo_t, 0.0, val_sw).astype(BF16)
    va_s[1, WINDOW:, :] = jnp.where(lo_t, val_sw, 0.0).astype(BF16)
    vb_s[1, WINDOW:, :] = jnp.where(lo_t, 0.0, val).astype(BF16)

    first_block_bias = jnp.where(j == 0, -jnp.inf, 0.0).astype(F32)
    row2 = lax.broadcasted_iota(jnp.int32, (2 * WINDOW, 1), 0)
    n_units = nb * N_PAIRS
    n_mid = 2 * (n_chunks - EARLY_CHUNKS) - 9
    late_after_scores = tuple(1 + (k * n_units) // n_mid for k in range(n_mid))
    for n, (i, m) in enumerate((i, m) for i in range(nb) for m in range(N_PAIRS)):
        hk = (2 * m) // Q_PER_KV
        rows = slice(i * WINDOW, (i + 1) * WINDOW)
        keys = slice(i * WINDOW, (i + 2) * WINDOW)
        cols = slice(m * LANES, (m + 1) * LANES)
        qb = q_s[rows, cols]
        qq = jnp.concatenate([jnp.where(lo, qb, 0.0), jnp.where(lo, 0.0, qb)],
                             axis=0).astype(BF16)
        s = _dot(qq, kt_s[hk * LANES:(hk + 1) * LANES, keys]) + tab_s[m]
        if i == 0:
            s = jnp.concatenate([s[:, 0:WINDOW] + first_block_bias, s[:, WINDOW:]], axis=1)
        if n in late_after_scores:
            late_stage()
        sink = jnp.where(row2 < WINDOW, sink_ref[layer, hk, (2 * m) % Q_PER_KV],
                         sink_ref[layer, hk, (2 * m + 1) % Q_PER_KV])
        mx = jnp.maximum(jnp.max(s, axis=-1, keepdims=True), sink)
        p_un = jnp.exp(s - mx)
        z = jnp.sum(p_un, axis=-1, keepdims=True) + jnp.exp(sink - mx)
        pb = p_un.astype(BF16)
        o = (_dot(pb[0:WINDOW], va_s[hk, keys, :]) +
             _dot(pb[WINDOW:], vb_s[hk, keys, :]))
        rz = 1.0 / z
        bo_s[rows, cols] = o * jnp.where(lo, rz[0:WINDOW], rz[WINDOW:])

    kt_s[:, 0:WINDOW] = kt_s[:, T:T + WINDOW]
    va_s[:, 0:WINDOW, :] = va_s[:, T:T + WINDOW, :]
    vb_s[:, 0:WINDOW, :] = vb_s[:, T:T + WINDOW, :]

    late_stage()
    m_s[:, cw:cw + ATTN_WIDTH] = _rms(
        bo_s[...], vec_ref[ROW_GOUT:ROW_GOUT + 1, cw:cw + ATTN_WIDTH]).astype(BF16)
    o_proj = _dot(m_s[...], w_o_ref[...])
    late_stage(3)
    x1 = x_ref[0] + _rms(o_proj, vec_ref[ROW_G_PM:ROW_G_PM + 1, :d])
    x1n_s[...] = x1
    h2_s[...] = _rms(x1, vec_ref[ROW_G_PFF:ROW_G_PFF + 1, :d]).astype(BF16)
    late_stage()

    ffn_up(0)
    y_ref[0] = x1_s[...] + _rms(f_s[...], vec_ref[ROW_G_PF:ROW_G_PF + 1, :d])
    x1_s[...] = x1n_s[...]
    ffn_down(0)
    for c in range(1, EARLY_CHUNKS):
        ffn_up(c)
        ffn_down(c)


def _resident(shape):
    return pl.BlockSpec(shape, lambda *_: (0,) * len(shape), pipeline_mode=pl.Buffered(1))


def _layer_row(arr, layer):
    block = (1,) + arr.shape[1:]
    return pl.BlockSpec(block, lambda *_: (layer,) + (0,) * (arr.ndim - 1),
                        pipeline_mode=pl.Buffered(1))


def _prompt_call(x, layer, sinks, w_in, w_o, w_up, w_down, norms, w_sp, b_sp):
    B, S, D = x.shape
    T = PROMPT_TILE
    assert S % T == 0 and T % WINDOW == 0 and WINDOW == CHUNK
    assert 1 <= EARLY_CHUNKS and 2 * (w_up.shape[1] // FF_CHUNK - EARLY_CHUNKS) >= 9
    nj = S // T
    n_tiles = B * nj
    cur = lambda s: jnp.minimum(s, n_tiles - 1)
    prev = lambda s: jnp.maximum(s - 1, 0)
    tile_in = pl.BlockSpec((1, T, D), lambda s: (cur(s) // nj, cur(s) % nj, 0))
    tile_out = pl.BlockSpec((1, T, D), lambda s: (prev(s) // nj, prev(s) % nj, 0))
    last = lambda r, c: pl.BlockSpec((1, r, c), lambda s: (cur(s) // nj, 0, 0))
    return pl.pallas_call(
        functools.partial(_prompt_kernel, n_tiles, nj, layer),
        grid=(n_tiles + 1,),
        in_specs=[pl.BlockSpec(memory_space=pltpu.SMEM), tile_in,
                  _resident(w_in.shape), _resident(w_o.shape), _resident(w_up.shape),
                  _resident(w_down.shape)] + [_layer_row(a, layer) for a in norms] +
                 [_layer_row(w_sp, layer), _layer_row(b_sp, layer)],
        out_specs=[tile_out, last(KV_WIDTH, WINDOW), last(KV_WIDTH, WINDOW),
                   last(CHUNK, CHUNK_WIDTH)],
        out_shape=[jax.ShapeDtypeStruct((B, S, D), F32),
                   jax.ShapeDtypeStruct((B, KV_WIDTH, WINDOW), F32),
                   jax.ShapeDtypeStruct((B, KV_WIDTH, WINDOW), F32),
                   jax.ShapeDtypeStruct((B, CHUNK, CHUNK_WIDTH), F32)],
        scratch_shapes=[pltpu.VMEM((8, D), F32),
                        pltpu.VMEM((CHUNK, CHUNK_WIDTH), F32),
                        pltpu.VMEM((N_CHUNK_HEADS // 2, CHUNK, 2 * CHUNK), BF16),
                        pltpu.VMEM((N_PAIRS, 2 * WINDOW, 2 * WINDOW), F32),
                        pltpu.VMEM((2 * LANES, T + WINDOW), BF16),
                        pltpu.VMEM((N_KV_HEADS, T + WINDOW, LANES), BF16),
                        pltpu.VMEM((N_KV_HEADS, T + WINDOW, LANES), BF16),
                        pltpu.VMEM((T, CHUNK_WIDTH), F32),
                        pltpu.VMEM((T, ATTN_WIDTH), F32),
                        pltpu.VMEM((T, CHUNK_WIDTH + ATTN_WIDTH), BF16),
                        pltpu.VMEM((T, D), F32),
                        pltpu.VMEM((T, D), F32),
                        pltpu.VMEM((T, D), BF16),
                        pltpu.VMEM((T, ATTN_WIDTH), F32),
                        pltpu.VMEM((T, D), BF16),
                        pltpu.VMEM((T, FF_CHUNK), BF16),
                        pltpu.VMEM((T, D), F32)],
        compiler_params=pltpu.CompilerParams(
            dimension_semantics=("arbitrary",), vmem_limit_bytes=VMEM_LIMIT),
        name="prompt_layer",
    )(sinks, x, w_in, w_o, w_up, w_down, *norms, w_sp, b_sp)


def _sample_kernel(L, layer, sink_ref, w4_ref, b4_ref, x_ref, ckt_ref, cvt_ref, w_in_ref,
                   w_o_ref, w_up_ref, w_down_ref,
                   g_pre, g_pm, g_pff, g_pf, ln_g, ln_b, g_oc, g_oa,
                   y_ref, vn_ref, wkt_ref, wvt_ref,
                   vec_ref, q_s, kvt_s, a_s, b_s):
    g = pl.program_id(0)
    n_steps = pl.num_programs(0)
    bb, hd, W = ckt_ref.shape
    n, d = q_s.shape[0], x_ref.shape[-1]
    n_heads = N_ATTN_HEADS
    pair_rows = 2 * L
    assert pair_rows == 8 and hd == LANES and W == LANES
    batch_per_block = LANES // L

    @pl.when(g == 0)
    def _project_and_gate():
        cw = CHUNK_WIDTH
        _fill_vector_table(vec_ref, g_pre, g_pm, g_pff, g_pf, ln_g, ln_b, g_oc, g_oa)
        h = _rms(x_ref[...].reshape(n, d), vec_ref[ROW_G_PRE:ROW_G_PRE + 1, :d]).astype(BF16)
        u = jax.nn.gelu(_dot(h, w_in_ref[:, 0:cw]))
        v = _layer_norm(jax.nn.gelu(_dot(h, w_in_ref[:, cw:2 * cw])),
                        vec_ref[ROW_LN:ROW_LN + 1, 0:cw], vec_ref[ROW_LN:ROW_LN + 1, cw:2 * cw])
        vn_ref[...] = v.reshape(vn_ref.shape)
        q_s[...] = _dot(h, w_in_ref[:, 2 * cw:2 * cw + ATTN_WIDTH]) * ATTN_SCALE
        kv0 = 2 * cw + ATTN_WIDTH
        kvt_s[...] = _dot(h, w_in_ref[:, kv0:kv0 + 2 * KV_WIDTH]).T
        t_of_row = lax.broadcasted_iota(jnp.int32, (n, 1), 0) % L
        group_of_lane = lax.broadcasted_iota(jnp.int32, (1, cw), 1) // HEAD_DIM

        def per_group(value_of_group):
            row = jnp.zeros((1, cw), F32)
            for grp in range(N_CHUNK_HEADS):
                row = jnp.where(group_of_lane == grp, value_of_group(grp), row)
            return row

        s = jnp.zeros(v.shape, F32)
        for t in range(L):
            s = jnp.where(t_of_row == t, per_group(lambda grp: b4_ref[layer, grp, t]), s)
        for delta in range(L):
            coef = jnp.zeros(v.shape, F32)
            for t in range(delta, L):
                w_row = per_group(lambda grp: w4_ref[grp, t, t - delta])
                coef = jnp.where(t_of_row == t, w_row, coef)
            vs = v if delta == 0 else pltpu.roll(v, delta, axis=0)
            s = s + coef * vs
        a_s[...] = u * s

    rows_all = n_heads * pair_rows
    r_idx = lax.broadcasted_iota(jnp.int32, (rows_all, LANES), 0)
    l_idx = lax.broadcasted_iota(jnp.int32, (rows_all, LANES), 1)
    t_idx = r_idx % L
    odd = (r_idx % pair_rows) >= L
    head_of_row = lax.broadcasted_iota(jnp.int32, (rows_all, 1), 0) // pair_rows
    sink = jnp.zeros((rows_all, 1), F32)
    slope = jnp.zeros((rows_all, 1), F32)
    for hh in range(n_heads):
        sink = jnp.where(head_of_row == hh, sink_ref[layer, hh // Q_PER_KV, hh % Q_PER_KV], sink)
        slope = jnp.where(head_of_row == hh, _alibi_slope(hh), slope)
    dist_old = W + t_idx - l_idx
    bias_old = jnp.where(dist_old <= WINDOW, -(slope * dist_old.astype(F32)), -jnp.inf)
    lane8 = lax.broadcasted_iota(jnp.int32, (pair_rows, LANES), 1)
    lo8 = lane8 < HALF
    lane_w = lax.broadcasted_iota(jnp.int32, (hd, W), 1)

    first_batch = g * bb
    new_block = pl.multiple_of((first_batch // batch_per_block) * LANES, LANES)
    kt_new = kvt_s[0:KV_WIDTH, pl.ds(new_block, LANES)]
    vt_new = kvt_s[KV_WIDTH:2 * KV_WIDTH, pl.ds(new_block, LANES)]
    kt_new_b = kt_new.astype(BF16)
    vt_new_b = vt_new.astype(BF16)
    for p in range(bb // 2):
        off = ((first_batch + 2 * p) % batch_per_block) * L
        row0 = pl.multiple_of((first_batch + 2 * p) * L, pair_rows)
        tiles = []
        for m in range(N_PAIRS):
            qb = q_s[pl.ds(row0, pair_rows), m * LANES:(m + 1) * LANES]
            qb_sw = pltpu.roll(qb, HALF, axis=1)
            if (2 * m) // Q_PER_KV == 0:
                tiles += [jnp.where(lo8, qb, 0.0), jnp.where(lo8, qb_sw, 0.0)]
            else:
                tiles += [jnp.where(lo8, 0.0, qb_sw), jnp.where(lo8, 0.0, qb)]
        lhs = jnp.concatenate(tiles, axis=0).astype(BF16)
        s_pair = []
        for e in range(2):
            rhs = jnp.concatenate([ckt_ref[2 * p + e].astype(BF16), kt_new_b], axis=1)
            s_pair.append(_dot(lhs, rhs))
        sc = jnp.where(jnp.concatenate([odd, odd], axis=1), s_pair[1], s_pair[0])
        rel = l_idx - off - jnp.where(odd, L, 0)
        bias_new = jnp.where((rel >= 0) & (rel <= t_idx),
                             -(slope * (t_idx - rel).astype(F32)), -jnp.inf)
        sc = sc + jnp.concatenate([bias_old, bias_new], axis=1)
        mx = jnp.maximum(jnp.max(sc, axis=-1, keepdims=True), sink)
        p_un = jnp.exp(sc - mx)
        z = jnp.sum(p_un, axis=-1, keepdims=True) + jnp.exp(sink - mx)
        pb = p_un.astype(BF16)
        o_pair = []
        for e in range(2):
            vt = jnp.concatenate([cvt_ref[2 * p + e].astype(BF16), vt_new_b], axis=1)
            o_pair.append(lax.dot_general(pb, vt, (((1,), (1,)), ((), ())),
                                          preferred_element_type=F32))
        o = jnp.where(odd, o_pair[1], o_pair[0]) * (1.0 / z)
        for m in range(N_PAIRS):
            kv_head = (2 * m) // Q_PER_KV
            o_even = o[(2 * m) * pair_rows:(2 * m + 1) * pair_rows]
            o_odd = o[(2 * m + 1) * pair_rows:(2 * m + 2) * pair_rows]
            if kv_head == 0:
                o_odd = pltpu.roll(o_odd, HALF, axis=1)
            else:
                o_even = pltpu.roll(o_even, HALF, axis=1)
            b_s[pl.ds(row0, pair_rows), m * LANES:(m + 1) * LANES] = jnp.where(lo8, o_even, o_odd)
        for e in range(2):
            shift_new = (W - L) - (off + e * L)
            keep = lane_w < W - L
            wkt_ref[2 * p + e] = jnp.where(keep, pltpu.roll(ckt_ref[2 * p + e], W - L, axis=1),
                                           pltpu.roll(kt_new, shift_new, axis=1))
            wvt_ref[2 * p + e] = jnp.where(keep, pltpu.roll(cvt_ref[2 * p + e], W - L, axis=1),
                                           pltpu.roll(vt_new, shift_new, axis=1))

    @pl.when(g == n_steps - 1)
    def _merge_and_ffn():
        x1 = _merge(x_ref[...].reshape(n, d), a_s[...], b_s[...], w_o_ref, vec_ref)
        y_ref[...] = _ffn(x1, w_up_ref, w_down_ref, vec_ref).reshape(y_ref.shape)


def _sample_call(x, layer, sinks, w4, b_sp, cache_kt, cache_vt, w_in, w_o, w_up, w_down,
                 norms):
    Bd, L, D = x.shape
    hd, W = cache_kt.shape[1:]
    n = Bd * L
    bb = SAMPLE_BB
    assert Bd % bb == 0 and bb % 2 == 0 and (LANES // L) % bb == 0
    smem = pl.BlockSpec(memory_space=pltpu.SMEM)
    blk = pl.BlockSpec((bb, hd, W), lambda g: (g, 0, 0))
    const = lambda shape: pl.BlockSpec(shape, lambda g: (0,) * len(shape))
    weights = (w_in, w_o, w_up, w_down)
    return pl.pallas_call(
        functools.partial(_sample_kernel, L, layer),
        grid=(Bd // bb,),
        in_specs=[smem, smem, smem, _resident(x.shape), blk, blk] +
                 [_resident(w.shape) for w in weights] + [_layer_row(a, layer) for a in norms],
        out_specs=[const((Bd, L, D)), const((Bd, L, CHUNK_WIDTH)), blk, blk],
        out_shape=[jax.ShapeDtypeStruct((Bd, L, D), F32),
                   jax.ShapeDtypeStruct((Bd, L, CHUNK_WIDTH), F32),
                   jax.ShapeDtypeStruct(cache_kt.shape, F32),
                   jax.ShapeDtypeStruct(cache_vt.shape, F32)],
        scratch_shapes=[pltpu.VMEM((8, D), F32),
                        pltpu.VMEM((n, ATTN_WIDTH), F32),
                        pltpu.VMEM((2 * KV_WIDTH, n), F32),
                        pltpu.VMEM((n, CHUNK_WIDTH), F32),
                        pltpu.VMEM((n, ATTN_WIDTH), F32)],
        compiler_params=pltpu.CompilerParams(
            dimension_semantics=("arbitrary",), vmem_limit_bytes=VMEM_LIMIT),
        name="sample_layer",
    )(sinks, w4, b_sp, x, cache_kt, cache_vt, *weights, *norms)


def _heads_last(t):
    b, _, w = t.shape
    return t.reshape(b, N_KV_HEADS, HEAD_DIM, w).transpose(0, 3, 1, 2)


def kernel(x_prompt, x_sample, cache_win_k, cache_win_v, w_in, g_pre_mix, ln_v_g, ln_v_b,
           w_spatial, b_spatial, attn_sinks, g_out_chunk, g_out_attn, w_o, g_post_mix,
           g_pre_ffn, w_up, w_down, g_post_ffn):
    depth = w_in.shape[0]
    Bd, L, D = x_sample.shape
    W = cache_win_k.shape[2]
    norms = (g_pre_mix, g_post_mix, g_pre_ffn, g_post_ffn, ln_v_g, ln_v_b, g_out_chunk,
             g_out_attn)
    yp, ys = x_prompt, x_sample
    wk_p, wv_p, cv_p, wk_s, wv_s, cv_s = [], [], [], [], [], []
    for l in range(depth):
        w_in_b = w_in[l].astype(BF16)
        w_o_b = w_o[l].astype(BF16)
        w_up_b = w_up[l].astype(BF16)
        w_down_b = w_down[l].astype(BF16)

        yp, wk, wv, cv = _prompt_call(yp, l, attn_sinks, w_in_b, w_o_b, w_up_b, w_down_b, norms,
                                      w_spatial, b_spatial)
        wk_p.append(_heads_last(wk))
        wv_p.append(_heads_last(wv))
        cv_p.append(cv)

        w4 = w_spatial[l, :, :L, :L]
        cache_kt = cache_win_k[l].transpose(0, 2, 3, 1).reshape(Bd, KV_WIDTH, W)
        cache_vt = cache_win_v[l].transpose(0, 2, 3, 1).reshape(Bd, KV_WIDTH, W)
        ys, cv, wk, wv = _sample_call(ys, l, attn_sinks, w4, b_spatial, cache_kt, cache_vt,
                                      w_in_b, w_o_b, w_up_b, w_down_b, norms)
        wk_s.append(_heads_last(wk))
        wv_s.append(_heads_last(wv))
        cv_s.append(cv)
    return (yp, ys, jnp.stack(wk_p), jnp.stack(wv_p), jnp.stack(cv_p),
            jnp.stack(wk_s), jnp.stack(wv_s), jnp.stack(cv_s))
```

```python
import functools

import jax
import jax.numpy as jnp
from jax import lax
from jax.experimental import pallas as pl
from jax.experimental.pallas import tpu as pltpu

F32 = jnp.float32
BF16 = jnp.bfloat16

HEAD_DIM = 64
N_CHUNK_HEADS = 8
N_ATTN_HEADS = 8
N_KV_HEADS = 2
Q_PER_KV = N_ATTN_HEADS // N_KV_HEADS
CHUNK_WIDTH = N_CHUNK_HEADS * HEAD_DIM
ATTN_WIDTH = N_ATTN_HEADS * HEAD_DIM
KV_WIDTH = N_KV_HEADS * HEAD_DIM
CHUNK = 128
WINDOW = 128
EPS = 1e-6
ATTN_SCALE = HEAD_DIM ** -0.5

LANES = 128
HALF = LANES // 2
N_PAIRS = ATTN_WIDTH // LANES
VMEM_LIMIT = 56 * 1024 * 1024

PROMPT_TILE = 512
FF_CHUNK = 512
EARLY_CHUNKS = 1
SAMPLE_BB = 16

ROW_G_PRE, ROW_G_PM, ROW_G_PFF, ROW_G_PF, ROW_LN, ROW_GOUT = range(6)


def _dot(a, b):
    return jnp.dot(a, b, preferred_element_type=F32)


def _rms(x, g):
    return x * lax.rsqrt(jnp.mean(x * x, axis=-1, keepdims=True) + EPS) * g


def _layer_norm(x, g, b):
    xc = x - jnp.mean(x, axis=-1, keepdims=True)
    return xc * lax.rsqrt(jnp.mean(xc * xc, axis=-1, keepdims=True) + EPS) * g + b


def _alibi_slope(head):
    return 2.0 ** (-8.0 * (head + 1) / N_ATTN_HEADS)


def _merge(x, a_out, b_out, w_o_ref, vec_ref):
    d = x.shape[-1]
    cw = CHUNK_WIDTH
    ra = _rms(a_out, vec_ref[ROW_GOUT:ROW_GOUT + 1, 0:cw]).astype(BF16)
    rb = _rms(b_out, vec_ref[ROW_GOUT:ROW_GOUT + 1, cw:cw + ATTN_WIDTH]).astype(BF16)
    o = _dot(ra, w_o_ref[0:cw, :]) + _dot(rb, w_o_ref[cw:cw + ATTN_WIDTH, :])
    return x + _rms(o, vec_ref[ROW_G_PM:ROW_G_PM + 1, :d])


def _ffn(x1, w_up_ref, w_down_ref, vec_ref):
    d = x1.shape[-1]
    h2 = _rms(x1, vec_ref[ROW_G_PFF:ROW_G_PFF + 1, :d]).astype(BF16)
    d_ff = w_up_ref.shape[1]
    f = jnp.zeros(x1.shape, F32)
    for c in range(d_ff // FF_CHUNK):
        up = _dot(h2, w_up_ref[:, c * FF_CHUNK:(c + 1) * FF_CHUNK])
        r = jnp.maximum(up, 0.0)
        f = f + _dot((r * r).astype(BF16), w_down_ref[c * FF_CHUNK:(c + 1) * FF_CHUNK, :])
    return x1 + _rms(f, vec_ref[ROW_G_PF:ROW_G_PF + 1, :d])


def _fill_vector_table(vec_s, g_pre, g_pm, g_pff, g_pf, ln_g, ln_b, g_oc, g_oa):
    d = g_pre.shape[-1]
    cw = CHUNK_WIDTH
    vec_s[ROW_G_PRE:ROW_G_PRE + 1, 0:d] = g_pre[...]
    vec_s[ROW_G_PM:ROW_G_PM + 1, 0:d] = g_pm[...]
    vec_s[ROW_G_PFF:ROW_G_PFF + 1, 0:d] = g_pff[...]
    vec_s[ROW_G_PF:ROW_G_PF + 1, 0:d] = g_pf[...]
    vec_s[ROW_LN:ROW_LN + 1, 0:cw] = ln_g[...]
    vec_s[ROW_LN:ROW_LN + 1, cw:2 * cw] = ln_b[...]
    vec_s[ROW_GOUT:ROW_GOUT + 1, 0:cw] = g_oc[...]
    vec_s[ROW_GOUT:ROW_GOUT + 1, cw:cw + ATTN_WIDTH] = g_oa[...]


def _prompt_kernel(n_tiles, nj, layer, sink_ref, x_ref, w_in_ref, w_o_ref, w_up_ref, w_down_ref,
                   g_pre, g_pm, g_pff, g_pf, ln_g, ln_b, g_oc, g_oa, wsp_ref, bsp_ref,
                   y_ref, wk_ref, wv_ref, cv_ref,
                   vec_ref, bias_s, wcat_s, tab_s, kt_s, vd_s, a_s, bo_s, m_s, x1_s, x1n_s,
                   h_s, q_s, h2_s, act_s, f_s):
    step = pl.program_id(0)
    j = jnp.minimum(step, n_tiles - 1) % nj
    T, d = x_ref.shape[1], x_ref.shape[2]
    nb = T // WINDOW
    cw = CHUNK_WIDTH
    n_chunks = w_up_ref.shape[1] // FF_CHUNK

    lane = lax.broadcasted_iota(jnp.int32, (WINDOW, LANES), 1)
    lo = lane < HALF

    @pl.when(step == 0)
    def _init_tables():
        x1_s[...] = jnp.zeros(x1_s.shape, F32)
        h2_s[...] = jnp.zeros(h2_s.shape, BF16)
        f_s[...] = jnp.zeros(f_s.shape, F32)
        _fill_vector_table(vec_ref, g_pre, g_pm, g_pff, g_pf, ln_g, ln_b, g_oc, g_oa)
        t_idx = lax.broadcasted_iota(jnp.int32, (CHUNK, CHUNK), 0)
        s_idx = lax.broadcasted_iota(jnp.int32, (CHUNK, CHUNK), 1)
        for g in range(N_CHUNK_HEADS):
            wm = jnp.where(s_idx <= t_idx, wsp_ref[0, g], 0.0).astype(BF16)
            wcat_s[g // 2, :, (g % 2) * CHUNK:(g % 2 + 1) * CHUNK] = wm
        b_rows = jnp.concatenate(
            [bsp_ref[0], jnp.zeros((CHUNK - N_CHUNK_HEADS, CHUNK), F32)], axis=0)
        b_cols = b_rows.T
        for p in range(cw // LANES):
            bias_s[:, p * LANES:(p + 1) * LANES] = jnp.where(
                lo, jnp.broadcast_to(b_cols[:, 2 * p:2 * p + 1], (CHUNK, LANES)),
                jnp.broadcast_to(b_cols[:, 2 * p + 1:2 * p + 2], (CHUNK, LANES)))
        a_idx = lax.broadcasted_iota(jnp.int32, (WINDOW, 2 * WINDOW), 0)
        c_idx = lax.broadcasted_iota(jnp.int32, (WINDOW, 2 * WINDOW), 1)
        dist = WINDOW + a_idx - c_idx
        valid = (dist >= 0) & (dist <= WINDOW)
        for hd in range(N_ATTN_HEADS):
            bias = jnp.where(valid, -(_alibi_slope(hd) * dist.astype(F32)), -jnp.inf)
            r = hd % Q_PER_KV
            tab_s[hd // Q_PER_KV, r * WINDOW:(r + 1) * WINDOW, :] = bias

    @pl.when(j == 0)
    def _clear_carry():
        kt_s[:, 0:WINDOW] = jnp.zeros((2 * LANES, WINDOW), BF16)
        vd_s[:, 0:WINDOW, :] = jnp.zeros((N_KV_HEADS, WINDOW, LANES), BF16)

    def ffn_up(c):
        r = jnp.maximum(_dot(h2_s[...], w_up_ref[:, c * FF_CHUNK:(c + 1) * FF_CHUNK]), 0.0)
        act_s[...] = (r * r).astype(BF16)

    def ffn_down(c):
        part = _dot(act_s[...], w_down_ref[c * FF_CHUNK:(c + 1) * FF_CHUNK, :])
        if c == 0:
            f_s[...] = part
        else:
            f_s[...] += part

    late = iter([functools.partial(fn, c) for c in range(EARLY_CHUNKS, n_chunks)
                 for fn in (ffn_up, ffn_down)])

    def late_stage(count=1):
        for _ in range(count):
            next(late)()

    late_stage()
    h_s[...] = _rms(x_ref[0], vec_ref[ROW_G_PRE:ROW_G_PRE + 1, :d]).astype(BF16)
    u_pre = _dot(h_s[...], w_in_ref[:, 0:cw])
    v_pre = _dot(h_s[...], w_in_ref[:, cw:2 * cw])
    late_stage()
    u = jax.nn.gelu(u_pre)
    late_stage()
    v = _layer_norm(jax.nn.gelu(v_pre), vec_ref[ROW_LN:ROW_LN + 1, 0:cw],
                    vec_ref[ROW_LN:ROW_LN + 1, cw:2 * cw])
    kv0 = 2 * cw + ATTN_WIDTH
    kv = _dot(h_s[...], w_in_ref[:, kv0:kv0 + 2 * KV_WIDTH])
    k = kv[:, 0:KV_WIDTH]
    val = kv[:, KV_WIDTH:2 * KV_WIDTH]
    q0 = 2 * cw
    q_s[...] = _dot(h_s[...], w_in_ref[:, q0:q0 + ATTN_WIDTH]) * ATTN_SCALE
    late_stage()

    kt_f = k.T
    wk_ref[0] = kt_f[:, T - WINDOW:]
    wv_ref[0] = val[T - WINDOW:, :].T
    cv_ref[0] = v[T - CHUNK:, :]

    assert nb % 2 == 0
    for c in range(0, nb, 2):
        for p in range(cw // LANES):
            cols = slice(p * LANES, (p + 1) * LANES)
            halves = []
            for cc in (c, c + 1):
                vcp = v[cc * CHUNK:(cc + 1) * CHUNK, cols]
                halves.append(jnp.concatenate([jnp.where(lo, vcp, 0.0), jnp.where(lo, 0.0, vcp)],
                                              axis=0))
            rhs = jnp.concatenate(halves, axis=1).astype(BF16)
            s_pair = _dot(wcat_s[p], rhs)
            for k, cc in enumerate((c, c + 1)):
                rows = slice(cc * CHUNK, (cc + 1) * CHUNK)
                s_cp = s_pair[:, k * LANES:(k + 1) * LANES] + bias_s[:, cols]
                a_s[rows, cols] = u[rows, cols] * s_cp
    m_s[:, 0:cw] = _rms(a_s[...], vec_ref[ROW_GOUT:ROW_GOUT + 1, 0:cw]).astype(BF16)

    kt = kt_f.astype(BF16)
    for hk in range(N_KV_HEADS):
        kth = kt[hk * HEAD_DIM:(hk + 1) * HEAD_DIM, :]
        kt_s[hk * LANES:hk * LANES + HEAD_DIM, WINDOW:] = kth
        kt_s[hk * LANES + HEAD_DIM:(hk + 1) * LANES, WINDOW:] = kth
    lane_t = lax.broadcasted_iota(jnp.int32, (T, LANES), 1)
    lo_t = lane_t < HALF
    val_sw = pltpu.roll(val, HALF, axis=1)
    vd_s[0, WINDOW:, :] = jnp.where(lo_t, val, val_sw).astype(BF16)
    vd_s[1, WINDOW:, :] = jnp.where(lo_t, val_sw, val).astype(BF16)

    first_block_bias = jnp.where(j == 0, -jnp.inf, 0.0).astype(F32)
    head_of_row = lax.broadcasted_iota(jnp.int32, (Q_PER_KV * WINDOW, 1), 0) // WINDOW
    n_units = nb * N_KV_HEADS
    n_mid = 2 * (n_chunks - EARLY_CHUNKS) - 9
    late_after_scores = tuple(1 + (k * n_units) // n_mid for k in range(n_mid))
    for n, (i, hk) in enumerate((i, hk) for i in range(nb) for hk in range(N_KV_HEADS)):
        rows = slice(i * WINDOW, (i + 1) * WINDOW)
        keys = slice(i * WINDOW, (i + 2) * WINDOW)
        tiles = []
        for m in range(hk * 2, hk * 2 + 2):
            qb = q_s[rows, m * LANES:(m + 1) * LANES]
            tiles += [jnp.where(lo, qb, 0.0), jnp.where(lo, 0.0, qb)]
        qq = jnp.concatenate(tiles, axis=0).astype(BF16)
        s = _dot(qq, kt_s[hk * LANES:(hk + 1) * LANES, keys]) + tab_s[hk]
        if i == 0:
            s = jnp.concatenate([s[:, 0:WINDOW] + first_block_bias, s[:, WINDOW:]], axis=1)
        if n in late_after_scores:
            late_stage()
        sink = jnp.zeros((Q_PER_KV * WINDOW, 1), F32)
        for r in range(Q_PER_KV):
            sink = jnp.where(head_of_row == r, sink_ref[layer, hk, r], sink)
        mx = jnp.maximum(jnp.max(s, axis=-1, keepdims=True), sink)
        p_un = jnp.exp(s - mx)
        z = jnp.sum(p_un, axis=-1, keepdims=True) + jnp.exp(sink - mx)
        o = _dot(p_un.astype(BF16), vd_s[hk, keys, :]) * (1.0 / z)
        for pair in range(Q_PER_KV // 2):
            m = hk * 2 + pair
            o_even = o[(2 * pair) * WINDOW:(2 * pair + 1) * WINDOW]
            o_odd = o[(2 * pair + 1) * WINDOW:(2 * pair + 2) * WINDOW]
            bo_s[rows, m * LANES:(m + 1) * LANES] = jnp.where(lo, o_even, o_odd)

    kt_s[:, 0:WINDOW] = kt_s[:, T:T + WINDOW]
    vd_s[:, 0:WINDOW, :] = vd_s[:, T:T + WINDOW, :]

    late_stage()
    m_s[:, cw:cw + ATTN_WIDTH] = _rms(
        bo_s[...], vec_ref[ROW_GOUT:ROW_GOUT + 1, cw:cw + ATTN_WIDTH]).astype(BF16)
    o_proj = _dot(m_s[...], w_o_ref[...])
    late_stage(3)
    x1 = x_ref[0] + _rms(o_proj, vec_ref[ROW_G_PM:ROW_G_PM + 1, :d])
    x1n_s[...] = x1
    h2_s[...] = _rms(x1, vec_ref[ROW_G_PFF:ROW_G_PFF + 1, :d]).astype(BF16)
    late_stage()

    ffn_up(0)
    y_ref[0] = x1_s[...] + _rms(f_s[...], vec_ref[ROW_G_PF:ROW_G_PF + 1, :d])
    x1_s[...] = x1n_s[...]
    ffn_down(0)
    for c in range(1, EARLY_CHUNKS):
        ffn_up(c)
        ffn_down(c)


def _resident(shape):
    return pl.BlockSpec(shape, lambda *_: (0,) * len(shape), pipeline_mode=pl.Buffered(1))


def _layer_row(arr, layer):
    block = (1,) + arr.shape[1:]
    return pl.BlockSpec(block, lambda *_: (layer,) + (0,) * (arr.ndim - 1),
                        pipeline_mode=pl.Buffered(1))


def _prompt_call(x, layer, sinks, w_in, w_o, w_up, w_down, norms, w_sp, b_sp):
    B, S, D = x.shape
    T = PROMPT_TILE
    assert S % T == 0 and T % WINDOW == 0 and WINDOW == CHUNK
    assert 1 <= EARLY_CHUNKS and 2 * (w_up.shape[1] // FF_CHUNK - EARLY_CHUNKS) >= 9
    nj = S // T
    n_tiles = B * nj
    cur = lambda s: jnp.minimum(s, n_tiles - 1)
    prev = lambda s: jnp.maximum(s - 1, 0)
    tile_in = pl.BlockSpec((1, T, D), lambda s: (cur(s) // nj, cur(s) % nj, 0))
    tile_out = pl.BlockSpec((1, T, D), lambda s: (prev(s) // nj, prev(s) % nj, 0))
    last = lambda r, c: pl.BlockSpec((1, r, c), lambda s: (cur(s) // nj, 0, 0))
    return pl.pallas_call(
        functools.partial(_prompt_kernel, n_tiles, nj, layer),
        grid=(n_tiles + 1,),
        in_specs=[pl.BlockSpec(memory_space=pltpu.SMEM), tile_in,
                  _resident(w_in.shape), _resident(w_o.shape), _resident(w_up.shape),
                  _resident(w_down.shape)] + [_layer_row(a, layer) for a in norms] +
                 [_layer_row(w_sp, layer), _layer_row(b_sp, layer)],
        out_specs=[tile_out, last(KV_WIDTH, WINDOW), last(KV_WIDTH, WINDOW),
                   last(CHUNK, CHUNK_WIDTH)],
        out_shape=[jax.ShapeDtypeStruct((B, S, D), F32),
                   jax.ShapeDtypeStruct((B, KV_WIDTH, WINDOW), F32),
                   jax.ShapeDtypeStruct((B, KV_WIDTH, WINDOW), F32),
                   jax.ShapeDtypeStruct((B, CHUNK, CHUNK_WIDTH), F32)],
        scratch_shapes=[pltpu.VMEM((8, D), F32),
                        pltpu.VMEM((CHUNK, CHUNK_WIDTH), F32),
                        pltpu.VMEM((N_CHUNK_HEADS // 2, CHUNK, 2 * CHUNK), BF16),
                        pltpu.VMEM((N_KV_HEADS, Q_PER_KV * WINDOW, 2 * WINDOW), F32),
                        pltpu.VMEM((2 * LANES, T + WINDOW), BF16),
                        pltpu.VMEM((N_KV_HEADS, T + WINDOW, LANES), BF16),
                        pltpu.VMEM((T, CHUNK_WIDTH), F32),
                        pltpu.VMEM((T, ATTN_WIDTH), F32),
                        pltpu.VMEM((T, CHUNK_WIDTH + ATTN_WIDTH), BF16),
                        pltpu.VMEM((T, D), F32),
                        pltpu.VMEM((T, D), F32),
                        pltpu.VMEM((T, D), BF16),
                        pltpu.VMEM((T, ATTN_WIDTH), F32),
                        pltpu.VMEM((T, D), BF16),
                        pltpu.VMEM((T, FF_CHUNK), BF16),
                        pltpu.VMEM((T, D), F32)],
        compiler_params=pltpu.CompilerParams(
            dimension_semantics=("arbitrary",), vmem_limit_bytes=VMEM_LIMIT),
        name="prompt_layer",
    )(sinks, x, w_in, w_o, w_up, w_down, *norms, w_sp, b_sp)


def _sample_kernel(L, layer, sink_ref, w4_ref, b4_ref, x_ref, ckt_ref, cvt_ref, w_in_ref,
                   w_o_ref, w_up_ref, w_down_ref,
                   g_pre, g_pm, g_pff, g_pf, ln_g, ln_b, g_oc, g_oa,
                   y_ref, vn_ref, wkt_ref, wvt_ref,
                   vec_ref, q_s, kvt_s, a_s, b_s):
    g = pl.program_id(0)
    n_steps = pl.num_programs(0)
    bb, hd, W = ckt_ref.shape
    n, d = q_s.shape[0], x_ref.shape[-1]
    n_heads = N_ATTN_HEADS
    pair_rows = 2 * L
    assert pair_rows == 8 and hd == LANES and W == LANES
    batch_per_block = LANES // L

    @pl.when(g == 0)
    def _project_and_gate():
        cw = CHUNK_WIDTH
        _fill_vector_table(vec_ref, g_pre, g_pm, g_pff, g_pf, ln_g, ln_b, g_oc, g_oa)
        h = _rms(x_ref[...].reshape(n, d), vec_ref[ROW_G_PRE:ROW_G_PRE + 1, :d]).astype(BF16)
        u = jax.nn.gelu(_dot(h, w_in_ref[:, 0:cw]))
        v = _layer_norm(jax.nn.gelu(_dot(h, w_in_ref[:, cw:2 * cw])),
                        vec_ref[ROW_LN:ROW_LN + 1, 0:cw], vec_ref[ROW_LN:ROW_LN + 1, cw:2 * cw])
        vn_ref[...] = v.reshape(vn_ref.shape)
        q_s[...] = _dot(h, w_in_ref[:, 2 * cw:2 * cw + ATTN_WIDTH]) * ATTN_SCALE
        kv0 = 2 * cw + ATTN_WIDTH
        kvt_s[...] = _dot(h, w_in_ref[:, kv0:kv0 + 2 * KV_WIDTH]).T
        t_of_row = lax.broadcasted_iota(jnp.int32, (n, 1), 0) % L
        group_of_lane = lax.broadcasted_iota(jnp.int32, (1, cw), 1) // HEAD_DIM

        def per_group(value_of_group):
            row = jnp.zeros((1, cw), F32)
            for grp in range(N_CHUNK_HEADS):
                row = jnp.where(group_of_lane == grp, value_of_group(grp), row)
            return row

        s = jnp.zeros(v.shape, F32)
        for t in range(L):
            s = jnp.where(t_of_row == t, per_group(lambda grp: b4_ref[layer, grp, t]), s)
        for delta in range(L):
            coef = jnp.zeros(v.shape, F32)
            for t in range(delta, L):
                w_row = per_group(lambda grp: w4_ref[grp, t, t - delta])
                coef = jnp.where(t_of_row == t, w_row, coef)
            vs = v if delta == 0 else pltpu.roll(v, delta, axis=0)
            s = s + coef * vs
        a_s[...] = u * s

    rows_all = n_heads * pair_rows
    r_idx = lax.broadcasted_iota(jnp.int32, (rows_all, LANES), 0)
    l_idx = lax.broadcasted_iota(jnp.int32, (rows_all, LANES), 1)
    t_idx = r_idx % L
    odd = (r_idx % pair_rows) >= L
    head_of_row = lax.broadcasted_iota(jnp.int32, (rows_all, 1), 0) // pair_rows
    sink = jnp.zeros((rows_all, 1), F32)
    slope = jnp.zeros((rows_all, 1), F32)
    for hh in range(n_heads):
        sink = jnp.where(head_of_row == hh, sink_ref[layer, hh // Q_PER_KV, hh % Q_PER_KV], sink)
        slope = jnp.where(head_of_row == hh, _alibi_slope(hh), slope)
    dist_old = W + t_idx - l_idx
    bias_old = jnp.where(dist_old <= WINDOW, -(slope * dist_old.astype(F32)), -jnp.inf)
    lane8 = lax.broadcasted_iota(jnp.int32, (pair_rows, LANES), 1)
    lo8 = lane8 < HALF
    lane_w = lax.broadcasted_iota(jnp.int32, (hd, W), 1)

    first_batch = g * bb
    new_block = pl.multiple_of((first_batch // batch_per_block) * LANES, LANES)
    kt_new = kvt_s[0:KV_WIDTH, pl.ds(new_block, LANES)]
    vt_new = kvt_s[KV_WIDTH:2 * KV_WIDTH, pl.ds(new_block, LANES)]
    kt_new_b = kt_new.astype(BF16)
    vt_new_b = vt_new.astype(BF16)
    for p in range(bb // 2):
        off = ((first_batch + 2 * p) % batch_per_block) * L
        row0 = pl.multiple_of((first_batch + 2 * p) * L, pair_rows)
        tiles = []
        for m in range(N_PAIRS):
            qb = q_s[pl.ds(row0, pair_rows), m * LANES:(m + 1) * LANES]
            qb_sw = pltpu.roll(qb, HALF, axis=1)
            if (2 * m) // Q_PER_KV == 0:
                tiles += [jnp.where(lo8, qb, 0.0), jnp.where(lo8, qb_sw, 0.0)]
            else:
                tiles += [jnp.where(lo8, 0.0, qb_sw), jnp.where(lo8, 0.0, qb)]
        lhs = jnp.concatenate(tiles, axis=0).astype(BF16)
        s_pair = []
        for e in range(2):
            rhs = jnp.concatenate([ckt_ref[2 * p + e].astype(BF16), kt_new_b], axis=1)
            s_pair.append(_dot(lhs, rhs))
        sc = jnp.where(jnp.concatenate([odd, odd], axis=1), s_pair[1], s_pair[0])
        rel = l_idx - off - jnp.where(odd, L, 0)
        bias_new = jnp.where((rel >= 0) & (rel <= t_idx),
                             -(slope * (t_idx - rel).astype(F32)), -jnp.inf)
        sc = sc + jnp.concatenate([bias_old, bias_new], axis=1)
        mx = jnp.maximum(jnp.max(sc, axis=-1, keepdims=True), sink)
        p_un = jnp.exp(sc - mx)
        z = jnp.sum(p_un, axis=-1, keepdims=True) + jnp.exp(sink - mx)
        pb = p_un.astype(BF16)
        o_pair = []
        for e in range(2):
            vt = jnp.concatenate([cvt_ref[2 * p + e].astype(BF16), vt_new_b], axis=1)
            o_pair.append(lax.dot_general(pb, vt, (((1,), (1,)), ((), ())),
                                          preferred_element_type=F32))
        o = jnp.where(odd, o_pair[1], o_pair[0]) * (1.0 / z)
        for m in range(N_PAIRS):
            kv_head = (2 * m) // Q_PER_KV
            o_even = o[(2 * m) * pair_rows:(2 * m + 1) * pair_rows]
            o_odd = o[(2 * m + 1) * pair_rows:(2 * m + 2) * pair_rows]
            if kv_head == 0:
                o_odd = pltpu.roll(o_odd, HALF, axis=1)
            else:
                o_even = pltpu.roll(o_even, HALF, axis=1)
            b_s[pl.ds(row0, pair_rows), m * LANES:(m + 1) * LANES] = jnp.where(lo8, o_even, o_odd)
        for e in range(2):
            shift_new = (W - L) - (off + e * L)
            keep = lane_w < W - L
            wkt_ref[2 * p + e] = jnp.where(keep, pltpu.roll(ckt_ref[2 * p + e], W - L, axis=1),
                                           pltpu.roll(kt_new, shift_new, axis=1))
            wvt_ref[2 * p + e] = jnp.where(keep, pltpu.roll(cvt_ref[2 * p + e], W - L, axis=1),
                                           pltpu.roll(vt_new, shift_new, axis=1))

    @pl.when(g == n_steps - 1)
    def _merge_and_ffn():
        x1 = _merge(x_ref[...].reshape(n, d), a_s[...], b_s[...], w_o_ref, vec_ref)
        y_ref[...] = _ffn(x1, w_up_ref, w_down_ref, vec_ref).reshape(y_ref.shape)


def _sample_call(x, layer, sinks, w4, b_sp, cache_kt, cache_vt, w_in, w_o, w_up, w_down,
                 norms):
    Bd, L, D = x.shape
    hd, W = cache_kt.shape[1:]
    n = Bd * L
    bb = SAMPLE_BB
    assert Bd % bb == 0 and bb % 2 == 0 and (LANES // L) % bb == 0
    smem = pl.BlockSpec(memory_space=pltpu.SMEM)
    blk = pl.BlockSpec((bb, hd, W), lambda g: (g, 0, 0))
    const = lambda shape: pl.BlockSpec(shape, lambda g: (0,) * len(shape))
    weights = (w_in, w_o, w_up, w_down)
    return pl.pallas_call(
        functools.partial(_sample_kernel, L, layer),
        grid=(Bd // bb,),
        in_specs=[smem, smem, smem, _resident(x.shape), blk, blk] +
                 [_resident(w.shape) for w in weights] + [_layer_row(a, layer) for a in norms],
        out_specs=[const((Bd, L, D)), const((Bd, L, CHUNK_WIDTH)), blk, blk],
        out_shape=[jax.ShapeDtypeStruct((Bd, L, D), F32),
                   jax.ShapeDtypeStruct((Bd, L, CHUNK_WIDTH), F32),
                   jax.ShapeDtypeStruct(cache_kt.shape, F32),
                   jax.ShapeDtypeStruct(cache_vt.shape, F32)],
        scratch_shapes=[pltpu.VMEM((8, D), F32),
                        pltpu.VMEM((n, ATTN_WIDTH), F32),
                        pltpu.VMEM((2 * KV_WIDTH, n), F32),
                        pltpu.VMEM((n, CHUNK_WIDTH), F32),
                        pltpu.VMEM((n, ATTN_WIDTH), F32)],
        compiler_params=pltpu.CompilerParams(
            dimension_semantics=("arbitrary",), vmem_limit_bytes=VMEM_LIMIT),
        name="sample_layer",
    )(sinks, w4, b_sp, x, cache_kt, cache_vt, *weights, *norms)


def _heads_last(t):
    b, _, w = t.shape
    return t.reshape(b, N_KV_HEADS, HEAD_DIM, w).transpose(0, 3, 1, 2)


def kernel(x_prompt, x_sample, cache_win_k, cache_win_v, w_in, g_pre_mix, ln_v_g, ln_v_b,
           w_spatial, b_spatial, attn_sinks, g_out_chunk, g_out_attn, w_o, g_post_mix,
           g_pre_ffn, w_up, w_down, g_post_ffn):
    depth = w_in.shape[0]
    Bd, L, D = x_sample.shape
    W = cache_win_k.shape[2]
    norms = (g_pre_mix, g_post_mix, g_pre_ffn, g_post_ffn, ln_v_g, ln_v_b, g_out_chunk,
             g_out_attn)
    yp, ys = x_prompt, x_sample
    wk_p, wv_p, cv_p, wk_s, wv_s, cv_s = [], [], [], [], [], []
    for l in range(depth):
        w_in_b = w_in[l].astype(BF16)
        w_o_b = w_o[l].astype(BF16)
        w_up_b = w_up[l].astype(BF16)
        w_down_b = w_down[l].astype(BF16)

        yp, wk, wv, cv = _prompt_call(yp, l, attn_sinks, w_in_b, w_o_b, w_up_b, w_down_b, norms,
                                      w_spatial, b_spatial)
        wk_p.append(_heads_last(wk))
        wv_p.append(_heads_last(wv))
        cv_p.append(cv)

        w4 = w_spatial[l, :, :L, :L]
        cache_kt = cache_win_k[l].transpose(0, 2, 3, 1).reshape(Bd, KV_WIDTH, W)
        cache_vt = cache_win_v[l].transpose(0, 2, 3, 1).reshape(Bd, KV_WIDTH, W)
        ys, cv, wk, wv = _sample_call(ys, l, attn_sinks, w4, b_spatial, cache_kt, cache_vt,
                                      w_in_b, w_o_b, w_up_b, w_down_b, norms)
        wk_s.append(_heads_last(wk))
        wv_s.append(_heads_last(wv))
        cv_s.append(cv)
    return (yp, ys, jnp.stack(wk_p), jnp.stack(wv_p), jnp.stack(cv_p),
            jnp.stack(wk_s), jnp.stack(wv_s), jnp.stack(cv_s))
```

```python
import functools

import jax
import jax.numpy as jnp
from jax import lax
from jax.experimental import pallas as pl
from jax.experimental.pallas import tpu as pltpu

F32 = jnp.float32
BF16 = jnp.bfloat16

HEAD_DIM = 64
N_CHUNK_HEADS = 8
N_ATTN_HEADS = 8
N_KV_HEADS = 2
Q_PER_KV = N_ATTN_HEADS // N_KV_HEADS
CHUNK_WIDTH = N_CHUNK_HEADS * HEAD_DIM
ATTN_WIDTH = N_ATTN_HEADS * HEAD_DIM
KV_WIDTH = N_KV_HEADS * HEAD_DIM
CHUNK = 128
WINDOW = 128
EPS = 1e-6
ATTN_SCALE = HEAD_DIM ** -0.5

LANES = 128
HALF = LANES // 2
N_PAIRS = ATTN_WIDTH // LANES
VMEM_LIMIT = 56 * 1024 * 1024

PROMPT_TILE = 512
FF_CHUNK = 512
EARLY_CHUNKS = 1
SAMPLE_BB = 16

ROW_G_PRE, ROW_G_PM, ROW_G_PFF, ROW_G_PF, ROW_LN, ROW_GOUT = range(6)


def _dot(a, b):
    return jnp.dot(a, b, preferred_element_type=F32)


def _rms(x, g):
    return x * lax.rsqrt(jnp.mean(x * x, axis=-1, keepdims=True) + EPS) * g


def _layer_norm(x, g, b):
    xc = x - jnp.mean(x, axis=-1, keepdims=True)
    return xc * lax.rsqrt(jnp.mean(xc * xc, axis=-1, keepdims=True) + EPS) * g + b


def _alibi_slope(head):
    return 2.0 ** (-8.0 * (head + 1) / N_ATTN_HEADS)


def _merge(x, a_out, b_out, w_o_ref, vec_ref):
    d = x.shape[-1]
    cw = CHUNK_WIDTH
    ra = _rms(a_out, vec_ref[ROW_GOUT:ROW_GOUT + 1, 0:cw]).astype(BF16)
    rb = _rms(b_out, vec_ref[ROW_GOUT:ROW_GOUT + 1, cw:cw + ATTN_WIDTH]).astype(BF16)
    o = _dot(ra, w_o_ref[0:cw, :]) + _dot(rb, w_o_ref[cw:cw + ATTN_WIDTH, :])
    return x + _rms(o, vec_ref[ROW_G_PM:ROW_G_PM + 1, :d])


def _relu_sq(up):
    r = jnp.maximum(up.astype(BF16), 0.0)
    return r * r


def _ffn(x1, w_up_ref, w_down_ref, vec_ref):
    d = x1.shape[-1]
    h2 = _rms(x1, vec_ref[ROW_G_PFF:ROW_G_PFF + 1, :d]).astype(BF16)
    d_ff = w_up_ref.shape[1]
    f = jnp.zeros(x1.shape, F32)
    for c in range(d_ff // FF_CHUNK):
        up = _dot(h2, w_up_ref[:, c * FF_CHUNK:(c + 1) * FF_CHUNK])
        f = f + _dot(_relu_sq(up), w_down_ref[c * FF_CHUNK:(c + 1) * FF_CHUNK, :])
    return x1 + _rms(f, vec_ref[ROW_G_PF:ROW_G_PF + 1, :d])


def _fill_vector_table(vec_s, g_pre, g_pm, g_pff, g_pf, ln_g, ln_b, g_oc, g_oa):
    d = g_pre.shape[-1]
    cw = CHUNK_WIDTH
    vec_s[ROW_G_PRE:ROW_G_PRE + 1, 0:d] = g_pre[...]
    vec_s[ROW_G_PM:ROW_G_PM + 1, 0:d] = g_pm[...]
    vec_s[ROW_G_PFF:ROW_G_PFF + 1, 0:d] = g_pff[...]
    vec_s[ROW_G_PF:ROW_G_PF + 1, 0:d] = g_pf[...]
    vec_s[ROW_LN:ROW_LN + 1, 0:cw] = ln_g[...]
    vec_s[ROW_LN:ROW_LN + 1, cw:2 * cw] = ln_b[...]
    vec_s[ROW_GOUT:ROW_GOUT + 1, 0:cw] = g_oc[...]
    vec_s[ROW_GOUT:ROW_GOUT + 1, cw:cw + ATTN_WIDTH] = g_oa[...]


def _prompt_kernel(n_tiles, nj, layer, sink_ref, x_ref, w_in_ref, w_o_ref, w_up_ref, w_down_ref,
                   g_pre, g_pm, g_pff, g_pf, ln_g, ln_b, g_oc, g_oa, wsp_ref, bsp_ref,
                   y_ref, wk_ref, wv_ref, cv_ref,
                   vec_ref, bias_s, wcat_s, tab_s, kt_s, vd_s, a_s, bo_s, m_s, x1_s, x1n_s,
                   h_s, q_s, h2_s, act_s, f_s):
    step = pl.program_id(0)
    j = jnp.minimum(step, n_tiles - 1) % nj
    T, d = x_ref.shape[1], x_ref.shape[2]
    nb = T // WINDOW
    cw = CHUNK_WIDTH
    n_chunks = w_up_ref.shape[1] // FF_CHUNK

    lane = lax.broadcasted_iota(jnp.int32, (WINDOW, LANES), 1)
    lo = lane < HALF

    @pl.when(step == 0)
    def _init_tables():
        _fill_vector_table(vec_ref, g_pre, g_pm, g_pff, g_pf, ln_g, ln_b, g_oc, g_oa)
        t_idx = lax.broadcasted_iota(jnp.int32, (CHUNK, CHUNK), 0)
        s_idx = lax.broadcasted_iota(jnp.int32, (CHUNK, CHUNK), 1)
        for g in range(N_CHUNK_HEADS):
            wm = jnp.where(s_idx <= t_idx, wsp_ref[0, g], 0.0).astype(BF16)
            wcat_s[g // 2, :, (g % 2) * CHUNK:(g % 2 + 1) * CHUNK] = wm
        b_rows = jnp.concatenate(
            [bsp_ref[0], jnp.zeros((CHUNK - N_CHUNK_HEADS, CHUNK), F32)], axis=0)
        b_cols = b_rows.T
        for p in range(cw // LANES):
            bias_s[:, p * LANES:(p + 1) * LANES] = jnp.where(
                lo, jnp.broadcast_to(b_cols[:, 2 * p:2 * p + 1], (CHUNK, LANES)),
                jnp.broadcast_to(b_cols[:, 2 * p + 1:2 * p + 2], (CHUNK, LANES)))
        a_idx = lax.broadcasted_iota(jnp.int32, (WINDOW, 2 * WINDOW), 0)
        c_idx = lax.broadcasted_iota(jnp.int32, (WINDOW, 2 * WINDOW), 1)
        dist = WINDOW + a_idx - c_idx
        valid = (dist >= 0) & (dist <= WINDOW)
        for hd in range(N_ATTN_HEADS):
            bias = jnp.where(valid, -(_alibi_slope(hd) * dist.astype(F32)), -jnp.inf)
            r = hd % Q_PER_KV
            tab_s[hd // Q_PER_KV, r * WINDOW:(r + 1) * WINDOW, :] = bias

    @pl.when(j == 0)
    def _clear_carry():
        kt_s[:, 0:WINDOW] = jnp.zeros((2 * LANES, WINDOW), BF16)
        vd_s[:, 0:WINDOW, :] = jnp.zeros((N_KV_HEADS, WINDOW, LANES), BF16)

    def ffn_up(c):
        act_s[...] = _relu_sq(_dot(h2_s[...], w_up_ref[:, c * FF_CHUNK:(c + 1) * FF_CHUNK]))

    def ffn_down(c):
        part = _dot(act_s[...], w_down_ref[c * FF_CHUNK:(c + 1) * FF_CHUNK, :])
        if c == 0:
            f_s[...] = part
        else:
            f_s[...] += part

    def finish_previous():
        y_ref[0] = x1_s[...] + _rms(f_s[...], vec_ref[ROW_G_PF:ROW_G_PF + 1, :d])

    def tile_step(with_late):
        late = iter([functools.partial(fn, c) for c in range(EARLY_CHUNKS, n_chunks)
                     for fn in (ffn_up, ffn_down)])

        def late_stage(count=1):
            if with_late:
                for _ in range(count):
                    next(late)()

        late_stage()
        h_s[...] = _rms(x_ref[0], vec_ref[ROW_G_PRE:ROW_G_PRE + 1, :d]).astype(BF16)
        u_pre = _dot(h_s[...], w_in_ref[:, 0:cw])
        v_pre = _dot(h_s[...], w_in_ref[:, cw:2 * cw])
        late_stage()
        u = jax.nn.gelu(u_pre)
        late_stage()
        v = _layer_norm(jax.nn.gelu(v_pre), vec_ref[ROW_LN:ROW_LN + 1, 0:cw],
                        vec_ref[ROW_LN:ROW_LN + 1, cw:2 * cw])
        kv0 = 2 * cw + ATTN_WIDTH
        kv = _dot(h_s[...], w_in_ref[:, kv0:kv0 + 2 * KV_WIDTH])
        k = kv[:, 0:KV_WIDTH]
        val = kv[:, KV_WIDTH:2 * KV_WIDTH]
        q0 = 2 * cw
        q_s[...] = _dot(h_s[...], w_in_ref[:, q0:q0 + ATTN_WIDTH]) * ATTN_SCALE
        late_stage()

        kt_f = k.T
        wk_ref[0] = kt_f[:, T - WINDOW:]
        wv_ref[0] = val[T - WINDOW:, :].T
        cv_ref[0] = v[T - CHUNK:, :]

        assert nb % 2 == 0
        for c in range(0, nb, 2):
            for p in range(cw // LANES):
                cols = slice(p * LANES, (p + 1) * LANES)
                halves = []
                for cc in (c, c + 1):
                    vcp = v[cc * CHUNK:(cc + 1) * CHUNK, cols]
                    halves.append(jnp.concatenate([jnp.where(lo, vcp, 0.0), jnp.where(lo, 0.0, vcp)],
                                                  axis=0))
                rhs = jnp.concatenate(halves, axis=1).astype(BF16)
                s_pair = _dot(wcat_s[p], rhs)
                for side, cc in enumerate((c, c + 1)):
                    rows = slice(cc * CHUNK, (cc + 1) * CHUNK)
                    s_cp = s_pair[:, side * LANES:(side + 1) * LANES] + bias_s[:, cols]
                    a_s[rows, cols] = u[rows, cols] * s_cp
        m_s[:, 0:cw] = _rms(a_s[...], vec_ref[ROW_GOUT:ROW_GOUT + 1, 0:cw]).astype(BF16)

        kt = kt_f.astype(BF16)
        for hk in range(N_KV_HEADS):
            kth = kt[hk * HEAD_DIM:(hk + 1) * HEAD_DIM, :]
            kt_s[hk * LANES:hk * LANES + HEAD_DIM, WINDOW:] = kth
            kt_s[hk * LANES + HEAD_DIM:(hk + 1) * LANES, WINDOW:] = kth
        lane_t = lax.broadcasted_iota(jnp.int32, (T, LANES), 1)
        lo_t = lane_t < HALF
        val_sw = pltpu.roll(val, HALF, axis=1)
        vd_s[0, WINDOW:, :] = jnp.where(lo_t, val, val_sw).astype(BF16)
        vd_s[1, WINDOW:, :] = jnp.where(lo_t, val_sw, val).astype(BF16)

        first_block_bias = jnp.where(j == 0, -jnp.inf, 0.0).astype(F32)
        head_of_row = lax.broadcasted_iota(jnp.int32, (Q_PER_KV * WINDOW, 1), 0) // WINDOW
        n_units = nb * N_KV_HEADS
        n_mid = 2 * (n_chunks - EARLY_CHUNKS) - 9
        late_after_scores = tuple(1 + (k * n_units) // n_mid for k in range(n_mid))
        for n, (i, hk) in enumerate((i, hk) for i in range(nb) for hk in range(N_KV_HEADS)):
            rows = slice(i * WINDOW, (i + 1) * WINDOW)
            keys = slice(i * WINDOW, (i + 2) * WINDOW)
            tiles = []
            for m in range(hk * 2, hk * 2 + 2):
                qb = q_s[rows, m * LANES:(m + 1) * LANES]
                tiles += [jnp.where(lo, qb, 0.0), jnp.where(lo, 0.0, qb)]
            qq = jnp.concatenate(tiles, axis=0).astype(BF16)
            s = _dot(qq, kt_s[hk * LANES:(hk + 1) * LANES, keys]) + tab_s[hk]
            if i == 0:
                s = jnp.concatenate([s[:, 0:WINDOW] + first_block_bias, s[:, WINDOW:]], axis=1)
            if n in late_after_scores:
                late_stage()
            sink = jnp.zeros((Q_PER_KV * WINDOW, 1), F32)
            for r in range(Q_PER_KV):
                sink = jnp.where(head_of_row == r, sink_ref[layer, hk, r], sink)
            mx = jnp.maximum(jnp.max(s, axis=-1, keepdims=True), sink)
            p_un = jnp.exp(s - mx)
            z = jnp.sum(p_un, axis=-1, keepdims=True) + jnp.exp(sink - mx)
            o = _dot(p_un.astype(BF16), vd_s[hk, keys, :]) * (1.0 / z)
            for pair in range(Q_PER_KV // 2):
                m = hk * 2 + pair
                o_even = o[(2 * pair) * WINDOW:(2 * pair + 1) * WINDOW]
                o_odd = o[(2 * pair + 1) * WINDOW:(2 * pair + 2) * WINDOW]
                bo_s[rows, m * LANES:(m + 1) * LANES] = jnp.where(lo, o_even, o_odd)

        kt_s[:, 0:WINDOW] = kt_s[:, T:T + WINDOW]
        vd_s[:, 0:WINDOW, :] = vd_s[:, T:T + WINDOW, :]

        late_stage()
        m_s[:, cw:cw + ATTN_WIDTH] = _rms(
            bo_s[...], vec_ref[ROW_GOUT:ROW_GOUT + 1, cw:cw + ATTN_WIDTH]).astype(BF16)
        o_proj = _dot(m_s[...], w_o_ref[...])
        late_stage(3)
        x1 = x_ref[0] + _rms(o_proj, vec_ref[ROW_G_PM:ROW_G_PM + 1, :d])
        x1n_s[...] = x1
        h2_s[...] = _rms(x1, vec_ref[ROW_G_PFF:ROW_G_PFF + 1, :d]).astype(BF16)
        late_stage()

        ffn_up(0)
        if with_late:
            finish_previous()
        x1_s[...] = x1n_s[...]
        ffn_down(0)
        for c in range(1, EARLY_CHUNKS):
            ffn_up(c)
            ffn_down(c)

    @pl.when(step == 0)
    def _first_step():
        tile_step(False)

    @pl.when((step > 0) & (step < n_tiles))
    def _steady_step():
        tile_step(True)

    @pl.when(step == n_tiles)
    def _drain_step():
        for c in range(EARLY_CHUNKS, n_chunks):
            ffn_up(c)
            ffn_down(c)
        finish_previous()


def _resident(shape):
    return pl.BlockSpec(shape, lambda *_: (0,) * len(shape), pipeline_mode=pl.Buffered(1))


def _layer_row(arr, layer):
    block = (1,) + arr.shape[1:]
    return pl.BlockSpec(block, lambda *_: (layer,) + (0,) * (arr.ndim - 1),
                        pipeline_mode=pl.Buffered(1))


def _prompt_call(x, layer, sinks, w_in, w_o, w_up, w_down, norms, w_sp, b_sp):
    B, S, D = x.shape
    T = PROMPT_TILE
    assert S % T == 0 and T % WINDOW == 0 and WINDOW == CHUNK
    assert 1 <= EARLY_CHUNKS and 2 * (w_up.shape[1] // FF_CHUNK - EARLY_CHUNKS) >= 9
    nj = S // T
    n_tiles = B * nj
    cur = lambda s: jnp.minimum(s, n_tiles - 1)
    prev = lambda s: jnp.maximum(s - 1, 0)
    tile_in = pl.BlockSpec((1, T, D), lambda s: (cur(s) // nj, cur(s) % nj, 0))
    tile_out = pl.BlockSpec((1, T, D), lambda s: (prev(s) // nj, prev(s) % nj, 0))
    last = lambda r, c: pl.BlockSpec((1, r, c), lambda s: (cur(s) // nj, 0, 0))
    return pl.pallas_call(
        functools.partial(_prompt_kernel, n_tiles, nj, layer),
        grid=(n_tiles + 1,),
        in_specs=[pl.BlockSpec(memory_space=pltpu.SMEM), tile_in,
                  _resident(w_in.shape), _resident(w_o.shape), _resident(w_up.shape),
                  _resident(w_down.shape)] + [_layer_row(a, layer) for a in norms] +
                 [_layer_row(w_sp, layer), _layer_row(b_sp, layer)],
        out_specs=[tile_out, last(KV_WIDTH, WINDOW), last(KV_WIDTH, WINDOW),
                   last(CHUNK, CHUNK_WIDTH)],
        out_shape=[jax.ShapeDtypeStruct((B, S, D), F32),
                   jax.ShapeDtypeStruct((B, KV_WIDTH, WINDOW), F32),
                   jax.ShapeDtypeStruct((B, KV_WIDTH, WINDOW), F32),
                   jax.ShapeDtypeStruct((B, CHUNK, CHUNK_WIDTH), F32)],
        scratch_shapes=[pltpu.VMEM((8, D), F32),
                        pltpu.VMEM((CHUNK, CHUNK_WIDTH), F32),
                        pltpu.VMEM((N_CHUNK_HEADS // 2, CHUNK, 2 * CHUNK), BF16),
                        pltpu.VMEM((N_KV_HEADS, Q_PER_KV * WINDOW, 2 * WINDOW), F32),
                        pltpu.VMEM((2 * LANES, T + WINDOW), BF16),
                        pltpu.VMEM((N_KV_HEADS, T + WINDOW, LANES), BF16),
                        pltpu.VMEM((T, CHUNK_WIDTH), F32),
                        pltpu.VMEM((T, ATTN_WIDTH), F32),
                        pltpu.VMEM((T, CHUNK_WIDTH + ATTN_WIDTH), BF16),
                        pltpu.VMEM((T, D), F32),
                        pltpu.VMEM((T, D), F32),
                        pltpu.VMEM((T, D), BF16),
                        pltpu.VMEM((T, ATTN_WIDTH), F32),
                        pltpu.VMEM((T, D), BF16),
                        pltpu.VMEM((T, FF_CHUNK), BF16),
                        pltpu.VMEM((T, D), F32)],
        compiler_params=pltpu.CompilerParams(
            dimension_semantics=("arbitrary",), vmem_limit_bytes=VMEM_LIMIT),
        name="prompt_layer",
    )(sinks, x, w_in, w_o, w_up, w_down, *norms, w_sp, b_sp)


def _sample_kernel(L, layer, sink_ref, w4_ref, b4_ref, x_ref, ckt_ref, cvt_ref, w_in_ref,
                   w_o_ref, w_up_ref, w_down_ref,
                   g_pre, g_pm, g_pff, g_pf, ln_g, ln_b, g_oc, g_oa,
                   y_ref, vn_ref, wkt_ref, wvt_ref,
                   vec_ref, q_s, kvt_s, a_s, b_s):
    g = pl.program_id(0)
    n_steps = pl.num_programs(0)
    bb, hd, W = ckt_ref.shape
    n, d = q_s.shape[0], x_ref.shape[-1]
    n_heads = N_ATTN_HEADS
    pair_rows = 2 * L
    assert pair_rows == 8 and hd == LANES and W == LANES
    batch_per_block = LANES // L

    @pl.when(g == 0)
    def _project_and_gate():
        cw = CHUNK_WIDTH
        _fill_vector_table(vec_ref, g_pre, g_pm, g_pff, g_pf, ln_g, ln_b, g_oc, g_oa)
        h = _rms(x_ref[...].reshape(n, d), vec_ref[ROW_G_PRE:ROW_G_PRE + 1, :d]).astype(BF16)
        u = jax.nn.gelu(_dot(h, w_in_ref[:, 0:cw]))
        v = _layer_norm(jax.nn.gelu(_dot(h, w_in_ref[:, cw:2 * cw])),
                        vec_ref[ROW_LN:ROW_LN + 1, 0:cw], vec_ref[ROW_LN:ROW_LN + 1, cw:2 * cw])
        vn_ref[...] = v.reshape(vn_ref.shape)
        q_s[...] = _dot(h, w_in_ref[:, 2 * cw:2 * cw + ATTN_WIDTH]) * ATTN_SCALE
        kv0 = 2 * cw + ATTN_WIDTH
        kvt_s[...] = _dot(h, w_in_ref[:, kv0:kv0 + 2 * KV_WIDTH]).T
        t_of_row = lax.broadcasted_iota(jnp.int32, (n, 1), 0) % L
        group_of_lane = lax.broadcasted_iota(jnp.int32, (1, cw), 1) // HEAD_DIM

        def per_group(value_of_group):
            row = jnp.zeros((1, cw), F32)
            for grp in range(N_CHUNK_HEADS):
                row = jnp.where(group_of_lane == grp, value_of_group(grp), row)
            return row

        s = jnp.zeros(v.shape, F32)
        for t in range(L):
            s = jnp.where(t_of_row == t, per_group(lambda grp: b4_ref[layer, grp, t]), s)
        for delta in range(L):
            coef = jnp.zeros(v.shape, F32)
            for t in range(delta, L):
                w_row = per_group(lambda grp: w4_ref[grp, t, t - delta])
                coef = jnp.where(t_of_row == t, w_row, coef)
            vs = v if delta == 0 else pltpu.roll(v, delta, axis=0)
            s = s + coef * vs
        a_s[...] = u * s

    rows_all = n_heads * pair_rows
    r_idx = lax.broadcasted_iota(jnp.int32, (rows_all, LANES), 0)
    l_idx = lax.broadcasted_iota(jnp.int32, (rows_all, LANES), 1)
    t_idx = r_idx % L
    odd = (r_idx % pair_rows) >= L
    head_of_row = lax.broadcasted_iota(jnp.int32, (rows_all, 1), 0) // pair_rows
    sink = jnp.zeros((rows_all, 1), F32)
    slope = jnp.zeros((rows_all, 1), F32)
    for hh in range(n_heads):
        sink = jnp.where(head_of_row == hh, sink_ref[layer, hh // Q_PER_KV, hh % Q_PER_KV], sink)
        slope = jnp.where(head_of_row == hh, _alibi_slope(hh), slope)
    dist_old = W + t_idx - l_idx
    bias_old = jnp.where(dist_old <= WINDOW, -(slope * dist_old.astype(F32)), -jnp.inf)
    lane8 = lax.broadcasted_iota(jnp.int32, (pair_rows, LANES), 1)
    lo8 = lane8 < HALF
    lane_w = lax.broadcasted_iota(jnp.int32, (hd, W), 1)

    first_batch = g * bb
    new_block = pl.multiple_of((first_batch // batch_per_block) * LANES, LANES)
    kt_new = kvt_s[0:KV_WIDTH, pl.ds(new_block, LANES)]
    vt_new = kvt_s[KV_WIDTH:2 * KV_WIDTH, pl.ds(new_block, LANES)]
    kt_new_b = kt_new.astype(BF16)
    vt_new_b = vt_new.astype(BF16)
    for p in range(bb // 2):
        off = ((first_batch + 2 * p) % batch_per_block) * L
        row0 = pl.multiple_of((first_batch + 2 * p) * L, pair_rows)
        tiles = []
        for m in range(N_PAIRS):
            qb = q_s[pl.ds(row0, pair_rows), m * LANES:(m + 1) * LANES]
            qb_sw = pltpu.roll(qb, HALF, axis=1)
            if (2 * m) // Q_PER_KV == 0:
                tiles += [jnp.where(lo8, qb, 0.0), jnp.where(lo8, qb_sw, 0.0)]
            else:
                tiles += [jnp.where(lo8, 0.0, qb_sw), jnp.where(lo8, 0.0, qb)]
        lhs = jnp.concatenate(tiles, axis=0).astype(BF16)
        s_pair = []
        for e in range(2):
            rhs = jnp.concatenate([ckt_ref[2 * p + e].astype(BF16), kt_new_b], axis=1)
            s_pair.append(_dot(lhs, rhs))
        sc = jnp.where(jnp.concatenate([odd, odd], axis=1), s_pair[1], s_pair[0])
        rel = l_idx - off - jnp.where(odd, L, 0)
        bias_new = jnp.where((rel >= 0) & (rel <= t_idx),
                             -(slope * (t_idx - rel).astype(F32)), -jnp.inf)
        sc = sc + jnp.concatenate([bias_old, bias_new], axis=1)
        mx = jnp.maximum(jnp.max(sc, axis=-1, keepdims=True), sink)
        p_un = jnp.exp(sc - mx)
        z = jnp.sum(p_un, axis=-1, keepdims=True) + jnp.exp(sink - mx)
        pb = p_un.astype(BF16)
        o_pair = []
        for e in range(2):
            vt = jnp.concatenate([cvt_ref[2 * p + e].astype(BF16), vt_new_b], axis=1)
            o_pair.append(lax.dot_general(pb, vt, (((1,), (1,)), ((), ())),
                                          preferred_element_type=F32))
        o = jnp.where(odd, o_pair[1], o_pair[0]) * (1.0 / z)
        for m in range(N_PAIRS):
            kv_head = (2 * m) // Q_PER_KV
            o_even = o[(2 * m) * pair_rows:(2 * m + 1) * pair_rows]
            o_odd = o[(2 * m + 1) * pair_rows:(2 * m + 2) * pair_rows]
            if kv_head == 0:
                o_odd = pltpu.roll(o_odd, HALF, axis=1)
            else:
                o_even = pltpu.roll(o_even, HALF, axis=1)
            b_s[pl.ds(row0, pair_rows), m * LANES:(m + 1) * LANES] = jnp.where(lo8, o_even, o_odd)
        for e in range(2):
            shift_new = (W - L) - (off + e * L)
            keep = lane_w < W - L
            wkt_ref[2 * p + e] = jnp.where(keep, pltpu.roll(ckt_ref[2 * p + e], W - L, axis=1),
                                           pltpu.roll(kt_new, shift_new, axis=1))
            wvt_ref[2 * p + e] = jnp.where(keep, pltpu.roll(cvt_ref[2 * p + e], W - L, axis=1),
                                           pltpu.roll(vt_new, shift_new, axis=1))

    @pl.when(g == n_steps - 1)
    def _merge_and_ffn():
        x1 = _merge(x_ref[...].reshape(n, d), a_s[...], b_s[...], w_o_ref, vec_ref)
        y_ref[...] = _ffn(x1, w_up_ref, w_down_ref, vec_ref).reshape(y_ref.shape)


def _sample_call(x, layer, sinks, w4, b_sp, cache_kt, cache_vt, w_in, w_o, w_up, w_down,
                 norms):
    Bd, L, D = x.shape
    hd, W = cache_kt.shape[1:]
    n = Bd * L
    bb = SAMPLE_BB
    assert Bd % bb == 0 and bb % 2 == 0 and (LANES // L) % bb == 0
    smem = pl.BlockSpec(memory_space=pltpu.SMEM)
    blk = pl.BlockSpec((bb, hd, W), lambda g: (g, 0, 0))
    const = lambda shape: pl.BlockSpec(shape, lambda g: (0,) * len(shape))
    weights = (w_in, w_o, w_up, w_down)
    return pl.pallas_call(
        functools.partial(_sample_kernel, L, layer),
        grid=(Bd // bb,),
        in_specs=[smem, smem, smem, _resident(x.shape), blk, blk] +
                 [_resident(w.shape) for w in weights] + [_layer_row(a, layer) for a in norms],
        out_specs=[const((Bd, L, D)), const((Bd, L, CHUNK_WIDTH)), blk, blk],
        out_shape=[jax.ShapeDtypeStruct((Bd, L, D), F32),
                   jax.ShapeDtypeStruct((Bd, L, CHUNK_WIDTH), F32),
                   jax.ShapeDtypeStruct(cache_kt.shape, F32),
                   jax.ShapeDtypeStruct(cache_vt.shape, F32)],
        scratch_shapes=[pltpu.VMEM((8, D), F32),
                        pltpu.VMEM((n, ATTN_WIDTH), F32),
                        pltpu.VMEM((2 * KV_WIDTH, n), F32),
                        pltpu.VMEM((n, CHUNK_WIDTH), F32),
                        pltpu.VMEM((n, ATTN_WIDTH), F32)],
        compiler_params=pltpu.CompilerParams(
            dimension_semantics=("arbitrary",), vmem_limit_bytes=VMEM_LIMIT),
        name="sample_layer",
    )(sinks, w4, b_sp, x, cache_kt, cache_vt, *weights, *norms)


def _heads_last(t):
    b, _, w = t.shape
    return t.reshape(b, N_KV_HEADS, HEAD_DIM, w).transpose(0, 3, 1, 2)


def kernel(x_prompt, x_sample, cache_win_k, cache_win_v, w_in, g_pre_mix, ln_v_g, ln_v_b,
           w_spatial, b_spatial, attn_sinks, g_out_chunk, g_out_attn, w_o, g_post_mix,
           g_pre_ffn, w_up, w_down, g_post_ffn):
    depth = w_in.shape[0]
    Bd, L, D = x_sample.shape
    W = cache_win_k.shape[2]
    norms = (g_pre_mix, g_post_mix, g_pre_ffn, g_post_ffn, ln_v_g, ln_v_b, g_out_chunk,
             g_out_attn)
    yp, ys = x_prompt, x_sample
    wk_p, wv_p, cv_p, wk_s, wv_s, cv_s = [], [], [], [], [], []
    for l in range(depth):
        w_in_b = w_in[l].astype(BF16)
        w_o_b = w_o[l].astype(BF16)
        w_up_b = w_up[l].astype(BF16)
        w_down_b = w_down[l].astype(BF16)

        yp, wk, wv, cv = _prompt_call(yp, l, attn_sinks, w_in_b, w_o_b, w_up_b, w_down_b, norms,
                                      w_spatial, b_spatial)
        wk_p.append(_heads_last(wk))
        wv_p.append(_heads_last(wv))
        cv_p.append(cv)

        w4 = w_spatial[l, :, :L, :L]
        cache_kt = cache_win_k[l].transpose(0, 2, 3, 1).reshape(Bd, KV_WIDTH, W)
        cache_vt = cache_win_v[l].transpose(0, 2, 3, 1).reshape(Bd, KV_WIDTH, W)
        ys, cv, wk, wv = _sample_call(ys, l, attn_sinks, w4, b_spatial, cache_kt, cache_vt,
                                      w_in_b, w_o_b, w_up_b, w_down_b, norms)
        wk_s.append(_heads_last(wk))
        wv_s.append(_heads_last(wv))
        cv_s.append(cv)
    return (yp, ys, jnp.stack(wk_p), jnp.stack(wv_p), jnp.stack(cv_p),
            jnp.stack(wk_s), jnp.stack(wv_s), jnp.stack(cv_s))
```

```python
import functools

import jax
import jax.numpy as jnp
from jax import lax
from jax.experimental import pallas as pl
from jax.experimental.pallas import tpu as pltpu

F32 = jnp.float32
BF16 = jnp.bfloat16

HEAD_DIM = 64
N_CHUNK_HEADS = 8
N_ATTN_HEADS = 8
N_KV_HEADS = 2
Q_PER_KV = N_ATTN_HEADS // N_KV_HEADS
CHUNK_WIDTH = N_CHUNK_HEADS * HEAD_DIM
ATTN_WIDTH = N_ATTN_HEADS * HEAD_DIM
KV_WIDTH = N_KV_HEADS * HEAD_DIM
CHUNK = 128
WINDOW = 128
EPS = 1e-6
ATTN_SCALE = HEAD_DIM ** -0.5

LANES = 128
HALF = LANES // 2
N_PAIRS = ATTN_WIDTH // LANES
VMEM_LIMIT = 56 * 1024 * 1024

PROMPT_TILE = 512
FF_CHUNK = 512
EARLY_CHUNKS = 1
SAMPLE_BB = 16

ROW_G_PRE, ROW_G_PM, ROW_G_PFF, ROW_G_PF, ROW_LN, ROW_GOUT = range(6)


def _dot(a, b):
    return jnp.dot(a, b, preferred_element_type=F32)


def _rms(x, g):
    return x * lax.rsqrt(jnp.mean(x * x, axis=-1, keepdims=True) + EPS) * g


def _layer_norm(x, g, b):
    xc = x - jnp.mean(x, axis=-1, keepdims=True)
    return xc * lax.rsqrt(jnp.mean(xc * xc, axis=-1, keepdims=True) + EPS) * g + b


def _alibi_slope(head):
    return 2.0 ** (-8.0 * (head + 1) / N_ATTN_HEADS)


def _merge(x, a_out, b_out, w_o_ref, vec_ref):
    d = x.shape[-1]
    cw = CHUNK_WIDTH
    ra = _rms(a_out, vec_ref[ROW_GOUT:ROW_GOUT + 1, 0:cw]).astype(BF16)
    rb = _rms(b_out, vec_ref[ROW_GOUT:ROW_GOUT + 1, cw:cw + ATTN_WIDTH]).astype(BF16)
    o = _dot(ra, w_o_ref[0:cw, :]) + _dot(rb, w_o_ref[cw:cw + ATTN_WIDTH, :])
    return x + _rms(o, vec_ref[ROW_G_PM:ROW_G_PM + 1, :d])


def _relu_sq(up):
    r = jnp.maximum(up, 0.0)
    return (r * r).astype(BF16)


def _ffn(x1, w_up_ref, w_down_ref, vec_ref):
    d = x1.shape[-1]
    h2 = _rms(x1, vec_ref[ROW_G_PFF:ROW_G_PFF + 1, :d]).astype(BF16)
    d_ff = w_up_ref.shape[1]
    f = jnp.zeros(x1.shape, F32)
    for c in range(d_ff // FF_CHUNK):
        up = _dot(h2, w_up_ref[:, c * FF_CHUNK:(c + 1) * FF_CHUNK])
        f = f + _dot(_relu_sq(up), w_down_ref[c * FF_CHUNK:(c + 1) * FF_CHUNK, :])
    return x1 + _rms(f, vec_ref[ROW_G_PF:ROW_G_PF + 1, :d])


def _fill_vector_table(vec_s, g_pre, g_pm, g_pff, g_pf, ln_g, ln_b, g_oc, g_oa):
    d = g_pre.shape[-1]
    cw = CHUNK_WIDTH
    vec_s[ROW_G_PRE:ROW_G_PRE + 1, 0:d] = g_pre[...]
    vec_s[ROW_G_PM:ROW_G_PM + 1, 0:d] = g_pm[...]
    vec_s[ROW_G_PFF:ROW_G_PFF + 1, 0:d] = g_pff[...]
    vec_s[ROW_G_PF:ROW_G_PF + 1, 0:d] = g_pf[...]
    vec_s[ROW_LN:ROW_LN + 1, 0:cw] = ln_g[...]
    vec_s[ROW_LN:ROW_LN + 1, cw:2 * cw] = ln_b[...]
    vec_s[ROW_GOUT:ROW_GOUT + 1, 0:cw] = g_oc[...]
    vec_s[ROW_GOUT:ROW_GOUT + 1, cw:cw + ATTN_WIDTH] = g_oa[...]


def _prompt_kernel(n_tiles, nj, layer, sink_ref, x_ref, w_in_ref, w_o_ref, w_up_ref, w_down_ref,
                   g_pre, g_pm, g_pff, g_pf, ln_g, ln_b, g_oc, g_oa, wsp_ref, bsp_ref,
                   y_ref, wk_ref, wv_ref, cv_ref,
                   vec_ref, bias_s, wcat_s, tab_s, kt_s, vd_s, a_s, bo_s, m_s, x1_s, x1n_s,
                   h_s, q_s, h2_s, act_s, f_s):
    step = pl.program_id(0)
    j = jnp.minimum(step, n_tiles - 1) % nj
    T, d = x_ref.shape[1], x_ref.shape[2]
    nb = T // WINDOW
    cw = CHUNK_WIDTH
    n_chunks = w_up_ref.shape[1] // FF_CHUNK

    lane = lax.broadcasted_iota(jnp.int32, (WINDOW, LANES), 1)
    lo = lane < HALF

    @pl.when(step == 0)
    def _init_tables():
        _fill_vector_table(vec_ref, g_pre, g_pm, g_pff, g_pf, ln_g, ln_b, g_oc, g_oa)
        t_idx = lax.broadcasted_iota(jnp.int32, (CHUNK, CHUNK), 0)
        s_idx = lax.broadcasted_iota(jnp.int32, (CHUNK, CHUNK), 1)
        for g in range(N_CHUNK_HEADS):
            wm = jnp.where(s_idx <= t_idx, wsp_ref[0, g], 0.0).astype(BF16)
            wcat_s[g // 2, :, (g % 2) * CHUNK:(g % 2 + 1) * CHUNK] = wm
        b_rows = jnp.concatenate(
            [bsp_ref[0], jnp.zeros((CHUNK - N_CHUNK_HEADS, CHUNK), F32)], axis=0)
        b_cols = b_rows.T
        for p in range(cw // LANES):
            bias_s[:, p * LANES:(p + 1) * LANES] = jnp.where(
                lo, jnp.broadcast_to(b_cols[:, 2 * p:2 * p + 1], (CHUNK, LANES)),
                jnp.broadcast_to(b_cols[:, 2 * p + 1:2 * p + 2], (CHUNK, LANES)))
        a_idx = lax.broadcasted_iota(jnp.int32, (WINDOW, 2 * WINDOW), 0)
        c_idx = lax.broadcasted_iota(jnp.int32, (WINDOW, 2 * WINDOW), 1)
        dist = WINDOW + a_idx - c_idx
        valid = (dist >= 0) & (dist <= WINDOW)
        for hd in range(N_ATTN_HEADS):
            bias = jnp.where(valid, -(_alibi_slope(hd) * dist.astype(F32)), -jnp.inf)
            r = hd % Q_PER_KV
            tab_s[hd // Q_PER_KV, r * WINDOW:(r + 1) * WINDOW, :] = bias

    @pl.when(j == 0)
    def _clear_carry():
        kt_s[:, 0:WINDOW] = jnp.zeros((2 * LANES, WINDOW), BF16)
        vd_s[:, 0:WINDOW, :] = jnp.zeros((N_KV_HEADS, WINDOW, LANES), BF16)

    def ffn_up(c):
        act_s[...] = _relu_sq(_dot(h2_s[...], w_up_ref[:, c * FF_CHUNK:(c + 1) * FF_CHUNK]))

    def ffn_down(c):
        part = _dot(act_s[...], w_down_ref[c * FF_CHUNK:(c + 1) * FF_CHUNK, :])
        if c == 0:
            f_s[...] = part
        else:
            f_s[...] += part

    def finish_previous():
        y_ref[0] = x1_s[...] + _rms(f_s[...], vec_ref[ROW_G_PF:ROW_G_PF + 1, :d])

    def tile_step(with_late):
        late = iter([functools.partial(fn, c) for c in range(EARLY_CHUNKS, n_chunks)
                     for fn in (ffn_up, ffn_down)])

        def late_stage(count=1):
            if with_late:
                for _ in range(count):
                    next(late)()

        late_stage()
        h_s[...] = _rms(x_ref[0], vec_ref[ROW_G_PRE:ROW_G_PRE + 1, :d]).astype(BF16)
        u_pre = _dot(h_s[...], w_in_ref[:, 0:cw])
        v_pre = _dot(h_s[...], w_in_ref[:, cw:2 * cw])
        late_stage()
        u = jax.nn.gelu(u_pre)
        late_stage()
        v = _layer_norm(jax.nn.gelu(v_pre), vec_ref[ROW_LN:ROW_LN + 1, 0:cw],
                        vec_ref[ROW_LN:ROW_LN + 1, cw:2 * cw])
        kv0 = 2 * cw + ATTN_WIDTH
        kv = _dot(h_s[...], w_in_ref[:, kv0:kv0 + 2 * KV_WIDTH])
        k = kv[:, 0:KV_WIDTH]
        val = kv[:, KV_WIDTH:2 * KV_WIDTH]
        q0 = 2 * cw
        q_s[...] = _dot(h_s[...], w_in_ref[:, q0:q0 + ATTN_WIDTH]) * ATTN_SCALE
        late_stage()

        kt_f = k.T
        wk_ref[0] = kt_f[:, T - WINDOW:]
        wv_ref[0] = val[T - WINDOW:, :].T
        cv_ref[0] = v[T - CHUNK:, :]

        assert nb % 2 == 0
        for c in range(0, nb, 2):
            for p in range(cw // LANES):
                cols = slice(p * LANES, (p + 1) * LANES)
                halves = []
                for cc in (c, c + 1):
                    vcp = v[cc * CHUNK:(cc + 1) * CHUNK, cols]
                    halves.append(jnp.concatenate([jnp.where(lo, vcp, 0.0), jnp.where(lo, 0.0, vcp)],
                                                  axis=0))
                rhs = jnp.concatenate(halves, axis=1).astype(BF16)
                s_pair = _dot(wcat_s[p], rhs)
                for side, cc in enumerate((c, c + 1)):
                    rows = slice(cc * CHUNK, (cc + 1) * CHUNK)
                    s_cp = s_pair[:, side * LANES:(side + 1) * LANES] + bias_s[:, cols]
                    a_s[rows, cols] = u[rows, cols] * s_cp
        m_s[:, 0:cw] = _rms(a_s[...], vec_ref[ROW_GOUT:ROW_GOUT + 1, 0:cw]).astype(BF16)

        kt = kt_f.astype(BF16)
        for hk in range(N_KV_HEADS):
            kth = kt[hk * HEAD_DIM:(hk + 1) * HEAD_DIM, :]
            kt_s[hk * LANES:hk * LANES + HEAD_DIM, WINDOW:] = kth
            kt_s[hk * LANES + HEAD_DIM:(hk + 1) * LANES, WINDOW:] = kth
        lane_t = lax.broadcasted_iota(jnp.int32, (T, LANES), 1)
        lo_t = lane_t < HALF
        val_sw = pltpu.roll(val, HALF, axis=1)
        vd_s[0, WINDOW:, :] = jnp.where(lo_t, val, val_sw).astype(BF16)
        vd_s[1, WINDOW:, :] = jnp.where(lo_t, val_sw, val).astype(BF16)

        first_block_bias = jnp.where(j == 0, -jnp.inf, 0.0).astype(F32)
        head_of_row = lax.broadcasted_iota(jnp.int32, (Q_PER_KV * WINDOW, 1), 0) // WINDOW
        n_units = nb * N_KV_HEADS
        n_mid = 2 * (n_chunks - EARLY_CHUNKS) - 9
        late_after_scores = tuple(1 + (k * n_units) // n_mid for k in range(n_mid))
        for n, (i, hk) in enumerate((i, hk) for i in range(nb) for hk in range(N_KV_HEADS)):
            rows = slice(i * WINDOW, (i + 1) * WINDOW)
            keys = slice(i * WINDOW, (i + 2) * WINDOW)
            tiles = []
            for m in range(hk * 2, hk * 2 + 2):
                qb = q_s[rows, m * LANES:(m + 1) * LANES]
                tiles += [jnp.where(lo, qb, 0.0), jnp.where(lo, 0.0, qb)]
            qq = jnp.concatenate(tiles, axis=0).astype(BF16)
            s = _dot(qq, kt_s[hk * LANES:(hk + 1) * LANES, keys]) + tab_s[hk]
            if i == 0:
                s = jnp.concatenate([s[:, 0:WINDOW] + first_block_bias, s[:, WINDOW:]], axis=1)
            if n in late_after_scores:
                late_stage()
            sink = jnp.zeros((Q_PER_KV * WINDOW, 1), F32)
            for r in range(Q_PER_KV):
                sink = jnp.where(head_of_row == r, sink_ref[layer, hk, r], sink)
            mx = jnp.maximum(jnp.max(s, axis=-1, keepdims=True), sink)
            p_un = jnp.exp(s - mx)
            z = jnp.sum(p_un, axis=-1, keepdims=True) + jnp.exp(sink - mx)
            o = _dot(p_un.astype(BF16), vd_s[hk, keys, :]) * (1.0 / z)
            for pair in range(Q_PER_KV // 2):
                m = hk * 2 + pair
                o_even = o[(2 * pair) * WINDOW:(2 * pair + 1) * WINDOW]
                o_odd = o[(2 * pair + 1) * WINDOW:(2 * pair + 2) * WINDOW]
                bo_s[rows, m * LANES:(m + 1) * LANES] = jnp.where(lo, o_even, o_odd)

        kt_s[:, 0:WINDOW] = kt_s[:, T:T + WINDOW]
        vd_s[:, 0:WINDOW, :] = vd_s[:, T:T + WINDOW, :]

        late_stage()
        m_s[:, cw:cw + ATTN_WIDTH] = _rms(
            bo_s[...], vec_ref[ROW_GOUT:ROW_GOUT + 1, cw:cw + ATTN_WIDTH]).astype(BF16)
        o_proj = _dot(m_s[...], w_o_ref[...])
        late_stage(3)
        x1 = x_ref[0] + _rms(o_proj, vec_ref[ROW_G_PM:ROW_G_PM + 1, :d])
        x1n_s[...] = x1
        h2_s[...] = _rms(x1, vec_ref[ROW_G_PFF:ROW_G_PFF + 1, :d]).astype(BF16)
        late_stage()

        ffn_up(0)
        if with_late:
            finish_previous()
        x1_s[...] = x1n_s[...]
        ffn_down(0)
        for c in range(1, EARLY_CHUNKS):
            ffn_up(c)
            ffn_down(c)

    @pl.when(step == 0)
    def _first_step():
        tile_step(False)

    @pl.when((step > 0) & (step < n_tiles))
    def _steady_step():
        tile_step(True)

    @pl.when(step == n_tiles)
    def _drain_step():
        for c in range(EARLY_CHUNKS, n_chunks):
            ffn_up(c)
            ffn_down(c)
        finish_previous()


def _resident(shape):
    return pl.BlockSpec(shape, lambda *_: (0,) * len(shape), pipeline_mode=pl.Buffered(1))


def _layer_row(arr, layer):
    block = (1,) + arr.shape[1:]
    return pl.BlockSpec(block, lambda *_: (layer,) + (0,) * (arr.ndim - 1),
                        pipeline_mode=pl.Buffered(1))


def _prompt_call(x, layer, sinks, w_in, w_o, w_up, w_down, norms, w_sp, b_sp):
    B, S, D = x.shape
    T = PROMPT_TILE
    assert S % T == 0 and T % WINDOW == 0 and WINDOW == CHUNK
    assert 1 <= EARLY_CHUNKS and 2 * (w_up.shape[1] // FF_CHUNK - EARLY_CHUNKS) >= 9
    nj = S // T
    n_tiles = B * nj
    cur = lambda s: jnp.minimum(s, n_tiles - 1)
    prev = lambda s: jnp.maximum(s - 1, 0)
    tile_in = pl.BlockSpec((1, T, D), lambda s: (cur(s) // nj, cur(s) % nj, 0))
    tile_out = pl.BlockSpec((1, T, D), lambda s: (prev(s) // nj, prev(s) % nj, 0))
    last = lambda r, c: pl.BlockSpec((1, r, c), lambda s: (cur(s) // nj, 0, 0))
    return pl.pallas_call(
        functools.partial(_prompt_kernel, n_tiles, nj, layer),
        grid=(n_tiles + 1,),
        in_specs=[pl.BlockSpec(memory_space=pltpu.SMEM), tile_in,
                  _resident(w_in.shape), _resident(w_o.shape), _resident(w_up.shape),
                  _resident(w_down.shape)] + [_layer_row(a, layer) for a in norms] +
                 [_layer_row(w_sp, layer), _layer_row(b_sp, layer)],
        out_specs=[tile_out, last(KV_WIDTH, WINDOW), last(KV_WIDTH, WINDOW),
                   last(CHUNK, CHUNK_WIDTH)],
        out_shape=[jax.ShapeDtypeStruct((B, S, D), F32),
                   jax.ShapeDtypeStruct((B, KV_WIDTH, WINDOW), F32),
                   jax.ShapeDtypeStruct((B, KV_WIDTH, WINDOW), F32),
                   jax.ShapeDtypeStruct((B, CHUNK, CHUNK_WIDTH), F32)],
        scratch_shapes=[pltpu.VMEM((8, D), F32),
                        pltpu.VMEM((CHUNK, CHUNK_WIDTH), F32),
                        pltpu.VMEM((N_CHUNK_HEADS // 2, CHUNK, 2 * CHUNK), BF16),
                        pltpu.VMEM((N_KV_HEADS, Q_PER_KV * WINDOW, 2 * WINDOW), F32),
                        pltpu.VMEM((2 * LANES, T + WINDOW), BF16),
                        pltpu.VMEM((N_KV_HEADS, T + WINDOW, LANES), BF16),
                        pltpu.VMEM((T, CHUNK_WIDTH), F32),
                        pltpu.VMEM((T, ATTN_WIDTH), F32),
                        pltpu.VMEM((T, CHUNK_WIDTH + ATTN_WIDTH), BF16),
                        pltpu.VMEM((T, D), F32),
                        pltpu.VMEM((T, D), F32),
                        pltpu.VMEM((T, D), BF16),
                        pltpu.VMEM((T, ATTN_WIDTH), F32),
                        pltpu.VMEM((T, D), BF16),
                        pltpu.VMEM((T, FF_CHUNK), BF16),
                        pltpu.VMEM((T, D), F32)],
        compiler_params=pltpu.CompilerParams(
            dimension_semantics=("arbitrary",), vmem_limit_bytes=VMEM_LIMIT),
        name="prompt_layer",
    )(sinks, x, w_in, w_o, w_up, w_down, *norms, w_sp, b_sp)


def _sample_kernel(L, layer, sink_ref, w4_ref, b4_ref, x_ref, ckt_ref, cvt_ref, w_in_ref,
                   w_o_ref, w_up_ref, w_down_ref,
                   g_pre, g_pm, g_pff, g_pf, ln_g, ln_b, g_oc, g_oa,
                   y_ref, vn_ref, wkt_ref, wvt_ref,
                   vec_ref, q_s, kvt_s, a_s, b_s):
    g = pl.program_id(0)
    n_steps = pl.num_programs(0)
    bb, hd, W = ckt_ref.shape
    n, d = q_s.shape[0], x_ref.shape[-1]
    n_heads = N_ATTN_HEADS
    pair_rows = 2 * L
    assert pair_rows == 8 and hd == LANES and W == LANES
    batch_per_block = LANES // L

    @pl.when(g == 0)
    def _project_and_gate():
        cw = CHUNK_WIDTH
        _fill_vector_table(vec_ref, g_pre, g_pm, g_pff, g_pf, ln_g, ln_b, g_oc, g_oa)
        h = _rms(x_ref[...].reshape(n, d), vec_ref[ROW_G_PRE:ROW_G_PRE + 1, :d]).astype(BF16)
        u = jax.nn.gelu(_dot(h, w_in_ref[:, 0:cw]))
        v = _layer_norm(jax.nn.gelu(_dot(h, w_in_ref[:, cw:2 * cw])),
                        vec_ref[ROW_LN:ROW_LN + 1, 0:cw], vec_ref[ROW_LN:ROW_LN + 1, cw:2 * cw])
        vn_ref[...] = v.reshape(vn_ref.shape)
        q_s[...] = _dot(h, w_in_ref[:, 2 * cw:2 * cw + ATTN_WIDTH]) * ATTN_SCALE
        kv0 = 2 * cw + ATTN_WIDTH
        kvt_s[...] = _dot(h, w_in_ref[:, kv0:kv0 + 2 * KV_WIDTH]).T
        t_of_row = lax.broadcasted_iota(jnp.int32, (n, 1), 0) % L
        group_of_lane = lax.broadcasted_iota(jnp.int32, (1, cw), 1) // HEAD_DIM

        def per_group(value_of_group):
            row = jnp.zeros((1, cw), F32)
            for grp in range(N_CHUNK_HEADS):
                row = jnp.where(group_of_lane == grp, value_of_group(grp), row)
            return row

        s = jnp.zeros(v.shape, F32)
        for t in range(L):
            s = jnp.where(t_of_row == t, per_group(lambda grp: b4_ref[layer, grp, t]), s)
        for delta in range(L):
            coef = jnp.zeros(v.shape, F32)
            for t in range(delta, L):
                w_row = per_group(lambda grp: w4_ref[grp, t, t - delta])
                coef = jnp.where(t_of_row == t, w_row, coef)
            vs = v if delta == 0 else pltpu.roll(v, delta, axis=0)
            s = s + coef * vs
        a_s[...] = u * s

    rows_all = n_heads * pair_rows
    r_idx = lax.broadcasted_iota(jnp.int32, (rows_all, LANES), 0)
    l_idx = lax.broadcasted_iota(jnp.int32, (rows_all, LANES), 1)
    t_idx = r_idx % L
    odd = (r_idx % pair_rows) >= L
    head_of_row = lax.broadcasted_iota(jnp.int32, (rows_all, 1), 0) // pair_rows
    sink = jnp.zeros((rows_all, 1), F32)
    slope = jnp.zeros((rows_all, 1), F32)
    for hh in range(n_heads):
        sink = jnp.where(head_of_row == hh, sink_ref[layer, hh // Q_PER_KV, hh % Q_PER_KV], sink)
        slope = jnp.where(head_of_row == hh, _alibi_slope(hh), slope)
    dist_old = W + t_idx - l_idx
    bias_old = jnp.where(dist_old <= WINDOW, -(slope * dist_old.astype(F32)), -jnp.inf)
    lane8 = lax.broadcasted_iota(jnp.int32, (pair_rows, LANES), 1)
    lo8 = lane8 < HALF
    lane_w = lax.broadcasted_iota(jnp.int32, (hd, W), 1)

    first_batch = g * bb
    new_block = pl.multiple_of((first_batch // batch_per_block) * LANES, LANES)
    kt_new = kvt_s[0:KV_WIDTH, pl.ds(new_block, LANES)]
    vt_new = kvt_s[KV_WIDTH:2 * KV_WIDTH, pl.ds(new_block, LANES)]
    kt_new_b = kt_new.astype(BF16)
    vt_new_b = vt_new.astype(BF16)
    for p in range(bb // 2):
        off = ((first_batch + 2 * p) % batch_per_block) * L
        row0 = pl.multiple_of((first_batch + 2 * p) * L, pair_rows)
        tiles = []
        for m in range(N_PAIRS):
            qb = q_s[pl.ds(row0, pair_rows), m * LANES:(m + 1) * LANES]
            qb_sw = pltpu.roll(qb, HALF, axis=1)
            if (2 * m) // Q_PER_KV == 0:
                tiles += [jnp.where(lo8, qb, 0.0), jnp.where(lo8, qb_sw, 0.0)]
            else:
                tiles += [jnp.where(lo8, 0.0, qb_sw), jnp.where(lo8, 0.0, qb)]
        lhs = jnp.concatenate(tiles, axis=0).astype(BF16)
        s_pair = []
        for e in range(2):
            rhs = jnp.concatenate([ckt_ref[2 * p + e].astype(BF16), kt_new_b], axis=1)
            s_pair.append(_dot(lhs, rhs))
        sc = jnp.where(jnp.concatenate([odd, odd], axis=1), s_pair[1], s_pair[0])
        rel = l_idx - off - jnp.where(odd, L, 0)
        bias_new = jnp.where((rel >= 0) & (rel <= t_idx),
                             -(slope * (t_idx - rel).astype(F32)), -jnp.inf)
        sc = sc + jnp.concatenate([bias_old, bias_new], axis=1)
        mx = jnp.maximum(jnp.max(sc, axis=-1, keepdims=True), sink)
        p_un = jnp.exp(sc - mx)
        z = jnp.sum(p_un, axis=-1, keepdims=True) + jnp.exp(sink - mx)
        pb = p_un.astype(BF16)
        o_pair = []
        for e in range(2):
            vt = jnp.concatenate([cvt_ref[2 * p + e].astype(BF16), vt_new_b], axis=1)
            o_pair.append(lax.dot_general(pb, vt, (((1,), (1,)), ((), ())),
                                          preferred_element_type=F32))
        o = jnp.where(odd, o_pair[1], o_pair[0]) * (1.0 / z)
        for m in range(N_PAIRS):
            kv_head = (2 * m) // Q_PER_KV
            o_even = o[(2 * m) * pair_rows:(2 * m + 1) * pair_rows]
            o_odd = o[(2 * m + 1) * pair_rows:(2 * m + 2) * pair_rows]
            if kv_head == 0:
                o_odd = pltpu.roll(o_odd, HALF, axis=1)
            else:
                o_even = pltpu.roll(o_even, HALF, axis=1)
            b_s[pl.ds(row0, pair_rows), m * LANES:(m + 1) * LANES] = jnp.where(lo8, o_even, o_odd)
        for e in range(2):
            shift_new = (W - L) - (off + e * L)
            keep = lane_w < W - L
            wkt_ref[2 * p + e] = jnp.where(keep, pltpu.roll(ckt_ref[2 * p + e], W - L, axis=1),
                                           pltpu.roll(kt_new, shift_new, axis=1))
            wvt_ref[2 * p + e] = jnp.where(keep, pltpu.roll(cvt_ref[2 * p + e], W - L, axis=1),
                                           pltpu.roll(vt_new, shift_new, axis=1))

    @pl.when(g == n_steps - 1)
    def _merge_and_ffn():
        x1 = _merge(x_ref[...].reshape(n, d), a_s[...], b_s[...], w_o_ref, vec_ref)
        y_ref[...] = _ffn(x1, w_up_ref, w_down_ref, vec_ref).reshape(y_ref.shape)


def _sample_call(x, layer, sinks, w4, b_sp, cache_kt, cache_vt, w_in, w_o, w_up, w_down,
                 norms):
    Bd, L, D = x.shape
    hd, W = cache_kt.shape[1:]
    n = Bd * L
    bb = SAMPLE_BB
    assert Bd % bb == 0 and bb % 2 == 0 and (LANES // L) % bb == 0
    smem = pl.BlockSpec(memory_space=pltpu.SMEM)
    blk = pl.BlockSpec((bb, hd, W), lambda g: (g, 0, 0))
    const = lambda shape: pl.BlockSpec(shape, lambda g: (0,) * len(shape))
    weights = (w_in, w_o, w_up, w_down)
    return pl.pallas_call(
        functools.partial(_sample_kernel, L, layer),
        grid=(Bd // bb,),
        in_specs=[smem, smem, smem, _resident(x.shape), blk, blk] +
                 [_resident(w.shape) for w in weights] + [_layer_row(a, layer) for a in norms],
        out_specs=[const((Bd, L, D)), const((Bd, L, CHUNK_WIDTH)), blk, blk],
        out_shape=[jax.ShapeDtypeStruct((Bd, L, D), F32),
                   jax.ShapeDtypeStruct((Bd, L, CHUNK_WIDTH), F32),
                   jax.ShapeDtypeStruct(cache_kt.shape, F32),
                   jax.ShapeDtypeStruct(cache_vt.shape, F32)],
        scratch_shapes=[pltpu.VMEM((8, D), F32),
                        pltpu.VMEM((n, ATTN_WIDTH), F32),
                        pltpu.VMEM((2 * KV_WIDTH, n), F32),
                        pltpu.VMEM((n, CHUNK_WIDTH), F32),
                        pltpu.VMEM((n, ATTN_WIDTH), F32)],
        compiler_params=pltpu.CompilerParams(
            dimension_semantics=("arbitrary",), vmem_limit_bytes=VMEM_LIMIT),
        name="sample_layer",
    )(sinks, w4, b_sp, x, cache_kt, cache_vt, *weights, *norms)


def _heads_last(t):
    b, _, w = t.shape
    return t.reshape(b, N_KV_HEADS, HEAD_DIM, w).transpose(0, 3, 1, 2)


def kernel(x_prompt, x_sample, cache_win_k, cache_win_v, w_in, g_pre_mix, ln_v_g, ln_v_b,
           w_spatial, b_spatial, attn_sinks, g_out_chunk, g_out_attn, w_o, g_post_mix,
           g_pre_ffn, w_up, w_down, g_post_ffn):
    depth = w_in.shape[0]
    Bd, L, D = x_sample.shape
    W = cache_win_k.shape[2]
    norms = (g_pre_mix, g_post_mix, g_pre_ffn, g_post_ffn, ln_v_g, ln_v_b, g_out_chunk,
             g_out_attn)
    yp, ys = x_prompt, x_sample
    wk_p, wv_p, cv_p, wk_s, wv_s, cv_s = [], [], [], [], [], []
    for l in range(depth):
        w_in_b = w_in[l].astype(BF16)
        w_o_b = w_o[l].astype(BF16)
        w_up_b = w_up[l].astype(BF16)
        w_down_b = w_down[l].astype(BF16)

        yp, wk, wv, cv = _prompt_call(yp, l, attn_sinks, w_in_b, w_o_b, w_up_b, w_down_b, norms,
                                      w_spatial, b_spatial)
        wk_p.append(_heads_last(wk))
        wv_p.append(_heads_last(wv))
        cv_p.append(cv)

        w4 = w_spatial[l, :, :L, :L]
        cache_kt = cache_win_k[l].transpose(0, 2, 3, 1).reshape(Bd, KV_WIDTH, W)
        cache_vt = cache_win_v[l].transpose(0, 2, 3, 1).reshape(Bd, KV_WIDTH, W)
        ys, cv, wk, wv = _sample_call(ys, l, attn_sinks, w4, b_spatial, cache_kt, cache_vt,
                                      w_in_b, w_o_b, w_up_b, w_down_b, norms)
        wk_s.append(_heads_last(wk))
        wv_s.append(_heads_last(wv))
        cv_s.append(cv)
    return (yp, ys, jnp.stack(wk_p), jnp.stack(wv_p), jnp.stack(cv_p),
            jnp.stack(wk_s), jnp.stack(wv_s), jnp.stack(cv_s))
```

```python
import functools

import jax
import jax.numpy as jnp
from jax import lax
from jax.experimental import pallas as pl
from jax.experimental.pallas import tpu as pltpu

F32 = jnp.float32
BF16 = jnp.bfloat16

HEAD_DIM = 64
N_CHUNK_HEADS = 8
N_ATTN_HEADS = 8
N_KV_HEADS = 2
Q_PER_KV = N_ATTN_HEADS // N_KV_HEADS
CHUNK_WIDTH = N_CHUNK_HEADS * HEAD_DIM
ATTN_WIDTH = N_ATTN_HEADS * HEAD_DIM
KV_WIDTH = N_KV_HEADS * HEAD_DIM
CHUNK = 128
WINDOW = 128
EPS = 1e-6
ATTN_SCALE = HEAD_DIM ** -0.5

LANES = 128
HALF = LANES // 2
N_PAIRS = ATTN_WIDTH // LANES
VMEM_LIMIT = 56 * 1024 * 1024

PROMPT_TILE = 512
FF_CHUNK = 512
EARLY_CHUNKS = 1
SAMPLE_BB = 16

ROW_G_PRE, ROW_G_PM, ROW_G_PFF, ROW_G_PF, ROW_LN, ROW_GOUT = range(6)


def _dot(a, b):
    return jnp.dot(a, b, preferred_element_type=F32)


def _rms(x, g):
    return x * lax.rsqrt(jnp.mean(x * x, axis=-1, keepdims=True) + EPS) * g


def _layer_norm(x, g, b):
    xc = x - jnp.mean(x, axis=-1, keepdims=True)
    return xc * lax.rsqrt(jnp.mean(xc * xc, axis=-1, keepdims=True) + EPS) * g + b


def _alibi_slope(head):
    return 2.0 ** (-8.0 * (head + 1) / N_ATTN_HEADS)


def _merge(x, a_out, b_out, w_o_ref, vec_ref):
    d = x.shape[-1]
    cw = CHUNK_WIDTH
    ra = _rms(a_out, vec_ref[ROW_GOUT:ROW_GOUT + 1, 0:cw]).astype(BF16)
    rb = _rms(b_out, vec_ref[ROW_GOUT:ROW_GOUT + 1, cw:cw + ATTN_WIDTH]).astype(BF16)
    o = _dot(ra, w_o_ref[0:cw, :]) + _dot(rb, w_o_ref[cw:cw + ATTN_WIDTH, :])
    return x + _rms(o, vec_ref[ROW_G_PM:ROW_G_PM + 1, :d])


def _relu_sq(up):
    r = jnp.maximum(up, 0.0)
    return (r * r).astype(BF16)


def _ffn(x1, w_up_ref, w_down_ref, vec_ref):
    d = x1.shape[-1]
    h2 = _rms(x1, vec_ref[ROW_G_PFF:ROW_G_PFF + 1, :d]).astype(BF16)
    d_ff = w_up_ref.shape[1]
    f = jnp.zeros(x1.shape, F32)
    for c in range(d_ff // FF_CHUNK):
        up = _dot(h2, w_up_ref[:, c * FF_CHUNK:(c + 1) * FF_CHUNK])
        f = f + _dot(_relu_sq(up), w_down_ref[c * FF_CHUNK:(c + 1) * FF_CHUNK, :])
    return x1 + _rms(f, vec_ref[ROW_G_PF:ROW_G_PF + 1, :d])


def _fill_vector_table(vec_s, g_pre, g_pm, g_pff, g_pf, ln_g, ln_b, g_oc, g_oa):
    d = g_pre.shape[-1]
    cw = CHUNK_WIDTH
    vec_s[ROW_G_PRE:ROW_G_PRE + 1, 0:d] = g_pre[...]
    vec_s[ROW_G_PM:ROW_G_PM + 1, 0:d] = g_pm[...]
    vec_s[ROW_G_PFF:ROW_G_PFF + 1, 0:d] = g_pff[...]
    vec_s[ROW_G_PF:ROW_G_PF + 1, 0:d] = g_pf[...]
    vec_s[ROW_LN:ROW_LN + 1, 0:cw] = ln_g[...]
    vec_s[ROW_LN:ROW_LN + 1, cw:2 * cw] = ln_b[...]
    vec_s[ROW_GOUT:ROW_GOUT + 1, 0:cw] = g_oc[...]
    vec_s[ROW_GOUT:ROW_GOUT + 1, cw:cw + ATTN_WIDTH] = g_oa[...]


def _prompt_kernel(n_tiles, nj, layer, sink_ref, x_ref, w_in_ref, w_o_ref, w_up_ref, w_down_ref,
                   g_pre, g_pm, g_pff, g_pf, ln_g, ln_b, g_oc, g_oa, wsp_ref, bsp_ref,
                   y_ref, wk_ref, wv_ref, cv_ref,
                   vec_ref, bias_s, wcat_s, tab_s, kt_s, vd_s, a_s, bo_s, m_s, x1_s, x1n_s,
                   h_s, q_s, h2_s, act_s, f_s):
    step = pl.program_id(0)
    j = jnp.minimum(step, n_tiles - 1) % nj
    T, d = x_ref.shape[1], x_ref.shape[2]
    nb = T // WINDOW
    cw = CHUNK_WIDTH
    n_chunks = w_up_ref.shape[1] // FF_CHUNK

    lane = lax.broadcasted_iota(jnp.int32, (WINDOW, LANES), 1)
    lo = lane < HALF

    @pl.when(step == 0)
    def _init_tables():
        x1_s[...] = jnp.zeros(x1_s.shape, F32)
        h2_s[...] = jnp.zeros(h2_s.shape, BF16)
        f_s[...] = jnp.zeros(f_s.shape, F32)
        _fill_vector_table(vec_ref, g_pre, g_pm, g_pff, g_pf, ln_g, ln_b, g_oc, g_oa)
        t_idx = lax.broadcasted_iota(jnp.int32, (CHUNK, CHUNK), 0)
        s_idx = lax.broadcasted_iota(jnp.int32, (CHUNK, CHUNK), 1)
        for g in range(N_CHUNK_HEADS):
            wm = jnp.where(s_idx <= t_idx, wsp_ref[0, g], 0.0).astype(BF16)
            wcat_s[g // 2, :, (g % 2) * CHUNK:(g % 2 + 1) * CHUNK] = wm
        b_rows = jnp.concatenate(
            [bsp_ref[0], jnp.zeros((CHUNK - N_CHUNK_HEADS, CHUNK), F32)], axis=0)
        b_cols = b_rows.T
        for p in range(cw // LANES):
            bias_s[:, p * LANES:(p + 1) * LANES] = jnp.where(
                lo, jnp.broadcast_to(b_cols[:, 2 * p:2 * p + 1], (CHUNK, LANES)),
                jnp.broadcast_to(b_cols[:, 2 * p + 1:2 * p + 2], (CHUNK, LANES)))
        a_idx = lax.broadcasted_iota(jnp.int32, (WINDOW, 2 * WINDOW), 0)
        c_idx = lax.broadcasted_iota(jnp.int32, (WINDOW, 2 * WINDOW), 1)
        dist = WINDOW + a_idx - c_idx
        valid = (dist >= 0) & (dist <= WINDOW)
        for hd in range(N_ATTN_HEADS):
            bias = jnp.where(valid, -(_alibi_slope(hd) * dist.astype(F32)), -jnp.inf)
            r = hd % Q_PER_KV
            tab_s[hd // Q_PER_KV, r * WINDOW:(r + 1) * WINDOW, :] = bias

    @pl.when(j == 0)
    def _clear_carry():
        kt_s[:, 0:WINDOW] = jnp.zeros((2 * LANES, WINDOW), BF16)
        vd_s[:, 0:WINDOW, :] = jnp.zeros((N_KV_HEADS, WINDOW, LANES), BF16)

    def ffn_up(c):
        act_s[...] = _relu_sq(_dot(h2_s[...], w_up_ref[:, c * FF_CHUNK:(c + 1) * FF_CHUNK]))

    def ffn_down(c):
        part = _dot(act_s[...], w_down_ref[c * FF_CHUNK:(c + 1) * FF_CHUNK, :])
        if c == 0:
            f_s[...] = part
        else:
            f_s[...] += part

    def finish_previous():
        y_ref[0] = x1_s[...] + _rms(f_s[...], vec_ref[ROW_G_PF:ROW_G_PF + 1, :d])

    def tile_step(with_late):
        late = iter([functools.partial(fn, c) for c in range(EARLY_CHUNKS, n_chunks)
                     for fn in (ffn_up, ffn_down)])

        def late_stage(count=1):
            if with_late:
                for _ in range(count):
                    next(late)()

        late_stage()
        h_s[...] = _rms(x_ref[0], vec_ref[ROW_G_PRE:ROW_G_PRE + 1, :d]).astype(BF16)
        u_pre = _dot(h_s[...], w_in_ref[:, 0:cw])
        v_pre = _dot(h_s[...], w_in_ref[:, cw:2 * cw])
        late_stage()
        u = jax.nn.gelu(u_pre)
        late_stage()
        v = _layer_norm(jax.nn.gelu(v_pre), vec_ref[ROW_LN:ROW_LN + 1, 0:cw],
                        vec_ref[ROW_LN:ROW_LN + 1, cw:2 * cw])
        kv0 = 2 * cw + ATTN_WIDTH
        kv = _dot(h_s[...], w_in_ref[:, kv0:kv0 + 2 * KV_WIDTH])
        k = kv[:, 0:KV_WIDTH]
        val = kv[:, KV_WIDTH:2 * KV_WIDTH]
        q0 = 2 * cw
        q_s[...] = _dot(h_s[...], w_in_ref[:, q0:q0 + ATTN_WIDTH]) * ATTN_SCALE
        late_stage()

        kt_f = k.T
        wk_ref[0] = kt_f[:, T - WINDOW:]
        wv_ref[0] = val[T - WINDOW:, :].T
        cv_ref[0] = v[T - CHUNK:, :]

        assert nb % 2 == 0
        for c in range(0, nb, 2):
            for p in range(cw // LANES):
                cols = slice(p * LANES, (p + 1) * LANES)
                halves = []
                for cc in (c, c + 1):
                    vcp = v[cc * CHUNK:(cc + 1) * CHUNK, cols]
                    halves.append(jnp.concatenate([jnp.where(lo, vcp, 0.0), jnp.where(lo, 0.0, vcp)],
                                                  axis=0))
                rhs = jnp.concatenate(halves, axis=1).astype(BF16)
                s_pair = _dot(wcat_s[p], rhs)
                for side, cc in enumerate((c, c + 1)):
                    rows = slice(cc * CHUNK, (cc + 1) * CHUNK)
                    s_cp = s_pair[:, side * LANES:(side + 1) * LANES] + bias_s[:, cols]
                    a_s[rows, cols] = u[rows, cols] * s_cp
        m_s[:, 0:cw] = _rms(a_s[...], vec_ref[ROW_GOUT:ROW_GOUT + 1, 0:cw]).astype(BF16)

        kt = kt_f.astype(BF16)
        for hk in range(N_KV_HEADS):
            kth = kt[hk * HEAD_DIM:(hk + 1) * HEAD_DIM, :]
            kt_s[hk * LANES:hk * LANES + HEAD_DIM, WINDOW:] = kth
            kt_s[hk * LANES + HEAD_DIM:(hk + 1) * LANES, WINDOW:] = kth
        lane_t = lax.broadcasted_iota(jnp.int32, (T, LANES), 1)
        lo_t = lane_t < HALF
        val_sw = pltpu.roll(val, HALF, axis=1)
        vd_s[0, WINDOW:, :] = jnp.where(lo_t, val, val_sw).astype(BF16)
        vd_s[1, WINDOW:, :] = jnp.where(lo_t, val_sw, val).astype(BF16)

        first_block_bias = jnp.where(j == 0, -jnp.inf, 0.0).astype(F32)
        head_of_row = lax.broadcasted_iota(jnp.int32, (Q_PER_KV * WINDOW, 1), 0) // WINDOW
        n_units = nb * N_KV_HEADS
        n_mid = 2 * (n_chunks - EARLY_CHUNKS) - 9
        late_after_scores = tuple(1 + (k * n_units) // n_mid for k in range(n_mid))
        for n, (i, hk) in enumerate((i, hk) for i in range(nb) for hk in range(N_KV_HEADS)):
            rows = slice(i * WINDOW, (i + 1) * WINDOW)
            keys = slice(i * WINDOW, (i + 2) * WINDOW)
            tiles = []
            for m in range(hk * 2, hk * 2 + 2):
                qb = q_s[rows, m * LANES:(m + 1) * LANES]
                tiles += [jnp.where(lo, qb, 0.0), jnp.where(lo, 0.0, qb)]
            qq = jnp.concatenate(tiles, axis=0).astype(BF16)
            s = _dot(qq, kt_s[hk * LANES:(hk + 1) * LANES, keys]) + tab_s[hk]
            if i == 0:
                s = jnp.concatenate([s[:, 0:WINDOW] + first_block_bias, s[:, WINDOW:]], axis=1)
            if n in late_after_scores:
                late_stage()
            sink = jnp.zeros((Q_PER_KV * WINDOW, 1), F32)
            for r in range(Q_PER_KV):
                sink = jnp.where(head_of_row == r, sink_ref[layer, hk, r], sink)
            mx = jnp.maximum(jnp.max(s, axis=-1, keepdims=True), sink)
            p_un = jnp.exp(s - mx)
            z = jnp.sum(p_un, axis=-1, keepdims=True) + jnp.exp(sink - mx)
            o = _dot(p_un.astype(BF16), vd_s[hk, keys, :]) * (1.0 / z)
            for pair in range(Q_PER_KV // 2):
                m = hk * 2 + pair
                o_even = o[(2 * pair) * WINDOW:(2 * pair + 1) * WINDOW]
                o_odd = o[(2 * pair + 1) * WINDOW:(2 * pair + 2) * WINDOW]
                bo_s[rows, m * LANES:(m + 1) * LANES] = jnp.where(lo, o_even, o_odd)

        kt_s[:, 0:WINDOW] = kt_s[:, T:T + WINDOW]
        vd_s[:, 0:WINDOW, :] = vd_s[:, T:T + WINDOW, :]

        late_stage()
        m_s[:, cw:cw + ATTN_WIDTH] = _rms(
            bo_s[...], vec_ref[ROW_GOUT:ROW_GOUT + 1, cw:cw + ATTN_WIDTH]).astype(BF16)
        o_proj = _dot(m_s[...], w_o_ref[...])
        late_stage(3)
        x1 = x_ref[0] + _rms(o_proj, vec_ref[ROW_G_PM:ROW_G_PM + 1, :d])
        x1n_s[...] = x1
        h2_s[...] = _rms(x1, vec_ref[ROW_G_PFF:ROW_G_PFF + 1, :d]).astype(BF16)
        late_stage()

        ffn_up(0)
        if with_late:
            finish_previous()
        x1_s[...] = x1n_s[...]
        ffn_down(0)
        for c in range(1, EARLY_CHUNKS):
            ffn_up(c)
            ffn_down(c)

    tile_step(True)


def _resident(shape):
    return pl.BlockSpec(shape, lambda *_: (0,) * len(shape), pipeline_mode=pl.Buffered(1))


def _layer_row(arr, layer):
    block = (1,) + arr.shape[1:]
    return pl.BlockSpec(block, lambda *_: (layer,) + (0,) * (arr.ndim - 1),
                        pipeline_mode=pl.Buffered(1))


def _prompt_call(x, layer, sinks, w_in, w_o, w_up, w_down, norms, w_sp, b_sp):
    B, S, D = x.shape
    T = PROMPT_TILE
    assert S % T == 0 and T % WINDOW == 0 and WINDOW == CHUNK
    assert 1 <= EARLY_CHUNKS and 2 * (w_up.shape[1] // FF_CHUNK - EARLY_CHUNKS) >= 9
    nj = S // T
    n_tiles = B * nj
    cur = lambda s: jnp.minimum(s, n_tiles - 1)
    prev = lambda s: jnp.maximum(s - 1, 0)
    tile_in = pl.BlockSpec((1, T, D), lambda s: (cur(s) // nj, cur(s) % nj, 0))
    tile_out = pl.BlockSpec((1, T, D), lambda s: (prev(s) // nj, prev(s) % nj, 0))
    last = lambda r, c: pl.BlockSpec((1, r, c), lambda s: (cur(s) // nj, 0, 0))
    return pl.pallas_call(
        functools.partial(_prompt_kernel, n_tiles, nj, layer),
        grid=(n_tiles + 1,),
        in_specs=[pl.BlockSpec(memory_space=pltpu.SMEM), tile_in,
                  _resident(w_in.shape), _resident(w_o.shape), _resident(w_up.shape),
                  _resident(w_down.shape)] + [_layer_row(a, layer) for a in norms] +
                 [_layer_row(w_sp, layer), _layer_row(b_sp, layer)],
        out_specs=[tile_out, last(KV_WIDTH, WINDOW), last(KV_WIDTH, WINDOW),
                   last(CHUNK, CHUNK_WIDTH)],
        out_shape=[jax.ShapeDtypeStruct((B, S, D), F32),
                   jax.ShapeDtypeStruct((B, KV_WIDTH, WINDOW), F32),
                   jax.ShapeDtypeStruct((B, KV_WIDTH, WINDOW), F32),
                   jax.ShapeDtypeStruct((B, CHUNK, CHUNK_WIDTH), F32)],
        scratch_shapes=[pltpu.VMEM((8, D), F32),
                        pltpu.VMEM((CHUNK, CHUNK_WIDTH), F32),
                        pltpu.VMEM((N_CHUNK_HEADS // 2, CHUNK, 2 * CHUNK), BF16),
                        pltpu.VMEM((N_KV_HEADS, Q_PER_KV * WINDOW, 2 * WINDOW), F32),
                        pltpu.VMEM((2 * LANES, T + WINDOW), BF16),
                        pltpu.VMEM((N_KV_HEADS, T + WINDOW, LANES), BF16),
                        pltpu.VMEM((T, CHUNK_WIDTH), F32),
                        pltpu.VMEM((T, ATTN_WIDTH), F32),
                        pltpu.VMEM((T, CHUNK_WIDTH + ATTN_WIDTH), BF16),
                        pltpu.VMEM((T, D), F32),
                        pltpu.VMEM((T, D), F32),
                        pltpu.VMEM((T, D), BF16),
                        pltpu.VMEM((T, ATTN_WIDTH), F32),
                        pltpu.VMEM((T, D), BF16),
                        pltpu.VMEM((T, FF_CHUNK), BF16),
                        pltpu.VMEM((T, D), F32)],
        compiler_params=pltpu.CompilerParams(
            dimension_semantics=("arbitrary",), vmem_limit_bytes=VMEM_LIMIT),
        name="prompt_layer",
    )(sinks, x, w_in, w_o, w_up, w_down, *norms, w_sp, b_sp)


def _sample_kernel(L, layer, sink_ref, w4_ref, b4_ref, x_ref, ckt_ref, cvt_ref, w_in_ref,
                   w_o_hbm, w_up_hbm, w_down_hbm,
                   g_pre, g_pm, g_pff, g_pf, ln_g, ln_b, g_oc, g_oa,
                   y_ref, vn_ref, wkt_ref, wvt_ref,
                   vec_ref, q_s, kvt_s, a_s, b_s, w_o_ref, w_up_ref, w_down_ref, w_sem):
    g = pl.program_id(0)
    n_steps = pl.num_programs(0)
    bb, hd, W = ckt_ref.shape
    n, d = q_s.shape[0], x_ref.shape[-1]
    n_heads = N_ATTN_HEADS
    pair_rows = 2 * L
    assert pair_rows == 8 and hd == LANES and W == LANES
    batch_per_block = LANES // L

    def late_weight_copies():
        pairs = ((w_o_hbm, w_o_ref), (w_up_hbm, w_up_ref), (w_down_hbm, w_down_ref))
        return [pltpu.make_async_copy(src, dst, w_sem.at[i]) for i, (src, dst) in enumerate(pairs)]

    @pl.when(g == 0)
    def _project_and_gate():
        cw = CHUNK_WIDTH
        for copy in late_weight_copies():
            copy.start()
        _fill_vector_table(vec_ref, g_pre, g_pm, g_pff, g_pf, ln_g, ln_b, g_oc, g_oa)
        h = _rms(x_ref[...].reshape(n, d), vec_ref[ROW_G_PRE:ROW_G_PRE + 1, :d]).astype(BF16)
        u = jax.nn.gelu(_dot(h, w_in_ref[:, 0:cw]))
        v = _layer_norm(jax.nn.gelu(_dot(h, w_in_ref[:, cw:2 * cw])),
                        vec_ref[ROW_LN:ROW_LN + 1, 0:cw], vec_ref[ROW_LN:ROW_LN + 1, cw:2 * cw])
        vn_ref[...] = v.reshape(vn_ref.shape)
        q_s[...] = _dot(h, w_in_ref[:, 2 * cw:2 * cw + ATTN_WIDTH]) * ATTN_SCALE
        kv0 = 2 * cw + ATTN_WIDTH
        kvt_s[...] = _dot(h, w_in_ref[:, kv0:kv0 + 2 * KV_WIDTH]).T
        t_of_row = lax.broadcasted_iota(jnp.int32, (n, 1), 0) % L
        group_of_lane = lax.broadcasted_iota(jnp.int32, (1, cw), 1) // HEAD_DIM

        def per_group(value_of_group):
            row = jnp.zeros((1, cw), F32)
            for grp in range(N_CHUNK_HEADS):
                row = jnp.where(group_of_lane == grp, value_of_group(grp), row)
            return row

        s = jnp.zeros(v.shape, F32)
        for t in range(L):
            s = jnp.where(t_of_row == t, per_group(lambda grp: b4_ref[layer, grp, t]), s)
        for delta in range(L):
            coef = jnp.zeros(v.shape, F32)
            for t in range(delta, L):
                w_row = per_group(lambda grp: w4_ref[grp, t, t - delta])
                coef = jnp.where(t_of_row == t, w_row, coef)
            vs = v if delta == 0 else pltpu.roll(v, delta, axis=0)
            s = s + coef * vs
        a_s[...] = u * s

    rows_all = n_heads * pair_rows
    r_idx = lax.broadcasted_iota(jnp.int32, (rows_all, LANES), 0)
    l_idx = lax.broadcasted_iota(jnp.int32, (rows_all, LANES), 1)
    t_idx = r_idx % L
    odd = (r_idx % pair_rows) >= L
    head_of_row = lax.broadcasted_iota(jnp.int32, (rows_all, 1), 0) // pair_rows
    sink = jnp.zeros((rows_all, 1), F32)
    slope = jnp.zeros((rows_all, 1), F32)
    for hh in range(n_heads):
        sink = jnp.where(head_of_row == hh, sink_ref[layer, hh // Q_PER_KV, hh % Q_PER_KV], sink)
        slope = jnp.where(head_of_row == hh, _alibi_slope(hh), slope)
    dist_old = W + t_idx - l_idx
    bias_old = jnp.where(dist_old <= WINDOW, -(slope * dist_old.astype(F32)), -jnp.inf)
    lane8 = lax.broadcasted_iota(jnp.int32, (pair_rows, LANES), 1)
    lo8 = lane8 < HALF
    lane_w = lax.broadcasted_iota(jnp.int32, (hd, W), 1)

    first_batch = g * bb
    new_block = pl.multiple_of((first_batch // batch_per_block) * LANES, LANES)
    kt_new = kvt_s[0:KV_WIDTH, pl.ds(new_block, LANES)]
    vt_new = kvt_s[KV_WIDTH:2 * KV_WIDTH, pl.ds(new_block, LANES)]
    kt_new_b = kt_new.astype(BF16)
    vt_new_b = vt_new.astype(BF16)
    n_pairs = bb // 2
    offs = [((first_batch + 2 * p) % batch_per_block) * L for p in range(n_pairs)]
    rows0 = [pl.multiple_of((first_batch + 2 * p) * L, pair_rows) for p in range(n_pairs)]
    keep = lane_w < W - L

    scores = []
    for p in range(n_pairs):
        tiles = []
        for m in range(N_PAIRS):
            qb = q_s[pl.ds(rows0[p], pair_rows), m * LANES:(m + 1) * LANES]
            qb_sw = pltpu.roll(qb, HALF, axis=1)
            if (2 * m) // Q_PER_KV == 0:
                tiles += [jnp.where(lo8, qb, 0.0), jnp.where(lo8, qb_sw, 0.0)]
            else:
                tiles += [jnp.where(lo8, 0.0, qb_sw), jnp.where(lo8, 0.0, qb)]
        lhs = jnp.concatenate(tiles, axis=0).astype(BF16)
        s_pair = []
        for e in range(2):
            rhs = jnp.concatenate([ckt_ref[2 * p + e].astype(BF16), kt_new_b], axis=1)
            s_pair.append(_dot(lhs, rhs))
        scores.append(jnp.where(jnp.concatenate([odd, odd], axis=1), s_pair[1], s_pair[0]))
        for e in range(2):
            wkt_ref[2 * p + e] = jnp.where(
                keep, pltpu.roll(ckt_ref[2 * p + e], W - L, axis=1),
                pltpu.roll(kt_new, (W - L) - (offs[p] + e * L), axis=1))

    weights = []
    for p in range(n_pairs):
        rel = l_idx - offs[p] - jnp.where(odd, L, 0)
        bias_new = jnp.where((rel >= 0) & (rel <= t_idx),
                             -(slope * (t_idx - rel).astype(F32)), -jnp.inf)
        sc = scores[p] + jnp.concatenate([bias_old, bias_new], axis=1)
        mx = jnp.maximum(jnp.max(sc, axis=-1, keepdims=True), sink)
        p_un = jnp.exp(sc - mx)
        z = jnp.sum(p_un, axis=-1, keepdims=True) + jnp.exp(sink - mx)
        weights.append((p_un.astype(BF16), z))

    for p in range(n_pairs):
        pb, z = weights[p]
        o_pair = []
        for e in range(2):
            vt = jnp.concatenate([cvt_ref[2 * p + e].astype(BF16), vt_new_b], axis=1)
            o_pair.append(lax.dot_general(pb, vt, (((1,), (1,)), ((), ())),
                                          preferred_element_type=F32))
        o = jnp.where(odd, o_pair[1], o_pair[0]) * (1.0 / z)
        for m in range(N_PAIRS):
            kv_head = (2 * m) // Q_PER_KV
            o_even = o[(2 * m) * pair_rows:(2 * m + 1) * pair_rows]
            o_odd = o[(2 * m + 1) * pair_rows:(2 * m + 2) * pair_rows]
            if kv_head == 0:
                o_odd = pltpu.roll(o_odd, HALF, axis=1)
            else:
                o_even = pltpu.roll(o_even, HALF, axis=1)
            b_s[pl.ds(rows0[p], pair_rows), m * LANES:(m + 1) * LANES] = (
                jnp.where(lo8, o_even, o_odd))
        for e in range(2):
            wvt_ref[2 * p + e] = jnp.where(
                keep, pltpu.roll(cvt_ref[2 * p + e], W - L, axis=1),
                pltpu.roll(vt_new, (W - L) - (offs[p] + e * L), axis=1))

    @pl.when(g == n_steps - 1)
    def _merge_and_ffn():
        for copy in late_weight_copies():
            copy.wait()
        x1 = _merge(x_ref[...].reshape(n, d), a_s[...], b_s[...], w_o_ref, vec_ref)
        y_ref[...] = _ffn(x1, w_up_ref, w_down_ref, vec_ref).reshape(y_ref.shape)


def _sample_call(x, layer, sinks, w4, b_sp, cache_kt, cache_vt, w_in, w_o, w_up, w_down,
                 norms):
    Bd, L, D = x.shape
    hd, W = cache_kt.shape[1:]
    n = Bd * L
    bb = SAMPLE_BB
    assert Bd % bb == 0 and bb % 2 == 0 and (LANES // L) % bb == 0
    smem = pl.BlockSpec(memory_space=pltpu.SMEM)
    blk = pl.BlockSpec((bb, hd, W), lambda g: (g, 0, 0))
    const = lambda shape: pl.BlockSpec(shape, lambda g: (0,) * len(shape))
    late_weights = (w_o, w_up, w_down)
    in_hbm = pl.BlockSpec(memory_space=pl.ANY)
    return pl.pallas_call(
        functools.partial(_sample_kernel, L, layer),
        grid=(Bd // bb,),
        in_specs=[smem, smem, smem, _resident(x.shape), blk, blk, _resident(w_in.shape)] +
                 [in_hbm for _ in late_weights] + [_layer_row(a, layer) for a in norms],
        out_specs=[const((Bd, L, D)), const((Bd, L, CHUNK_WIDTH)), blk, blk],
        out_shape=[jax.ShapeDtypeStruct((Bd, L, D), F32),
                   jax.ShapeDtypeStruct((Bd, L, CHUNK_WIDTH), F32),
                   jax.ShapeDtypeStruct(cache_kt.shape, F32),
                   jax.ShapeDtypeStruct(cache_vt.shape, F32)],
        scratch_shapes=[pltpu.VMEM((8, D), F32),
                        pltpu.VMEM((n, ATTN_WIDTH), F32),
                        pltpu.VMEM((2 * KV_WIDTH, n), F32),
                        pltpu.VMEM((n, CHUNK_WIDTH), F32),
                        pltpu.VMEM((n, ATTN_WIDTH), F32)] +
                       [pltpu.VMEM(w.shape, w.dtype) for w in late_weights] +
                       [pltpu.SemaphoreType.DMA((len(late_weights),))],
        compiler_params=pltpu.CompilerParams(
            dimension_semantics=("arbitrary",), vmem_limit_bytes=VMEM_LIMIT),
        name="sample_layer",
    )(sinks, w4, b_sp, x, cache_kt, cache_vt, w_in, *late_weights, *norms)


def _heads_last(t):
    b, _, w = t.shape
    return t.reshape(b, N_KV_HEADS, HEAD_DIM, w).transpose(0, 3, 1, 2)


def kernel(x_prompt, x_sample, cache_win_k, cache_win_v, w_in, g_pre_mix, ln_v_g, ln_v_b,
           w_spatial, b_spatial, attn_sinks, g_out_chunk, g_out_attn, w_o, g_post_mix,
           g_pre_ffn, w_up, w_down, g_post_ffn):
    depth = w_in.shape[0]
    Bd, L, D = x_sample.shape
    W = cache_win_k.shape[2]
    norms = (g_pre_mix, g_post_mix, g_pre_ffn, g_post_ffn, ln_v_g, ln_v_b, g_out_chunk,
             g_out_attn)
    yp, ys = x_prompt, x_sample
    wk_p, wv_p, cv_p, wk_s, wv_s, cv_s = [], [], [], [], [], []
    for l in range(depth):
        w_in_b = w_in[l].astype(BF16)
        w_o_b = w_o[l].astype(BF16)
        w_up_b = w_up[l].astype(BF16)
        w_down_b = w_down[l].astype(BF16)

        yp, wk, wv, cv = _prompt_call(yp, l, attn_sinks, w_in_b, w_o_b, w_up_b, w_down_b, norms,
                                      w_spatial, b_spatial)
        wk_p.append(_heads_last(wk))
        wv_p.append(_heads_last(wv))
        cv_p.append(cv)

        w4 = w_spatial[l, :, :L, :L]
        cache_kt = cache_win_k[l].transpose(0, 2, 3, 1).reshape(Bd, KV_WIDTH, W)
        cache_vt = cache_win_v[l].transpose(0, 2, 3, 1).reshape(Bd, KV_WIDTH, W)
        ys, cv, wk, wv = _sample_call(ys, l, attn_sinks, w4, b_spatial, cache_kt, cache_vt,
                                      w_in_b, w_o_b, w_up_b, w_down_b, norms)
        wk_s.append(_heads_last(wk))
        wv_s.append(_heads_last(wv))
        cv_s.append(cv)
    return (yp, ys, jnp.stack(wk_p), jnp.stack(wv_p), jnp.stack(cv_p),
            jnp.stack(wk_s), jnp.stack(wv_s), jnp.stack(cv_s))
```

```python
import functools

import jax
import jax.numpy as jnp
from jax import lax
from jax.experimental import pallas as pl
from jax.experimental.pallas import tpu as pltpu

F32 = jnp.float32
BF16 = jnp.bfloat16

HEAD_DIM = 64
N_CHUNK_HEADS = 8
N_ATTN_HEADS = 8
N_KV_HEADS = 2
Q_PER_KV = N_ATTN_HEADS // N_KV_HEADS
CHUNK_WIDTH = N_CHUNK_HEADS * HEAD_DIM
ATTN_WIDTH = N_ATTN_HEADS * HEAD_DIM
KV_WIDTH = N_KV_HEADS * HEAD_DIM
CHUNK = 128
WINDOW = 128
EPS = 1e-6
ATTN_SCALE = HEAD_DIM ** -0.5

LANES = 128
HALF = LANES // 2
N_PAIRS = ATTN_WIDTH // LANES
VMEM_LIMIT = 56 * 1024 * 1024

PROMPT_TILE = 512
FF_CHUNK = 512
EARLY_CHUNKS = 1
SAMPLE_BB = 16

ROW_G_PRE, ROW_G_PM, ROW_G_PFF, ROW_G_PF, ROW_LN, ROW_GOUT = range(6)


def _dot(a, b):
    return jnp.dot(a, b, preferred_element_type=F32)


def _rms(x, g):
    return x * lax.rsqrt(jnp.mean(x * x, axis=-1, keepdims=True) + EPS) * g


def _layer_norm(x, g, b):
    xc = x - jnp.mean(x, axis=-1, keepdims=True)
    return xc * lax.rsqrt(jnp.mean(xc * xc, axis=-1, keepdims=True) + EPS) * g + b


def _alibi_slope(head):
    return 2.0 ** (-8.0 * (head + 1) / N_ATTN_HEADS)


def _merge(x, a_out, b_out, w_o_ref, vec_ref):
    d = x.shape[-1]
    cw = CHUNK_WIDTH
    ra = _rms(a_out, vec_ref[ROW_GOUT:ROW_GOUT + 1, 0:cw]).astype(BF16)
    rb = _rms(b_out, vec_ref[ROW_GOUT:ROW_GOUT + 1, cw:cw + ATTN_WIDTH]).astype(BF16)
    o = _dot(ra, w_o_ref[0:cw, :]) + _dot(rb, w_o_ref[cw:cw + ATTN_WIDTH, :])
    return x + _rms(o, vec_ref[ROW_G_PM:ROW_G_PM + 1, :d])


def _relu_sq(up):
    r = jnp.maximum(up, 0.0)
    return (r * r).astype(BF16)


def _ffn(x1, w_up_ref, w_down_ref, vec_ref):
    d = x1.shape[-1]
    h2 = _rms(x1, vec_ref[ROW_G_PFF:ROW_G_PFF + 1, :d]).astype(BF16)
    d_ff = w_up_ref.shape[1]
    f = jnp.zeros(x1.shape, F32)
    for c in range(d_ff // FF_CHUNK):
        up = _dot(h2, w_up_ref[:, c * FF_CHUNK:(c + 1) * FF_CHUNK])
        f = f + _dot(_relu_sq(up), w_down_ref[c * FF_CHUNK:(c + 1) * FF_CHUNK, :])
    return x1 + _rms(f, vec_ref[ROW_G_PF:ROW_G_PF + 1, :d])


def _fill_vector_table(vec_s, g_pre, g_pm, g_pff, g_pf, ln_g, ln_b, g_oc, g_oa):
    d = g_pre.shape[-1]
    cw = CHUNK_WIDTH
    vec_s[ROW_G_PRE:ROW_G_PRE + 1, 0:d] = g_pre[...]
    vec_s[ROW_G_PM:ROW_G_PM + 1, 0:d] = g_pm[...]
    vec_s[ROW_G_PFF:ROW_G_PFF + 1, 0:d] = g_pff[...]
    vec_s[ROW_G_PF:ROW_G_PF + 1, 0:d] = g_pf[...]
    vec_s[ROW_LN:ROW_LN + 1, 0:cw] = ln_g[...]
    vec_s[ROW_LN:ROW_LN + 1, cw:2 * cw] = ln_b[...]
    vec_s[ROW_GOUT:ROW_GOUT + 1, 0:cw] = g_oc[...]
    vec_s[ROW_GOUT:ROW_GOUT + 1, cw:cw + ATTN_WIDTH] = g_oa[...]


def _prompt_kernel(n_tiles, nj, layer, sink_ref, x_ref, w_in_ref, w_o_ref, w_up_ref, w_down_ref,
                   g_pre, g_pm, g_pff, g_pf, ln_g, ln_b, g_oc, g_oa, wsp_ref, bsp_ref,
                   y_ref, wk_ref, wv_ref, cv_ref,
                   vec_ref, bias_s, wcat_s, tab_s, kt_s, vd_s, a_s, bo_s, m_s, x1_s, x1n_s,
                   h_s, q_s, h2_s, act_s, f_s):
    step = pl.program_id(0)
    j = jnp.minimum(step, n_tiles - 1) % nj
    T, d = x_ref.shape[1], x_ref.shape[2]
    nb = T // WINDOW
    cw = CHUNK_WIDTH
    n_chunks = w_up_ref.shape[1] // FF_CHUNK

    lane = lax.broadcasted_iota(jnp.int32, (WINDOW, LANES), 1)
    lo = lane < HALF

    @pl.when(step == 0)
    def _init_tables():
        x1_s[...] = jnp.zeros(x1_s.shape, F32)
        h2_s[...] = jnp.zeros(h2_s.shape, BF16)
        f_s[...] = jnp.zeros(f_s.shape, F32)
        _fill_vector_table(vec_ref, g_pre, g_pm, g_pff, g_pf, ln_g, ln_b, g_oc, g_oa)
        t_idx = lax.broadcasted_iota(jnp.int32, (CHUNK, CHUNK), 0)
        s_idx = lax.broadcasted_iota(jnp.int32, (CHUNK, CHUNK), 1)
        for g in range(N_CHUNK_HEADS):
            wm = jnp.where(s_idx <= t_idx, wsp_ref[0, g], 0.0).astype(BF16)
            wcat_s[g // 2, :, (g % 2) * CHUNK:(g % 2 + 1) * CHUNK] = wm
        b_rows = jnp.concatenate(
            [bsp_ref[0], jnp.zeros((CHUNK - N_CHUNK_HEADS, CHUNK), F32)], axis=0)
        b_cols = b_rows.T
        for p in range(cw // LANES):
            bias_s[:, p * LANES:(p + 1) * LANES] = jnp.where(
                lo, jnp.broadcast_to(b_cols[:, 2 * p:2 * p + 1], (CHUNK, LANES)),
                jnp.broadcast_to(b_cols[:, 2 * p + 1:2 * p + 2], (CHUNK, LANES)))
        a_idx = lax.broadcasted_iota(jnp.int32, (WINDOW, 2 * WINDOW), 0)
        c_idx = lax.broadcasted_iota(jnp.int32, (WINDOW, 2 * WINDOW), 1)
        dist = WINDOW + a_idx - c_idx
        valid = (dist >= 0) & (dist <= WINDOW)
        for hd in range(N_ATTN_HEADS):
            bias = jnp.where(valid, -(_alibi_slope(hd) * dist.astype(F32)), -jnp.inf)
            r = hd % Q_PER_KV
            tab_s[hd // Q_PER_KV, r * WINDOW:(r + 1) * WINDOW, :] = bias

    @pl.when(j == 0)
    def _clear_carry():
        kt_s[:, 0:WINDOW] = jnp.zeros((2 * LANES, WINDOW), BF16)
        vd_s[:, 0:WINDOW, :] = jnp.zeros((N_KV_HEADS, WINDOW, LANES), BF16)

    def ffn_up(c):
        act_s[...] = _relu_sq(_dot(h2_s[...], w_up_ref[:, c * FF_CHUNK:(c + 1) * FF_CHUNK]))

    def ffn_down(c):
        part = _dot(act_s[...], w_down_ref[c * FF_CHUNK:(c + 1) * FF_CHUNK, :])
        if c == 0:
            f_s[...] = part
        else:
            f_s[...] += part

    def finish_previous():
        y_ref[0] = x1_s[...] + _rms(f_s[...], vec_ref[ROW_G_PF:ROW_G_PF + 1, :d])

    def tile_step(with_late):
        late = iter([functools.partial(fn, c) for c in range(EARLY_CHUNKS, n_chunks)
                     for fn in (ffn_up, ffn_down)])

        def late_stage(count=1):
            if with_late:
                for _ in range(count):
                    next(late)()

        late_stage()
        h_s[...] = _rms(x_ref[0], vec_ref[ROW_G_PRE:ROW_G_PRE + 1, :d]).astype(BF16)
        u_pre = _dot(h_s[...], w_in_ref[:, 0:cw])
        v_pre = _dot(h_s[...], w_in_ref[:, cw:2 * cw])
        late_stage()
        u = jax.nn.gelu(u_pre)
        late_stage()
        v = _layer_norm(jax.nn.gelu(v_pre), vec_ref[ROW_LN:ROW_LN + 1, 0:cw],
                        vec_ref[ROW_LN:ROW_LN + 1, cw:2 * cw])
        kv0 = 2 * cw + ATTN_WIDTH
        kv = _dot(h_s[...], w_in_ref[:, kv0:kv0 + 2 * KV_WIDTH])
        k = kv[:, 0:KV_WIDTH]
        val = kv[:, KV_WIDTH:2 * KV_WIDTH]
        q0 = 2 * cw
        q_s[...] = _dot(h_s[...], w_in_ref[:, q0:q0 + ATTN_WIDTH]) * ATTN_SCALE
        late_stage()

        kt_f = k.T
        wk_ref[0] = kt_f[:, T - WINDOW:]
        wv_ref[0] = val[T - WINDOW:, :].T
        cv_ref[0] = v[T - CHUNK:, :]

        assert nb % 2 == 0
        for c in range(0, nb, 2):
            for p in range(cw // LANES):
                cols = slice(p * LANES, (p + 1) * LANES)
                halves = []
                for cc in (c, c + 1):
                    vcp = v[cc * CHUNK:(cc + 1) * CHUNK, cols]
                    halves.append(jnp.concatenate([jnp.where(lo, vcp, 0.0), jnp.where(lo, 0.0, vcp)],
                                                  axis=0))
                rhs = jnp.concatenate(halves, axis=1).astype(BF16)
                s_pair = _dot(wcat_s[p], rhs)
                for side, cc in enumerate((c, c + 1)):
                    rows = slice(cc * CHUNK, (cc + 1) * CHUNK)
                    s_cp = s_pair[:, side * LANES:(side + 1) * LANES] + bias_s[:, cols]
                    a_s[rows, cols] = u[rows, cols] * s_cp
        m_s[:, 0:cw] = _rms(a_s[...], vec_ref[ROW_GOUT:ROW_GOUT + 1, 0:cw]).astype(BF16)

        kt = kt_f.astype(BF16)
        for hk in range(N_KV_HEADS):
            kth = kt[hk * HEAD_DIM:(hk + 1) * HEAD_DIM, :]
            kt_s[hk * LANES:hk * LANES + HEAD_DIM, WINDOW:] = kth
            kt_s[hk * LANES + HEAD_DIM:(hk + 1) * LANES, WINDOW:] = kth
        lane_t = lax.broadcasted_iota(jnp.int32, (T, LANES), 1)
        lo_t = lane_t < HALF
        val_sw = pltpu.roll(val, HALF, axis=1)
        vd_s[0, WINDOW:, :] = jnp.where(lo_t, val, val_sw).astype(BF16)
        vd_s[1, WINDOW:, :] = jnp.where(lo_t, val_sw, val).astype(BF16)

        first_block_bias = jnp.where(j == 0, -jnp.inf, 0.0).astype(F32)
        head_of_row = lax.broadcasted_iota(jnp.int32, (Q_PER_KV * WINDOW, 1), 0) // WINDOW
        n_units = nb * N_KV_HEADS
        n_mid = 2 * (n_chunks - EARLY_CHUNKS) - 9
        late_after_scores = tuple(1 + (k * n_units) // n_mid for k in range(n_mid))
        for n, (i, hk) in enumerate((i, hk) for i in range(nb) for hk in range(N_KV_HEADS)):
            rows = slice(i * WINDOW, (i + 1) * WINDOW)
            keys = slice(i * WINDOW, (i + 2) * WINDOW)
            tiles = []
            for m in range(hk * 2, hk * 2 + 2):
                qb = q_s[rows, m * LANES:(m + 1) * LANES]
                tiles += [jnp.where(lo, qb, 0.0), jnp.where(lo, 0.0, qb)]
            qq = jnp.concatenate(tiles, axis=0).astype(BF16)
            s = _dot(qq, kt_s[hk * LANES:(hk + 1) * LANES, keys]) + tab_s[hk]
            if i == 0:
                s = jnp.concatenate([s[:, 0:WINDOW] + first_block_bias, s[:, WINDOW:]], axis=1)
            if n in late_after_scores:
                late_stage()
            sink = jnp.zeros((Q_PER_KV * WINDOW, 1), F32)
            for r in range(Q_PER_KV):
                sink = jnp.where(head_of_row == r, sink_ref[layer, hk, r], sink)
            mx = jnp.maximum(jnp.max(s, axis=-1, keepdims=True), sink)
            p_un = jnp.exp(s - mx)
            z = jnp.sum(p_un, axis=-1, keepdims=True) + jnp.exp(sink - mx)
            o = _dot(p_un.astype(BF16), vd_s[hk, keys, :]) * (1.0 / z)
            for pair in range(Q_PER_KV // 2):
                m = hk * 2 + pair
                o_even = o[(2 * pair) * WINDOW:(2 * pair + 1) * WINDOW]
                o_odd = o[(2 * pair + 1) * WINDOW:(2 * pair + 2) * WINDOW]
                bo_s[rows, m * LANES:(m + 1) * LANES] = jnp.where(lo, o_even, o_odd)

        kt_s[:, 0:WINDOW] = kt_s[:, T:T + WINDOW]
        vd_s[:, 0:WINDOW, :] = vd_s[:, T:T + WINDOW, :]

        late_stage()
        m_s[:, cw:cw + ATTN_WIDTH] = _rms(
            bo_s[...], vec_ref[ROW_GOUT:ROW_GOUT + 1, cw:cw + ATTN_WIDTH]).astype(BF16)
        o_proj = _dot(m_s[...], w_o_ref[...])
        late_stage(3)
        x1 = x_ref[0] + _rms(o_proj, vec_ref[ROW_G_PM:ROW_G_PM + 1, :d])
        x1n_s[...] = x1
        h2_s[...] = _rms(x1, vec_ref[ROW_G_PFF:ROW_G_PFF + 1, :d]).astype(BF16)
        late_stage()

        ffn_up(0)
        if with_late:
            finish_previous()
        x1_s[...] = x1n_s[...]
        ffn_down(0)
        for c in range(1, EARLY_CHUNKS):
            ffn_up(c)
            ffn_down(c)

    @pl.when(step < n_tiles)
    def _tile_step():
        tile_step(True)

    @pl.when(step == n_tiles)
    def _drain_step():
        for c in range(EARLY_CHUNKS, n_chunks):
            ffn_up(c)
            ffn_down(c)
        finish_previous()


def _resident(shape):
    return pl.BlockSpec(shape, lambda *_: (0,) * len(shape), pipeline_mode=pl.Buffered(1))


def _layer_row(arr, layer):
    block = (1,) + arr.shape[1:]
    return pl.BlockSpec(block, lambda *_: (layer,) + (0,) * (arr.ndim - 1),
                        pipeline_mode=pl.Buffered(1))


def _prompt_call(x, layer, sinks, w_in, w_o, w_up, w_down, norms, w_sp, b_sp):
    B, S, D = x.shape
    T = PROMPT_TILE
    assert S % T == 0 and T % WINDOW == 0 and WINDOW == CHUNK
    assert 1 <= EARLY_CHUNKS and 2 * (w_up.shape[1] // FF_CHUNK - EARLY_CHUNKS) >= 9
    nj = S // T
    n_tiles = B * nj
    cur = lambda s: jnp.minimum(s, n_tiles - 1)
    prev = lambda s: jnp.maximum(s - 1, 0)
    tile_in = pl.BlockSpec((1, T, D), lambda s: (cur(s) // nj, cur(s) % nj, 0))
    tile_out = pl.BlockSpec((1, T, D), lambda s: (prev(s) // nj, prev(s) % nj, 0))
    last = lambda r, c: pl.BlockSpec((1, r, c), lambda s: (cur(s) // nj, 0, 0))
    return pl.pallas_call(
        functools.partial(_prompt_kernel, n_tiles, nj, layer),
        grid=(n_tiles + 1,),
        in_specs=[pl.BlockSpec(memory_space=pltpu.SMEM), tile_in,
                  _resident(w_in.shape), _resident(w_o.shape), _resident(w_up.shape),
                  _resident(w_down.shape)] + [_layer_row(a, layer) for a in norms] +
                 [_layer_row(w_sp, layer), _layer_row(b_sp, layer)],
        out_specs=[tile_out, last(KV_WIDTH, WINDOW), last(KV_WIDTH, WINDOW),
                   last(CHUNK, CHUNK_WIDTH)],
        out_shape=[jax.ShapeDtypeStruct((B, S, D), F32),
                   jax.ShapeDtypeStruct((B, KV_WIDTH, WINDOW), F32),
                   jax.ShapeDtypeStruct((B, KV_WIDTH, WINDOW), F32),
                   jax.ShapeDtypeStruct((B, CHUNK, CHUNK_WIDTH), F32)],
        scratch_shapes=[pltpu.VMEM((8, D), F32),
                        pltpu.VMEM((CHUNK, CHUNK_WIDTH), F32),
                        pltpu.VMEM((N_CHUNK_HEADS // 2, CHUNK, 2 * CHUNK), BF16),
                        pltpu.VMEM((N_KV_HEADS, Q_PER_KV * WINDOW, 2 * WINDOW), F32),
                        pltpu.VMEM((2 * LANES, T + WINDOW), BF16),
                        pltpu.VMEM((N_KV_HEADS, T + WINDOW, LANES), BF16),
                        pltpu.VMEM((T, CHUNK_WIDTH), F32),
                        pltpu.VMEM((T, ATTN_WIDTH), F32),
                        pltpu.VMEM((T, CHUNK_WIDTH + ATTN_WIDTH), BF16),
                        pltpu.VMEM((T, D), F32),
                        pltpu.VMEM((T, D), F32),
                        pltpu.VMEM((T, D), BF16),
                        pltpu.VMEM((T, ATTN_WIDTH), F32),
                        pltpu.VMEM((T, D), BF16),
                        pltpu.VMEM((T, FF_CHUNK), BF16),
                        pltpu.VMEM((T, D), F32)],
        compiler_params=pltpu.CompilerParams(
            dimension_semantics=("arbitrary",), vmem_limit_bytes=VMEM_LIMIT),
        name="prompt_layer",
    )(sinks, x, w_in, w_o, w_up, w_down, *norms, w_sp, b_sp)


def _sample_kernel(L, layer, sink_ref, w4_ref, b4_ref, x_ref, ckt_ref, cvt_ref, w_in_ref,
                   w_o_hbm, w_up_hbm, w_down_hbm,
                   g_pre, g_pm, g_pff, g_pf, ln_g, ln_b, g_oc, g_oa,
                   y_ref, vn_ref, wkt_ref, wvt_ref,
                   vec_ref, q_s, kvt_s, a_s, b_s, w_o_ref, w_up_ref, w_down_ref, w_sem):
    g = pl.program_id(0)
    n_steps = pl.num_programs(0)
    bb, hd, W = ckt_ref.shape
    n, d = q_s.shape[0], x_ref.shape[-1]
    n_heads = N_ATTN_HEADS
    pair_rows = 2 * L
    assert pair_rows == 8 and hd == LANES and W == LANES
    batch_per_block = LANES // L

    def late_weight_copies():
        pairs = ((w_o_hbm, w_o_ref), (w_up_hbm, w_up_ref), (w_down_hbm, w_down_ref))
        return [pltpu.make_async_copy(src, dst, w_sem.at[i]) for i, (src, dst) in enumerate(pairs)]

    @pl.when(g == 0)
    def _project_and_gate():
        cw = CHUNK_WIDTH
        for copy in late_weight_copies():
            copy.start()
        _fill_vector_table(vec_ref, g_pre, g_pm, g_pff, g_pf, ln_g, ln_b, g_oc, g_oa)
        h = _rms(x_ref[...].reshape(n, d), vec_ref[ROW_G_PRE:ROW_G_PRE + 1, :d]).astype(BF16)
        u = jax.nn.gelu(_dot(h, w_in_ref[:, 0:cw]))
        v = _layer_norm(jax.nn.gelu(_dot(h, w_in_ref[:, cw:2 * cw])),
                        vec_ref[ROW_LN:ROW_LN + 1, 0:cw], vec_ref[ROW_LN:ROW_LN + 1, cw:2 * cw])
        vn_ref[...] = v.reshape(vn_ref.shape)
        q_s[...] = _dot(h, w_in_ref[:, 2 * cw:2 * cw + ATTN_WIDTH]) * ATTN_SCALE
        kv0 = 2 * cw + ATTN_WIDTH
        kvt_s[...] = _dot(h, w_in_ref[:, kv0:kv0 + 2 * KV_WIDTH]).T
        t_of_row = lax.broadcasted_iota(jnp.int32, (n, 1), 0) % L
        group_of_lane = lax.broadcasted_iota(jnp.int32, (1, cw), 1) // HEAD_DIM

        def per_group(value_of_group):
            row = jnp.zeros((1, cw), F32)
            for grp in range(N_CHUNK_HEADS):
                row = jnp.where(group_of_lane == grp, value_of_group(grp), row)
            return row

        s = jnp.zeros(v.shape, F32)
        for t in range(L):
            s = jnp.where(t_of_row == t, per_group(lambda grp: b4_ref[layer, grp, t]), s)
        for delta in range(L):
            coef = jnp.zeros(v.shape, F32)
            for t in range(delta, L):
                w_row = per_group(lambda grp: w4_ref[grp, t, t - delta])
                coef = jnp.where(t_of_row == t, w_row, coef)
            vs = v if delta == 0 else pltpu.roll(v, delta, axis=0)
            s = s + coef * vs
        a_s[...] = u * s

    rows_all = n_heads * pair_rows
    r_idx = lax.broadcasted_iota(jnp.int32, (rows_all, LANES), 0)
    l_idx = lax.broadcasted_iota(jnp.int32, (rows_all, LANES), 1)
    t_idx = r_idx % L
    odd = (r_idx % pair_rows) >= L
    head_of_row = lax.broadcasted_iota(jnp.int32, (rows_all, 1), 0) // pair_rows
    sink = jnp.zeros((rows_all, 1), F32)
    slope = jnp.zeros((rows_all, 1), F32)
    for hh in range(n_heads):
        sink = jnp.where(head_of_row == hh, sink_ref[layer, hh // Q_PER_KV, hh % Q_PER_KV], sink)
        slope = jnp.where(head_of_row == hh, _alibi_slope(hh), slope)
    dist_old = W + t_idx - l_idx
    bias_old = jnp.where(dist_old <= WINDOW, -(slope * dist_old.astype(F32)), -jnp.inf)
    lane8 = lax.broadcasted_iota(jnp.int32, (pair_rows, LANES), 1)
    lo8 = lane8 < HALF
    lane_w = lax.broadcasted_iota(jnp.int32, (hd, W), 1)

    first_batch = g * bb
    new_block = pl.multiple_of((first_batch // batch_per_block) * LANES, LANES)
    kt_new = kvt_s[0:KV_WIDTH, pl.ds(new_block, LANES)]
    vt_new = kvt_s[KV_WIDTH:2 * KV_WIDTH, pl.ds(new_block, LANES)]
    kt_new_b = kt_new.astype(BF16)
    vt_new_b = vt_new.astype(BF16)
    n_pairs = bb // 2
    offs = [((first_batch + 2 * p) % batch_per_block) * L for p in range(n_pairs)]
    rows0 = [pl.multiple_of((first_batch + 2 * p) * L, pair_rows) for p in range(n_pairs)]
    keep = lane_w < W - L

    scores = []
    for p in range(n_pairs):
        tiles = []
        for m in range(N_PAIRS):
            qb = q_s[pl.ds(rows0[p], pair_rows), m * LANES:(m + 1) * LANES]
            qb_sw = pltpu.roll(qb, HALF, axis=1)
            if (2 * m) // Q_PER_KV == 0:
                tiles += [jnp.where(lo8, qb, 0.0), jnp.where(lo8, qb_sw, 0.0)]
            else:
                tiles += [jnp.where(lo8, 0.0, qb_sw), jnp.where(lo8, 0.0, qb)]
        lhs = jnp.concatenate(tiles, axis=0).astype(BF16)
        s_pair = []
        for e in range(2):
            rhs = jnp.concatenate([ckt_ref[2 * p + e].astype(BF16), kt_new_b], axis=1)
            s_pair.append(_dot(lhs, rhs))
        scores.append(jnp.where(jnp.concatenate([odd, odd], axis=1), s_pair[1], s_pair[0]))
        for e in range(2):
            wkt_ref[2 * p + e] = jnp.where(
                keep, pltpu.roll(ckt_ref[2 * p + e], W - L, axis=1),
                pltpu.roll(kt_new, (W - L) - (offs[p] + e * L), axis=1))

    weights = []
    for p in range(n_pairs):
        rel = l_idx - offs[p] - jnp.where(odd, L, 0)
        bias_new = jnp.where((rel >= 0) & (rel <= t_idx),
                             -(slope * (t_idx - rel).astype(F32)), -jnp.inf)
        sc = scores[p] + jnp.concatenate([bias_old, bias_new], axis=1)
        mx = jnp.maximum(jnp.max(sc, axis=-1, keepdims=True), sink)
        p_un = jnp.exp(sc - mx)
        z = jnp.sum(p_un, axis=-1, keepdims=True) + jnp.exp(sink - mx)
        weights.append((p_un.astype(BF16), z))

    for p in range(n_pairs):
        pb, z = weights[p]
        o_pair = []
        for e in range(2):
            vt = jnp.concatenate([cvt_ref[2 * p + e].astype(BF16), vt_new_b], axis=1)
            o_pair.append(lax.dot_general(pb, vt, (((1,), (1,)), ((), ())),
                                          preferred_element_type=F32))
        o = jnp.where(odd, o_pair[1], o_pair[0]) * (1.0 / z)
        for m in range(N_PAIRS):
            kv_head = (2 * m) // Q_PER_KV
            o_even = o[(2 * m) * pair_rows:(2 * m + 1) * pair_rows]
            o_odd = o[(2 * m + 1) * pair_rows:(2 * m + 2) * pair_rows]
            if kv_head == 0:
                o_odd = pltpu.roll(o_odd, HALF, axis=1)
            else:
                o_even = pltpu.roll(o_even, HALF, axis=1)
            b_s[pl.ds(rows0[p], pair_rows), m * LANES:(m + 1) * LANES] = (
                jnp.where(lo8, o_even, o_odd))
        for e in range(2):
            wvt_ref[2 * p + e] = jnp.where(
                keep, pltpu.roll(cvt_ref[2 * p + e], W - L, axis=1),
                pltpu.roll(vt_new, (W - L) - (offs[p] + e * L), axis=1))

    @pl.when(g == n_steps - 1)
    def _merge_and_ffn():
        for copy in late_weight_copies():
            copy.wait()
        x1 = _merge(x_ref[...].reshape(n, d), a_s[...], b_s[...], w_o_ref, vec_ref)
        y_ref[...] = _ffn(x1, w_up_ref, w_down_ref, vec_ref).reshape(y_ref.shape)


def _sample_call(x, layer, sinks, w4, b_sp, cache_kt, cache_vt, w_in, w_o, w_up, w_down,
                 norms):
    Bd, L, D = x.shape
    hd, W = cache_kt.shape[1:]
    n = Bd * L
    bb = SAMPLE_BB
    assert Bd % bb == 0 and bb % 2 == 0 and (LANES // L) % bb == 0
    smem = pl.BlockSpec(memory_space=pltpu.SMEM)
    blk = pl.BlockSpec((bb, hd, W), lambda g: (g, 0, 0))
    const = lambda shape: pl.BlockSpec(shape, lambda g: (0,) * len(shape))
    late_weights = (w_o, w_up, w_down)
    in_hbm = pl.BlockSpec(memory_space=pl.ANY)
    return pl.pallas_call(
        functools.partial(_sample_kernel, L, layer),
        grid=(Bd // bb,),
        in_specs=[smem, smem, smem, _resident(x.shape), blk, blk, _resident(w_in.shape)] +
                 [in_hbm for _ in late_weights] + [_layer_row(a, layer) for a in norms],
        out_specs=[const((Bd, L, D)), const((Bd, L, CHUNK_WIDTH)), blk, blk],
        out_shape=[jax.ShapeDtypeStruct((Bd, L, D), F32),
                   jax.ShapeDtypeStruct((Bd, L, CHUNK_WIDTH), F32),
                   jax.ShapeDtypeStruct(cache_kt.shape, F32),
                   jax.ShapeDtypeStruct(cache_vt.shape, F32)],
        scratch_shapes=[pltpu.VMEM((8, D), F32),
                        pltpu.VMEM((n, ATTN_WIDTH), F32),
                        pltpu.VMEM((2 * KV_WIDTH, n), F32),
                        pltpu.VMEM((n, CHUNK_WIDTH), F32),
                        pltpu.VMEM((n, ATTN_WIDTH), F32)] +
                       [pltpu.VMEM(w.shape, w.dtype) for w in late_weights] +
                       [pltpu.SemaphoreType.DMA((len(late_weights),))],
        compiler_params=pltpu.CompilerParams(
            dimension_semantics=("arbitrary",), vmem_limit_bytes=VMEM_LIMIT),
        name="sample_layer",
    )(sinks, w4, b_sp, x, cache_kt, cache_vt, w_in, *late_weights, *norms)


def _heads_last(t):
    b, _, w = t.shape
    return t.reshape(b, N_KV_HEADS, HEAD_DIM, w).transpose(0, 3, 1, 2)


def kernel(x_prompt, x_sample, cache_win_k, cache_win_v, w_in, g_pre_mix, ln_v_g, ln_v_b,
           w_spatial, b_spatial, attn_sinks, g_out_chunk, g_out_attn, w_o, g_post_mix,
           g_pre_ffn, w_up, w_down, g_post_ffn):
    depth = w_in.shape[0]
    Bd, L, D = x_sample.shape
    W = cache_win_k.shape[2]
    norms = (g_pre_mix, g_post_mix, g_pre_ffn, g_post_ffn, ln_v_g, ln_v_b, g_out_chunk,
             g_out_attn)
    yp, ys = x_prompt, x_sample
    wk_p, wv_p, cv_p, wk_s, wv_s, cv_s = [], [], [], [], [], []
    for l in range(depth):
        w_in_b = w_in[l].astype(BF16)
        w_o_b = w_o[l].astype(BF16)
        w_up_b = w_up[l].astype(BF16)
        w_down_b = w_down[l].astype(BF16)

        yp, wk, wv, cv = _prompt_call(yp, l, attn_sinks, w_in_b, w_o_b, w_up_b, w_down_b, norms,
                                      w_spatial, b_spatial)
        wk_p.append(_heads_last(wk))
        wv_p.append(_heads_last(wv))
        cv_p.append(cv)

        w4 = w_spatial[l, :, :L, :L]
        cache_kt = cache_win_k[l].transpose(0, 2, 3, 1).reshape(Bd, KV_WIDTH, W)
        cache_vt = cache_win_v[l].transpose(0, 2, 3, 1).reshape(Bd, KV_WIDTH, W)
        ys, cv, wk, wv = _sample_call(ys, l, attn_sinks, w4, b_spatial, cache_kt, cache_vt,
                                      w_in_b, w_o_b, w_up_b, w_down_b, norms)
        wk_s.append(_heads_last(wk))
        wv_s.append(_heads_last(wv))
        cv_s.append(cv)
    return (yp, ys, jnp.stack(wk_p), jnp.stack(wv_p), jnp.stack(cv_p),
            jnp.stack(wk_s), jnp.stack(wv_s), jnp.stack(cv_s))
```

```python
import functools

import jax
import jax.numpy as jnp
from jax import lax
from jax.experimental import pallas as pl
from jax.experimental.pallas import tpu as pltpu

F32 = jnp.float32
BF16 = jnp.bfloat16

HEAD_DIM = 64
N_CHUNK_HEADS = 8
N_ATTN_HEADS = 8
N_KV_HEADS = 2
Q_PER_KV = N_ATTN_HEADS // N_KV_HEADS
CHUNK_WIDTH = N_CHUNK_HEADS * HEAD_DIM
ATTN_WIDTH = N_ATTN_HEADS * HEAD_DIM
KV_WIDTH = N_KV_HEADS * HEAD_DIM
CHUNK = 128
WINDOW = 128
EPS = 1e-6
ATTN_SCALE = HEAD_DIM ** -0.5

LANES = 128
HALF = LANES // 2
N_PAIRS = ATTN_WIDTH // LANES
VMEM_LIMIT = 56 * 1024 * 1024

PROMPT_TILE = 512
FF_CHUNK = 512
EARLY_CHUNKS = 1
SAMPLE_BB = 16

ROW_G_PRE, ROW_G_PM, ROW_G_PFF, ROW_G_PF, ROW_LN, ROW_GOUT = range(6)


def _dot(a, b):
    return jnp.dot(a, b, preferred_element_type=F32)


def _rms(x, g):
    return x * lax.rsqrt(jnp.mean(x * x, axis=-1, keepdims=True) + EPS) * g


def _layer_norm(x, g, b):
    xc = x - jnp.mean(x, axis=-1, keepdims=True)
    return xc * lax.rsqrt(jnp.mean(xc * xc, axis=-1, keepdims=True) + EPS) * g + b


def _alibi_slope(head):
    return 2.0 ** (-8.0 * (head + 1) / N_ATTN_HEADS)


def _merge(x, a_out, b_out, w_o_ref, vec_ref):
    d = x.shape[-1]
    cw = CHUNK_WIDTH
    ra = _rms(a_out, vec_ref[ROW_GOUT:ROW_GOUT + 1, 0:cw]).astype(BF16)
    rb = _rms(b_out, vec_ref[ROW_GOUT:ROW_GOUT + 1, cw:cw + ATTN_WIDTH]).astype(BF16)
    o = _dot(ra, w_o_ref[0:cw, :]) + _dot(rb, w_o_ref[cw:cw + ATTN_WIDTH, :])
    return x + _rms(o, vec_ref[ROW_G_PM:ROW_G_PM + 1, :d])


def _relu_sq(up):
    r = jnp.maximum(up, 0.0)
    return (r * r).astype(BF16)


def _ffn(x1, w_up_ref, w_down_ref, vec_ref):
    d = x1.shape[-1]
    h2 = _rms(x1, vec_ref[ROW_G_PFF:ROW_G_PFF + 1, :d]).astype(BF16)
    d_ff = w_up_ref.shape[1]
    f = jnp.zeros(x1.shape, F32)
    for c in range(d_ff // FF_CHUNK):
        up = _dot(h2, w_up_ref[:, c * FF_CHUNK:(c + 1) * FF_CHUNK])
        f = f + _dot(_relu_sq(up), w_down_ref[c * FF_CHUNK:(c + 1) * FF_CHUNK, :])
    return x1 + _rms(f, vec_ref[ROW_G_PF:ROW_G_PF + 1, :d])


def _fill_vector_table(vec_s, g_pre, g_pm, g_pff, g_pf, ln_g, ln_b, g_oc, g_oa):
    d = g_pre.shape[-1]
    cw = CHUNK_WIDTH
    vec_s[ROW_G_PRE:ROW_G_PRE + 1, 0:d] = g_pre[...]
    vec_s[ROW_G_PM:ROW_G_PM + 1, 0:d] = g_pm[...]
    vec_s[ROW_G_PFF:ROW_G_PFF + 1, 0:d] = g_pff[...]
    vec_s[ROW_G_PF:ROW_G_PF + 1, 0:d] = g_pf[...]
    vec_s[ROW_LN:ROW_LN + 1, 0:cw] = ln_g[...]
    vec_s[ROW_LN:ROW_LN + 1, cw:2 * cw] = ln_b[...]
    vec_s[ROW_GOUT:ROW_GOUT + 1, 0:cw] = g_oc[...]
    vec_s[ROW_GOUT:ROW_GOUT + 1, cw:cw + ATTN_WIDTH] = g_oa[...]


def _prompt_kernel(n_tiles, nj, layer, sink_ref, x_ref, x1in_ref, f0in_ref, w_in_ref, w_o_ref,
                   w_up_ref, w_down_ref,
                   g_pre, g_pm, g_pff, g_pf, ln_g, ln_b, g_oc, g_oa, wsp_ref, bsp_ref,
                   y_ref, ys_ref, wk_ref, wv_ref, cv_ref,
                   vec_ref, bias_s, wcat_s, tab_s, kt_s, vd_s, a_s, bo_s, m_s, x1_s, x1n_s,
                   h_s, q_s, h2_s, act_s, f_s):
    step = pl.program_id(0)
    j = jnp.minimum(step, n_tiles - 1) % nj
    T, d = x_ref.shape[1], x_ref.shape[2]
    nb = T // WINDOW
    cw = CHUNK_WIDTH
    n_chunks = w_up_ref.shape[1] // FF_CHUNK

    lane = lax.broadcasted_iota(jnp.int32, (WINDOW, LANES), 1)
    lo = lane < HALF

    @pl.when(step == 0)
    def _init_tables():
        _fill_vector_table(vec_ref, g_pre, g_pm, g_pff, g_pf, ln_g, ln_b, g_oc, g_oa)
        x1_s[...] = x1in_ref[...]
        h2_s[...] = _rms(x1in_ref[...], vec_ref[ROW_G_PFF:ROW_G_PFF + 1, :d]).astype(BF16)
        f_s[...] = f0in_ref[...]
        t_idx = lax.broadcasted_iota(jnp.int32, (CHUNK, CHUNK), 0)
        s_idx = lax.broadcasted_iota(jnp.int32, (CHUNK, CHUNK), 1)
        for g in range(N_CHUNK_HEADS):
            wm = jnp.where(s_idx <= t_idx, wsp_ref[0, g], 0.0).astype(BF16)
            wcat_s[g // 2, :, (g % 2) * CHUNK:(g % 2 + 1) * CHUNK] = wm
        b_rows = jnp.concatenate(
            [bsp_ref[0], jnp.zeros((CHUNK - N_CHUNK_HEADS, CHUNK), F32)], axis=0)
        b_cols = b_rows.T
        for p in range(cw // LANES):
            bias_s[:, p * LANES:(p + 1) * LANES] = jnp.where(
                lo, jnp.broadcast_to(b_cols[:, 2 * p:2 * p + 1], (CHUNK, LANES)),
                jnp.broadcast_to(b_cols[:, 2 * p + 1:2 * p + 2], (CHUNK, LANES)))
        a_idx = lax.broadcasted_iota(jnp.int32, (WINDOW, 2 * WINDOW), 0)
        c_idx = lax.broadcasted_iota(jnp.int32, (WINDOW, 2 * WINDOW), 1)
        dist = WINDOW + a_idx - c_idx
        valid = (dist >= 0) & (dist <= WINDOW)
        for hd in range(N_ATTN_HEADS):
            bias = jnp.where(valid, -(_alibi_slope(hd) * dist.astype(F32)), -jnp.inf)
            r = hd % Q_PER_KV
            tab_s[hd // Q_PER_KV, r * WINDOW:(r + 1) * WINDOW, :] = bias

    @pl.when(j == 0)
    def _clear_carry():
        kt_s[:, 0:WINDOW] = jnp.zeros((2 * LANES, WINDOW), BF16)
        vd_s[:, 0:WINDOW, :] = jnp.zeros((N_KV_HEADS, WINDOW, LANES), BF16)

    def ffn_up(c):
        act_s[...] = _relu_sq(_dot(h2_s[...], w_up_ref[:, c * FF_CHUNK:(c + 1) * FF_CHUNK]))

    def ffn_down(c):
        part = _dot(act_s[...], w_down_ref[c * FF_CHUNK:(c + 1) * FF_CHUNK, :])
        if c == 0:
            f_s[...] = part
        else:
            f_s[...] += part

    def finish_previous():
        y_ref[0] = x1_s[...] + _rms(f_s[...], vec_ref[ROW_G_PF:ROW_G_PF + 1, :d])

    def tile_step(with_late):
        late = iter([functools.partial(fn, c) for c in range(EARLY_CHUNKS, n_chunks)
                     for fn in (ffn_up, ffn_down)])

        def late_stage(count=1):
            if with_late:
                for _ in range(count):
                    next(late)()

        late_stage()
        h_s[...] = _rms(x_ref[0], vec_ref[ROW_G_PRE:ROW_G_PRE + 1, :d]).astype(BF16)
        u_pre = _dot(h_s[...], w_in_ref[:, 0:cw])
        v_pre = _dot(h_s[...], w_in_ref[:, cw:2 * cw])
        late_stage()
        u = jax.nn.gelu(u_pre)
        late_stage()
        v = _layer_norm(jax.nn.gelu(v_pre), vec_ref[ROW_LN:ROW_LN + 1, 0:cw],
                        vec_ref[ROW_LN:ROW_LN + 1, cw:2 * cw])
        kv0 = 2 * cw + ATTN_WIDTH
        kv = _dot(h_s[...], w_in_ref[:, kv0:kv0 + 2 * KV_WIDTH])
        k = kv[:, 0:KV_WIDTH]
        val = kv[:, KV_WIDTH:2 * KV_WIDTH]
        q0 = 2 * cw
        q_s[...] = _dot(h_s[...], w_in_ref[:, q0:q0 + ATTN_WIDTH]) * ATTN_SCALE
        late_stage()

        kt_f = k.T
        wk_ref[0] = kt_f[:, T - WINDOW:]
        wv_ref[0] = val[T - WINDOW:, :].T
        cv_ref[0] = v[T - CHUNK:, :]

        assert nb % 2 == 0
        for c in range(0, nb, 2):
            for p in range(cw // LANES):
                cols = slice(p * LANES, (p + 1) * LANES)
                halves = []
                for cc in (c, c + 1):
                    vcp = v[cc * CHUNK:(cc + 1) * CHUNK, cols]
                    halves.append(jnp.concatenate([jnp.where(lo, vcp, 0.0), jnp.where(lo, 0.0, vcp)],
                                                  axis=0))
                rhs = jnp.concatenate(halves, axis=1).astype(BF16)
                s_pair = _dot(wcat_s[p], rhs)
                for side, cc in enumerate((c, c + 1)):
                    rows = slice(cc * CHUNK, (cc + 1) * CHUNK)
                    s_cp = s_pair[:, side * LANES:(side + 1) * LANES] + bias_s[:, cols]
                    a_s[rows, cols] = u[rows, cols] * s_cp
        m_s[:, 0:cw] = _rms(a_s[...], vec_ref[ROW_GOUT:ROW_GOUT + 1, 0:cw]).astype(BF16)

        kt = kt_f.astype(BF16)
        for hk in range(N_KV_HEADS):
            kth = kt[hk * HEAD_DIM:(hk + 1) * HEAD_DIM, :]
            kt_s[hk * LANES:hk * LANES + HEAD_DIM, WINDOW:] = kth
            kt_s[hk * LANES + HEAD_DIM:(hk + 1) * LANES, WINDOW:] = kth
        lane_t = lax.broadcasted_iota(jnp.int32, (T, LANES), 1)
        lo_t = lane_t < HALF
        val_sw = pltpu.roll(val, HALF, axis=1)
        vd_s[0, WINDOW:, :] = jnp.where(lo_t, val, val_sw).astype(BF16)
        vd_s[1, WINDOW:, :] = jnp.where(lo_t, val_sw, val).astype(BF16)

        first_block_bias = jnp.where(j == 0, -jnp.inf, 0.0).astype(F32)
        head_of_row = lax.broadcasted_iota(jnp.int32, (Q_PER_KV * WINDOW, 1), 0) // WINDOW
        n_units = nb * N_KV_HEADS
        n_mid = 2 * (n_chunks - EARLY_CHUNKS) - 9
        late_after_scores = tuple(1 + (k * n_units) // n_mid for k in range(n_mid))
        for n, (i, hk) in enumerate((i, hk) for i in range(nb) for hk in range(N_KV_HEADS)):
            rows = slice(i * WINDOW, (i + 1) * WINDOW)
            keys = slice(i * WINDOW, (i + 2) * WINDOW)
            tiles = []
            for m in range(hk * 2, hk * 2 + 2):
                qb = q_s[rows, m * LANES:(m + 1) * LANES]
                tiles += [jnp.where(lo, qb, 0.0), jnp.where(lo, 0.0, qb)]
            qq = jnp.concatenate(tiles, axis=0).astype(BF16)
            s = _dot(qq, kt_s[hk * LANES:(hk + 1) * LANES, keys]) + tab_s[hk]
            if i == 0:
                s = jnp.concatenate([s[:, 0:WINDOW] + first_block_bias, s[:, WINDOW:]], axis=1)
            if n in late_after_scores:
                late_stage()
            sink = jnp.zeros((Q_PER_KV * WINDOW, 1), F32)
            for r in range(Q_PER_KV):
                sink = jnp.where(head_of_row == r, sink_ref[layer, hk, r], sink)
            mx = jnp.maximum(jnp.max(s, axis=-1, keepdims=True), sink)
            p_un = jnp.exp(s - mx)
            z = jnp.sum(p_un, axis=-1, keepdims=True) + jnp.exp(sink - mx)
            o = _dot(p_un.astype(BF16), vd_s[hk, keys, :]) * (1.0 / z)
            for pair in range(Q_PER_KV // 2):
                m = hk * 2 + pair
                o_even = o[(2 * pair) * WINDOW:(2 * pair + 1) * WINDOW]
                o_odd = o[(2 * pair + 1) * WINDOW:(2 * pair + 2) * WINDOW]
                bo_s[rows, m * LANES:(m + 1) * LANES] = jnp.where(lo, o_even, o_odd)

        kt_s[:, 0:WINDOW] = kt_s[:, T:T + WINDOW]
        vd_s[:, 0:WINDOW, :] = vd_s[:, T:T + WINDOW, :]

        late_stage()
        m_s[:, cw:cw + ATTN_WIDTH] = _rms(
            bo_s[...], vec_ref[ROW_GOUT:ROW_GOUT + 1, cw:cw + ATTN_WIDTH]).astype(BF16)
        o_proj = _dot(m_s[...], w_o_ref[...])
        late_stage(3)
        x1 = x_ref[0] + _rms(o_proj, vec_ref[ROW_G_PM:ROW_G_PM + 1, :d])
        x1n_s[...] = x1
        h2_s[...] = _rms(x1, vec_ref[ROW_G_PFF:ROW_G_PFF + 1, :d]).astype(BF16)
        late_stage()

        ffn_up(0)
        if with_late:
            finish_previous()
        x1_s[...] = x1n_s[...]
        ffn_down(0)
        for c in range(1, EARLY_CHUNKS):
            ffn_up(c)
            ffn_down(c)

    @pl.when(step < n_tiles)
    def _tile_step():
        tile_step(True)

    @pl.when(step == 0)
    def _emit_sample():
        ys_ref[...] = y_ref[0].reshape(ys_ref.shape)

    @pl.when(step == n_tiles)
    def _drain_step():
        for c in range(EARLY_CHUNKS, n_chunks):
            ffn_up(c)
            ffn_down(c)
        finish_previous()


def _resident(shape):
    return pl.BlockSpec(shape, lambda *_: (0,) * len(shape), pipeline_mode=pl.Buffered(1))


def _layer_row(arr, layer):
    block = (1,) + arr.shape[1:]
    return pl.BlockSpec(block, lambda *_: (layer,) + (0,) * (arr.ndim - 1),
                        pipeline_mode=pl.Buffered(1))


def _prompt_call(x, x1_sample, f0_sample, sample_shape, layer, sinks, w_in, w_o, w_up, w_down,
                 norms, w_sp, b_sp):
    B, S, D = x.shape
    T = PROMPT_TILE
    assert x1_sample.shape == (T, D) and f0_sample.shape == (T, D)
    assert S % T == 0 and T % WINDOW == 0 and WINDOW == CHUNK
    assert 1 <= EARLY_CHUNKS and 2 * (w_up.shape[1] // FF_CHUNK - EARLY_CHUNKS) >= 9
    nj = S // T
    n_tiles = B * nj
    cur = lambda s: jnp.minimum(s, n_tiles - 1)
    prev = lambda s: jnp.maximum(s - 1, 0)
    tile_in = pl.BlockSpec((1, T, D), lambda s: (cur(s) // nj, cur(s) % nj, 0))
    tile_out = pl.BlockSpec((1, T, D), lambda s: (prev(s) // nj, prev(s) % nj, 0))
    last = lambda r, c: pl.BlockSpec((1, r, c), lambda s: (cur(s) // nj, 0, 0))
    return pl.pallas_call(
        functools.partial(_prompt_kernel, n_tiles, nj, layer),
        grid=(n_tiles + 1,),
        in_specs=[pl.BlockSpec(memory_space=pltpu.SMEM), tile_in, _resident((T, D)),
                  _resident((T, D)),
                  _resident(w_in.shape), _resident(w_o.shape), _resident(w_up.shape),
                  _resident(w_down.shape)] + [_layer_row(a, layer) for a in norms] +
                 [_layer_row(w_sp, layer), _layer_row(b_sp, layer)],
        out_specs=[tile_out, pl.BlockSpec(sample_shape, lambda s: (0,) * len(sample_shape)),
                   last(KV_WIDTH, WINDOW), last(KV_WIDTH, WINDOW), last(CHUNK, CHUNK_WIDTH)],
        out_shape=[jax.ShapeDtypeStruct((B, S, D), F32),
                   jax.ShapeDtypeStruct(sample_shape, F32),
                   jax.ShapeDtypeStruct((B, KV_WIDTH, WINDOW), F32),
                   jax.ShapeDtypeStruct((B, KV_WIDTH, WINDOW), F32),
                   jax.ShapeDtypeStruct((B, CHUNK, CHUNK_WIDTH), F32)],
        scratch_shapes=[pltpu.VMEM((8, D), F32),
                        pltpu.VMEM((CHUNK, CHUNK_WIDTH), F32),
                        pltpu.VMEM((N_CHUNK_HEADS // 2, CHUNK, 2 * CHUNK), BF16),
                        pltpu.VMEM((N_KV_HEADS, Q_PER_KV * WINDOW, 2 * WINDOW), F32),
                        pltpu.VMEM((2 * LANES, T + WINDOW), BF16),
                        pltpu.VMEM((N_KV_HEADS, T + WINDOW, LANES), BF16),
                        pltpu.VMEM((T, CHUNK_WIDTH), F32),
                        pltpu.VMEM((T, ATTN_WIDTH), F32),
                        pltpu.VMEM((T, CHUNK_WIDTH + ATTN_WIDTH), BF16),
                        pltpu.VMEM((T, D), F32),
                        pltpu.VMEM((T, D), F32),
                        pltpu.VMEM((T, D), BF16),
                        pltpu.VMEM((T, ATTN_WIDTH), F32),
                        pltpu.VMEM((T, D), BF16),
                        pltpu.VMEM((T, FF_CHUNK), BF16),
                        pltpu.VMEM((T, D), F32)],
        compiler_params=pltpu.CompilerParams(
            dimension_semantics=("arbitrary",), vmem_limit_bytes=VMEM_LIMIT),
        name="prompt_layer",
    )(sinks, x, x1_sample, f0_sample, w_in, w_o, w_up, w_down, *norms, w_sp, b_sp)


def _sample_kernel(L, layer, sink_ref, w4_ref, b4_ref, x_ref, ckt_ref, cvt_ref, w_in_ref,
                   w_o_hbm, w_up_hbm, w_down_hbm,
                   g_pre, g_pm, g_pff, g_pf, ln_g, ln_b, g_oc, g_oa,
                   x1o_ref, f0o_ref, vn_ref, wkt_ref, wvt_ref,
                   vec_ref, q_s, kvt_s, a_s, b_s, w_o_ref, w_up_ref, w_down_ref, w_sem):
    g = pl.program_id(0)
    n_steps = pl.num_programs(0)
    bb, hd, W = ckt_ref.shape
    n, d = q_s.shape[0], x_ref.shape[-1]
    n_heads = N_ATTN_HEADS
    pair_rows = 2 * L
    assert pair_rows == 8 and hd == LANES and W == LANES
    batch_per_block = LANES // L

    def late_weight_copies():
        pairs = ((w_o_hbm, w_o_ref), (w_up_hbm, w_up_ref), (w_down_hbm, w_down_ref))
        return [pltpu.make_async_copy(src, dst, w_sem.at[i]) for i, (src, dst) in enumerate(pairs)]

    @pl.when(g == 0)
    def _project_and_gate():
        cw = CHUNK_WIDTH
        for copy in late_weight_copies():
            copy.start()
        _fill_vector_table(vec_ref, g_pre, g_pm, g_pff, g_pf, ln_g, ln_b, g_oc, g_oa)
        h = _rms(x_ref[...].reshape(n, d), vec_ref[ROW_G_PRE:ROW_G_PRE + 1, :d]).astype(BF16)
        u = jax.nn.gelu(_dot(h, w_in_ref[:, 0:cw]))
        v = _layer_norm(jax.nn.gelu(_dot(h, w_in_ref[:, cw:2 * cw])),
                        vec_ref[ROW_LN:ROW_LN + 1, 0:cw], vec_ref[ROW_LN:ROW_LN + 1, cw:2 * cw])
        vn_ref[...] = v.reshape(vn_ref.shape)
        q_s[...] = _dot(h, w_in_ref[:, 2 * cw:2 * cw + ATTN_WIDTH]) * ATTN_SCALE
        kv0 = 2 * cw + ATTN_WIDTH
        kvt_s[...] = _dot(h, w_in_ref[:, kv0:kv0 + 2 * KV_WIDTH]).T
        t_of_row = lax.broadcasted_iota(jnp.int32, (n, 1), 0) % L
        group_of_lane = lax.broadcasted_iota(jnp.int32, (1, cw), 1) // HEAD_DIM

        def per_group(value_of_group):
            row = jnp.zeros((1, cw), F32)
            for grp in range(N_CHUNK_HEADS):
                row = jnp.where(group_of_lane == grp, value_of_group(grp), row)
            return row

        s = jnp.zeros(v.shape, F32)
        for t in range(L):
            s = jnp.where(t_of_row == t, per_group(lambda grp: b4_ref[layer, grp, t]), s)
        for delta in range(L):
            coef = jnp.zeros(v.shape, F32)
            for t in range(delta, L):
                w_row = per_group(lambda grp: w4_ref[grp, t, t - delta])
                coef = jnp.where(t_of_row == t, w_row, coef)
            vs = v if delta == 0 else pltpu.roll(v, delta, axis=0)
            s = s + coef * vs
        a_s[...] = u * s

    rows_all = n_heads * pair_rows
    r_idx = lax.broadcasted_iota(jnp.int32, (rows_all, LANES), 0)
    l_idx = lax.broadcasted_iota(jnp.int32, (rows_all, LANES), 1)
    t_idx = r_idx % L
    odd = (r_idx % pair_rows) >= L
    head_of_row = lax.broadcasted_iota(jnp.int32, (rows_all, 1), 0) // pair_rows
    sink = jnp.zeros((rows_all, 1), F32)
    slope = jnp.zeros((rows_all, 1), F32)
    for hh in range(n_heads):
        sink = jnp.where(head_of_row == hh, sink_ref[layer, hh // Q_PER_KV, hh % Q_PER_KV], sink)
        slope = jnp.where(head_of_row == hh, _alibi_slope(hh), slope)
    dist_old = W + t_idx - l_idx
    bias_old = jnp.where(dist_old <= WINDOW, -(slope * dist_old.astype(F32)), -jnp.inf)
    lane8 = lax.broadcasted_iota(jnp.int32, (pair_rows, LANES), 1)
    lo8 = lane8 < HALF
    lane_w = lax.broadcasted_iota(jnp.int32, (hd, W), 1)

    first_batch = g * bb
    new_block = pl.multiple_of((first_batch // batch_per_block) * LANES, LANES)
    kt_new = kvt_s[0:KV_WIDTH, pl.ds(new_block, LANES)]
    vt_new = kvt_s[KV_WIDTH:2 * KV_WIDTH, pl.ds(new_block, LANES)]
    kt_new_b = kt_new.astype(BF16)
    vt_new_b = vt_new.astype(BF16)
    n_pairs = bb // 2
    offs = [((first_batch + 2 * p) % batch_per_block) * L for p in range(n_pairs)]
    rows0 = [pl.multiple_of((first_batch + 2 * p) * L, pair_rows) for p in range(n_pairs)]
    keep = lane_w < W - L

    scores = []
    for p in range(n_pairs):
        tiles = []
        for m in range(N_PAIRS):
            qb = q_s[pl.ds(rows0[p], pair_rows), m * LANES:(m + 1) * LANES]
            qb_sw = pltpu.roll(qb, HALF, axis=1)
            if (2 * m) // Q_PER_KV == 0:
                tiles += [jnp.where(lo8, qb, 0.0), jnp.where(lo8, qb_sw, 0.0)]
            else:
                tiles += [jnp.where(lo8, 0.0, qb_sw), jnp.where(lo8, 0.0, qb)]
        lhs = jnp.concatenate(tiles, axis=0).astype(BF16)
        s_pair = []
        for e in range(2):
            rhs = jnp.concatenate([ckt_ref[2 * p + e].astype(BF16), kt_new_b], axis=1)
            s_pair.append(_dot(lhs, rhs))
        scores.append(jnp.where(jnp.concatenate([odd, odd], axis=1), s_pair[1], s_pair[0]))
        for e in range(2):
            wkt_ref[2 * p + e] = jnp.where(
                keep, pltpu.roll(ckt_ref[2 * p + e], W - L, axis=1),
                pltpu.roll(kt_new, (W - L) - (offs[p] + e * L), axis=1))

    weights = []
    for p in range(n_pairs):
        rel = l_idx - offs[p] - jnp.where(odd, L, 0)
        bias_new = jnp.where((rel >= 0) & (rel <= t_idx),
                             -(slope * (t_idx - rel).astype(F32)), -jnp.inf)
        sc = scores[p] + jnp.concatenate([bias_old, bias_new], axis=1)
        mx = jnp.maximum(jnp.max(sc, axis=-1, keepdims=True), sink)
        p_un = jnp.exp(sc - mx)
        z = jnp.sum(p_un, axis=-1, keepdims=True) + jnp.exp(sink - mx)
        weights.append((p_un.astype(BF16), z))

    for p in range(n_pairs):
        pb, z = weights[p]
        o_pair = []
        for e in range(2):
            vt = jnp.concatenate([cvt_ref[2 * p + e].astype(BF16), vt_new_b], axis=1)
            o_pair.append(lax.dot_general(pb, vt, (((1,), (1,)), ((), ())),
                                          preferred_element_type=F32))
        o = jnp.where(odd, o_pair[1], o_pair[0]) * (1.0 / z)
        for m in range(N_PAIRS):
            kv_head = (2 * m) // Q_PER_KV
            o_even = o[(2 * m) * pair_rows:(2 * m + 1) * pair_rows]
            o_odd = o[(2 * m + 1) * pair_rows:(2 * m + 2) * pair_rows]
            if kv_head == 0:
                o_odd = pltpu.roll(o_odd, HALF, axis=1)
            else:
                o_even = pltpu.roll(o_even, HALF, axis=1)
            b_s[pl.ds(rows0[p], pair_rows), m * LANES:(m + 1) * LANES] = (
                jnp.where(lo8, o_even, o_odd))
        for e in range(2):
            wvt_ref[2 * p + e] = jnp.where(
                keep, pltpu.roll(cvt_ref[2 * p + e], W - L, axis=1),
                pltpu.roll(vt_new, (W - L) - (offs[p] + e * L), axis=1))

    @pl.when(g == n_steps - 1)
    def _merge_and_ffn():
        for copy in late_weight_copies():
            copy.wait()
        x1 = _merge(x_ref[...].reshape(n, d), a_s[...], b_s[...], w_o_ref, vec_ref)
        x1o_ref[...] = x1
        h2 = _rms(x1, vec_ref[ROW_G_PFF:ROW_G_PFF + 1, :d]).astype(BF16)
        f0 = jnp.zeros(x1.shape, F32)
        for c in range(EARLY_CHUNKS):
            up = _dot(h2, w_up_ref[:, c * FF_CHUNK:(c + 1) * FF_CHUNK])
            f0 = f0 + _dot(_relu_sq(up), w_down_ref[c * FF_CHUNK:(c + 1) * FF_CHUNK, :])
        f0o_ref[...] = f0


def _sample_call(x, layer, sinks, w4, b_sp, cache_kt, cache_vt, w_in, w_o, w_up, w_down,
                 norms):
    Bd, L, D = x.shape
    hd, W = cache_kt.shape[1:]
    n = Bd * L
    bb = SAMPLE_BB
    assert Bd % bb == 0 and bb % 2 == 0 and (LANES // L) % bb == 0
    smem = pl.BlockSpec(memory_space=pltpu.SMEM)
    blk = pl.BlockSpec((bb, hd, W), lambda g: (g, 0, 0))
    const = lambda shape: pl.BlockSpec(shape, lambda g: (0,) * len(shape))
    late_weights = (w_o, w_up, w_down)
    in_hbm = pl.BlockSpec(memory_space=pl.ANY)
    return pl.pallas_call(
        functools.partial(_sample_kernel, L, layer),
        grid=(Bd // bb,),
        in_specs=[smem, smem, smem, _resident(x.shape), blk, blk, _resident(w_in.shape)] +
                 [in_hbm for _ in late_weights] + [_layer_row(a, layer) for a in norms],
        out_specs=[const((n, D)), const((n, D)), const((Bd, L, CHUNK_WIDTH)), blk, blk],
        out_shape=[jax.ShapeDtypeStruct((n, D), F32),
                   jax.ShapeDtypeStruct((n, D), F32),
                   jax.ShapeDtypeStruct((Bd, L, CHUNK_WIDTH), F32),
                   jax.ShapeDtypeStruct(cache_kt.shape, F32),
                   jax.ShapeDtypeStruct(cache_vt.shape, F32)],
        scratch_shapes=[pltpu.VMEM((8, D), F32),
                        pltpu.VMEM((n, ATTN_WIDTH), F32),
                        pltpu.VMEM((2 * KV_WIDTH, n), F32),
                        pltpu.VMEM((n, CHUNK_WIDTH), F32),
                        pltpu.VMEM((n, ATTN_WIDTH), F32)] +
                       [pltpu.VMEM(w.shape, w.dtype) for w in late_weights] +
                       [pltpu.SemaphoreType.DMA((len(late_weights),))],
        compiler_params=pltpu.CompilerParams(
            dimension_semantics=("arbitrary",), vmem_limit_bytes=VMEM_LIMIT),
        name="sample_layer",
    )(sinks, w4, b_sp, x, cache_kt, cache_vt, w_in, *late_weights, *norms)


def _heads_last(t):
    b, _, w = t.shape
    return t.reshape(b, N_KV_HEADS, HEAD_DIM, w).transpose(0, 3, 1, 2)


def kernel(x_prompt, x_sample, cache_win_k, cache_win_v, w_in, g_pre_mix, ln_v_g, ln_v_b,
           w_spatial, b_spatial, attn_sinks, g_out_chunk, g_out_attn, w_o, g_post_mix,
           g_pre_ffn, w_up, w_down, g_post_ffn):
    depth = w_in.shape[0]
    Bd, L, D = x_sample.shape
    W = cache_win_k.shape[2]
    norms = (g_pre_mix, g_post_mix, g_pre_ffn, g_post_ffn, ln_v_g, ln_v_b, g_out_chunk,
             g_out_attn)
    yp, ys = x_prompt, x_sample
    wk_p, wv_p, cv_p, wk_s, wv_s, cv_s = [], [], [], [], [], []
    for l in range(depth):
        w_in_b = w_in[l].astype(BF16)
        w_o_b = w_o[l].astype(BF16)
        w_up_b = w_up[l].astype(BF16)
        w_down_b = w_down[l].astype(BF16)

        w4 = w_spatial[l, :, :L, :L]
        cache_kt = cache_win_k[l].transpose(0, 2, 3, 1).reshape(Bd, KV_WIDTH, W)
        cache_vt = cache_win_v[l].transpose(0, 2, 3, 1).reshape(Bd, KV_WIDTH, W)
        x1_s, f0_s, cv, wk, wv = _sample_call(ys, l, attn_sinks, w4, b_spatial, cache_kt,
                                              cache_vt, w_in_b, w_o_b, w_up_b, w_down_b, norms)
        wk_s.append(_heads_last(wk))
        wv_s.append(_heads_last(wv))
        cv_s.append(cv)

        yp, ys, wk, wv, cv = _prompt_call(yp, x1_s, f0_s, ys.shape, l, attn_sinks, w_in_b, w_o_b,
                                          w_up_b, w_down_b, norms, w_spatial, b_spatial)
        wk_p.append(_heads_last(wk))
        wv_p.append(_heads_last(wv))
        cv_p.append(cv)
    return (yp, ys, jnp.stack(wk_p), jnp.stack(wv_p), jnp.stack(cv_p),
            jnp.stack(wk_s), jnp.stack(wv_s), jnp.stack(cv_s))
```

```python
import functools

import jax
import jax.numpy as jnp
from jax import lax
from jax.experimental import pallas as pl
from jax.experimental.pallas import tpu as pltpu

F32 = jnp.float32
BF16 = jnp.bfloat16

HEAD_DIM = 64
N_CHUNK_HEADS = 8
N_ATTN_HEADS = 8
N_KV_HEADS = 2
Q_PER_KV = N_ATTN_HEADS // N_KV_HEADS
CHUNK_WIDTH = N_CHUNK_HEADS * HEAD_DIM
ATTN_WIDTH = N_ATTN_HEADS * HEAD_DIM
KV_WIDTH = N_KV_HEADS * HEAD_DIM
CHUNK = 128
WINDOW = 128
EPS = 1e-6
ATTN_SCALE = HEAD_DIM ** -0.5

LANES = 128
HALF = LANES // 2
N_PAIRS = ATTN_WIDTH // LANES
VMEM_LIMIT = 56 * 1024 * 1024

PROMPT_TILE = 512
FF_CHUNK = 512
EARLY_CHUNKS = 1
SAMPLE_BB = 16

ROW_G_PRE, ROW_G_PM, ROW_G_PFF, ROW_G_PF, ROW_LN, ROW_GOUT = range(6)


def _dot(a, b):
    return jnp.dot(a, b, preferred_element_type=F32)


def _rms(x, g):
    return x * lax.rsqrt(jnp.mean(x * x, axis=-1, keepdims=True) + EPS) * g


def _layer_norm(x, g, b):
    xc = x - jnp.mean(x, axis=-1, keepdims=True)
    return xc * lax.rsqrt(jnp.mean(xc * xc, axis=-1, keepdims=True) + EPS) * g + b


def _alibi_slope(head):
    return 2.0 ** (-8.0 * (head + 1) / N_ATTN_HEADS)


def _merge(x, a_out, b_out, w_o_ref, vec_ref):
    d = x.shape[-1]
    cw = CHUNK_WIDTH
    ra = _rms(a_out, vec_ref[ROW_GOUT:ROW_GOUT + 1, 0:cw]).astype(BF16)
    rb = _rms(b_out, vec_ref[ROW_GOUT:ROW_GOUT + 1, cw:cw + ATTN_WIDTH]).astype(BF16)
    o = _dot(ra, w_o_ref[0:cw, :]) + _dot(rb, w_o_ref[cw:cw + ATTN_WIDTH, :])
    return x + _rms(o, vec_ref[ROW_G_PM:ROW_G_PM + 1, :d])


def _relu_sq(up):
    r = jnp.maximum(up, 0.0)
    return (r * r).astype(BF16)


def _ffn(x1, w_up_ref, w_down_ref, vec_ref):
    d = x1.shape[-1]
    h2 = _rms(x1, vec_ref[ROW_G_PFF:ROW_G_PFF + 1, :d]).astype(BF16)
    d_ff = w_up_ref.shape[1]
    f = jnp.zeros(x1.shape, F32)
    for c in range(d_ff // FF_CHUNK):
        up = _dot(h2, w_up_ref[:, c * FF_CHUNK:(c + 1) * FF_CHUNK])
        f = f + _dot(_relu_sq(up), w_down_ref[c * FF_CHUNK:(c + 1) * FF_CHUNK, :])
    return x1 + _rms(f, vec_ref[ROW_G_PF:ROW_G_PF + 1, :d])


def _fill_vector_table(vec_s, g_pre, g_pm, g_pff, g_pf, ln_g, ln_b, g_oc, g_oa):
    d = g_pre.shape[-1]
    cw = CHUNK_WIDTH
    vec_s[ROW_G_PRE:ROW_G_PRE + 1, 0:d] = g_pre[...]
    vec_s[ROW_G_PM:ROW_G_PM + 1, 0:d] = g_pm[...]
    vec_s[ROW_G_PFF:ROW_G_PFF + 1, 0:d] = g_pff[...]
    vec_s[ROW_G_PF:ROW_G_PF + 1, 0:d] = g_pf[...]
    vec_s[ROW_LN:ROW_LN + 1, 0:cw] = ln_g[...]
    vec_s[ROW_LN:ROW_LN + 1, cw:2 * cw] = ln_b[...]
    vec_s[ROW_GOUT:ROW_GOUT + 1, 0:cw] = g_oc[...]
    vec_s[ROW_GOUT:ROW_GOUT + 1, cw:cw + ATTN_WIDTH] = g_oa[...]


def _prompt_kernel(n_tiles, nj, layer, sink_ref, x_ref, x1in_ref, f0in_ref, w_in_ref, w_o_ref,
                   w_up_ref, w_down_ref,
                   g_pre, g_pm, g_pff, g_pf, ln_g, ln_b, g_oc, g_oa, wsp_ref, bsp_ref,
                   y_ref, ys_ref, wk_ref, wv_ref, cv_ref,
                   vec_ref, bias_s, wcat_s, tab_s, kt_s, vd_s, a_s, bo_s, m_s, x1_s, x1n_s,
                   h_s, q_s, h2_s, act_s, f_s):
    step = pl.program_id(0)
    j = jnp.minimum(step, n_tiles - 1) % nj
    T, d = x_ref.shape[1], x_ref.shape[2]
    nb = T // WINDOW
    cw = CHUNK_WIDTH
    n_chunks = w_up_ref.shape[1] // FF_CHUNK

    lane = lax.broadcasted_iota(jnp.int32, (WINDOW, LANES), 1)
    lo = lane < HALF

    @pl.when(step == 0)
    def _init_tables():
        _fill_vector_table(vec_ref, g_pre, g_pm, g_pff, g_pf, ln_g, ln_b, g_oc, g_oa)
        x1_s[...] = x1in_ref[...]
        h2_s[...] = _rms(x1in_ref[...], vec_ref[ROW_G_PFF:ROW_G_PFF + 1, :d]).astype(BF16)
        f_s[...] = f0in_ref[...]
        t_idx = lax.broadcasted_iota(jnp.int32, (CHUNK, CHUNK), 0)
        s_idx = lax.broadcasted_iota(jnp.int32, (CHUNK, CHUNK), 1)
        for g in range(N_CHUNK_HEADS):
            wm = jnp.where(s_idx <= t_idx, wsp_ref[0, g], 0.0).astype(BF16)
            wcat_s[g // 2, :, (g % 2) * CHUNK:(g % 2 + 1) * CHUNK] = wm
        b_rows = jnp.concatenate(
            [bsp_ref[0], jnp.zeros((CHUNK - N_CHUNK_HEADS, CHUNK), F32)], axis=0)
        b_cols = b_rows.T
        for p in range(cw // LANES):
            bias_s[:, p * LANES:(p + 1) * LANES] = jnp.where(
                lo, jnp.broadcast_to(b_cols[:, 2 * p:2 * p + 1], (CHUNK, LANES)),
                jnp.broadcast_to(b_cols[:, 2 * p + 1:2 * p + 2], (CHUNK, LANES)))
        a_idx = lax.broadcasted_iota(jnp.int32, (WINDOW, 2 * WINDOW), 0)
        c_idx = lax.broadcasted_iota(jnp.int32, (WINDOW, 2 * WINDOW), 1)
        dist = WINDOW + a_idx - c_idx
        valid = (dist >= 0) & (dist <= WINDOW)
        for hd in range(N_ATTN_HEADS):
            bias = jnp.where(valid, -(_alibi_slope(hd) * dist.astype(F32)), -jnp.inf)
            r = hd % Q_PER_KV
            tab_s[hd // Q_PER_KV, r * WINDOW:(r + 1) * WINDOW, :] = bias

    @pl.when(j == 0)
    def _clear_carry():
        kt_s[:, 0:WINDOW] = jnp.zeros((2 * LANES, WINDOW), BF16)
        vd_s[:, 0:WINDOW, :] = jnp.zeros((N_KV_HEADS, WINDOW, LANES), BF16)

    def ffn_up(c):
        act_s[...] = _relu_sq(_dot(h2_s[...], w_up_ref[:, c * FF_CHUNK:(c + 1) * FF_CHUNK]))

    def ffn_down(c):
        part = _dot(act_s[...], w_down_ref[c * FF_CHUNK:(c + 1) * FF_CHUNK, :])
        if c == 0:
            f_s[...] = part
        else:
            f_s[...] += part

    def finish_previous():
        y_ref[0] = x1_s[...] + _rms(f_s[...], vec_ref[ROW_G_PF:ROW_G_PF + 1, :d])

    def tile_step(with_late):
        late = iter([functools.partial(fn, c) for c in range(EARLY_CHUNKS, n_chunks)
                     for fn in (ffn_up, ffn_down)])

        def late_stage(count=1):
            if with_late:
                for _ in range(count):
                    next(late)()

        late_stage()
        h_s[...] = _rms(x_ref[0], vec_ref[ROW_G_PRE:ROW_G_PRE + 1, :d]).astype(BF16)
        u_pre = _dot(h_s[...], w_in_ref[:, 0:cw])
        v_pre = _dot(h_s[...], w_in_ref[:, cw:2 * cw])
        late_stage()
        u = jax.nn.gelu(u_pre)
        late_stage()
        v = _layer_norm(jax.nn.gelu(v_pre), vec_ref[ROW_LN:ROW_LN + 1, 0:cw],
                        vec_ref[ROW_LN:ROW_LN + 1, cw:2 * cw])
        kv0 = 2 * cw + ATTN_WIDTH
        kv = _dot(h_s[...], w_in_ref[:, kv0:kv0 + 2 * KV_WIDTH])
        k = kv[:, 0:KV_WIDTH]
        val = kv[:, KV_WIDTH:2 * KV_WIDTH]
        q0 = 2 * cw
        q_s[...] = _dot(h_s[...], w_in_ref[:, q0:q0 + ATTN_WIDTH]) * ATTN_SCALE
        late_stage()

        kt_f = k.T
        wk_ref[0] = kt_f[:, T - WINDOW:]
        wv_ref[0] = val[T - WINDOW:, :].T
        cv_ref[0] = v[T - CHUNK:, :]

        assert nb % 2 == 0
        for c in range(0, nb, 2):
            for p in range(cw // LANES):
                cols = slice(p * LANES, (p + 1) * LANES)
                halves = []
                for cc in (c, c + 1):
                    vcp = v[cc * CHUNK:(cc + 1) * CHUNK, cols]
                    halves.append(jnp.concatenate([jnp.where(lo, vcp, 0.0), jnp.where(lo, 0.0, vcp)],
                                                  axis=0))
                rhs = jnp.concatenate(halves, axis=1).astype(BF16)
                s_pair = _dot(wcat_s[p], rhs)
                for side, cc in enumerate((c, c + 1)):
                    rows = slice(cc * CHUNK, (cc + 1) * CHUNK)
                    s_cp = s_pair[:, side * LANES:(side + 1) * LANES] + bias_s[:, cols]
                    a_s[rows, cols] = u[rows, cols] * s_cp
        m_s[:, 0:cw] = _rms(a_s[...], vec_ref[ROW_GOUT:ROW_GOUT + 1, 0:cw]).astype(BF16)

        kt = kt_f.astype(BF16)
        for hk in range(N_KV_HEADS):
            kth = kt[hk * HEAD_DIM:(hk + 1) * HEAD_DIM, :]
            kt_s[hk * LANES:hk * LANES + HEAD_DIM, WINDOW:] = kth
            kt_s[hk * LANES + HEAD_DIM:(hk + 1) * LANES, WINDOW:] = kth
        lane_t = lax.broadcasted_iota(jnp.int32, (T, LANES), 1)
        lo_t = lane_t < HALF
        val_sw = pltpu.roll(val, HALF, axis=1)
        vd_s[0, WINDOW:, :] = jnp.where(lo_t, val, val_sw).astype(BF16)
        vd_s[1, WINDOW:, :] = jnp.where(lo_t, val_sw, val).astype(BF16)

        first_block_bias = jnp.where(j == 0, -jnp.inf, 0.0).astype(F32)
        head_of_row = lax.broadcasted_iota(jnp.int32, (Q_PER_KV * WINDOW, 1), 0) // WINDOW
        n_units = nb * N_KV_HEADS
        n_mid = 2 * (n_chunks - EARLY_CHUNKS) - 9
        late_after_scores = tuple(1 + (k * n_units) // n_mid for k in range(n_mid))
        for n, (i, hk) in enumerate((i, hk) for i in range(nb) for hk in range(N_KV_HEADS)):
            rows = slice(i * WINDOW, (i + 1) * WINDOW)
            keys = slice(i * WINDOW, (i + 2) * WINDOW)
            tiles = []
            for m in range(hk * 2, hk * 2 + 2):
                qb = q_s[rows, m * LANES:(m + 1) * LANES]
                tiles += [jnp.where(lo, qb, 0.0), jnp.where(lo, 0.0, qb)]
            qq = jnp.concatenate(tiles, axis=0).astype(BF16)
            s = _dot(qq, kt_s[hk * LANES:(hk + 1) * LANES, keys]) + tab_s[hk]
            if i == 0:
                s = jnp.concatenate([s[:, 0:WINDOW] + first_block_bias, s[:, WINDOW:]], axis=1)
            if n in late_after_scores:
                late_stage()
            sink = jnp.zeros((Q_PER_KV * WINDOW, 1), F32)
            for r in range(Q_PER_KV):
                sink = jnp.where(head_of_row == r, sink_ref[layer, hk, r], sink)
            mx = jnp.maximum(jnp.max(s, axis=-1, keepdims=True), sink)
            p_un = jnp.exp(s - mx)
            z = jnp.sum(p_un, axis=-1, keepdims=True) + jnp.exp(sink - mx)
            o = _dot(p_un.astype(BF16), vd_s[hk, keys, :]) * (1.0 / z)
            for pair in range(Q_PER_KV // 2):
                m = hk * 2 + pair
                o_even = o[(2 * pair) * WINDOW:(2 * pair + 1) * WINDOW]
                o_odd = o[(2 * pair + 1) * WINDOW:(2 * pair + 2) * WINDOW]
                bo_s[rows, m * LANES:(m + 1) * LANES] = jnp.where(lo, o_even, o_odd)

        kt_s[:, 0:WINDOW] = kt_s[:, T:T + WINDOW]
        vd_s[:, 0:WINDOW, :] = vd_s[:, T:T + WINDOW, :]

        late_stage()
        m_s[:, cw:cw + ATTN_WIDTH] = _rms(
            bo_s[...], vec_ref[ROW_GOUT:ROW_GOUT + 1, cw:cw + ATTN_WIDTH]).astype(BF16)
        o_proj = _dot(m_s[...], w_o_ref[...])
        late_stage(3)
        x1 = x_ref[0] + _rms(o_proj, vec_ref[ROW_G_PM:ROW_G_PM + 1, :d])
        x1n_s[...] = x1
        h2_s[...] = _rms(x1, vec_ref[ROW_G_PFF:ROW_G_PFF + 1, :d]).astype(BF16)
        late_stage()

        ffn_up(0)
        if with_late:
            finish_previous()
        x1_s[...] = x1n_s[...]
        ffn_down(0)
        for c in range(1, EARLY_CHUNKS):
            ffn_up(c)
            ffn_down(c)

    @pl.when(step < n_tiles)
    def _tile_step():
        tile_step(True)

    @pl.when(step == 0)
    def _emit_sample():
        ys_ref[...] = y_ref[0].reshape(ys_ref.shape)

    @pl.when(step == n_tiles)
    def _drain_step():
        for c in range(EARLY_CHUNKS, n_chunks):
            ffn_up(c)
            ffn_down(c)
        finish_previous()


def _resident(shape):
    return pl.BlockSpec(shape, lambda *_: (0,) * len(shape), pipeline_mode=pl.Buffered(1))


def _layer_row(arr, layer):
    block = (1,) + arr.shape[1:]
    return pl.BlockSpec(block, lambda *_: (layer,) + (0,) * (arr.ndim - 1),
                        pipeline_mode=pl.Buffered(1))


def _prompt_call(x, x1_sample, f0_sample, sample_shape, layer, sinks, w_in, w_o, w_up, w_down,
                 norms, w_sp, b_sp):
    B, S, D = x.shape
    T = PROMPT_TILE
    assert x1_sample.shape == (T, D) and f0_sample.shape == (T, D)
    assert S % T == 0 and T % WINDOW == 0 and WINDOW == CHUNK
    assert 1 <= EARLY_CHUNKS and 2 * (w_up.shape[1] // FF_CHUNK - EARLY_CHUNKS) >= 9
    nj = S // T
    n_tiles = B * nj
    cur = lambda s: jnp.minimum(s, n_tiles - 1)
    prev = lambda s: jnp.maximum(s - 1, 0)
    tile_in = pl.BlockSpec((1, T, D), lambda s: (cur(s) // nj, cur(s) % nj, 0))
    tile_out = pl.BlockSpec((1, T, D), lambda s: (prev(s) // nj, prev(s) % nj, 0))
    last = lambda r, c: pl.BlockSpec((1, r, c), lambda s: (cur(s) // nj, 0, 0))
    return pl.pallas_call(
        functools.partial(_prompt_kernel, n_tiles, nj, layer),
        grid=(n_tiles + 1,),
        in_specs=[pl.BlockSpec(memory_space=pltpu.SMEM), tile_in, _resident((T, D)),
                  _resident((T, D)),
                  _resident(w_in.shape), _resident(w_o.shape), _resident(w_up.shape),
                  _resident(w_down.shape)] + [_layer_row(a, layer) for a in norms] +
                 [_layer_row(w_sp, layer), _layer_row(b_sp, layer)],
        out_specs=[tile_out, pl.BlockSpec(sample_shape, lambda s: (0,) * len(sample_shape)),
                   last(KV_WIDTH, WINDOW), last(KV_WIDTH, WINDOW), last(CHUNK, CHUNK_WIDTH)],
        out_shape=[jax.ShapeDtypeStruct((B, S, D), F32),
                   jax.ShapeDtypeStruct(sample_shape, F32),
                   jax.ShapeDtypeStruct((B, KV_WIDTH, WINDOW), F32),
                   jax.ShapeDtypeStruct((B, KV_WIDTH, WINDOW), F32),
                   jax.ShapeDtypeStruct((B, CHUNK, CHUNK_WIDTH), F32)],
        scratch_shapes=[pltpu.VMEM((8, D), F32),
                        pltpu.VMEM((CHUNK, CHUNK_WIDTH), F32),
                        pltpu.VMEM((N_CHUNK_HEADS // 2, CHUNK, 2 * CHUNK), BF16),
                        pltpu.VMEM((N_KV_HEADS, Q_PER_KV * WINDOW, 2 * WINDOW), F32),
                        pltpu.VMEM((2 * LANES, T + WINDOW), BF16),
                        pltpu.VMEM((N_KV_HEADS, T + WINDOW, LANES), BF16),
                        pltpu.VMEM((T, CHUNK_WIDTH), F32),
                        pltpu.VMEM((T, ATTN_WIDTH), F32),
                        pltpu.VMEM((T, CHUNK_WIDTH + ATTN_WIDTH), BF16),
                        pltpu.VMEM((T, D), F32),
                        pltpu.VMEM((T, D), F32),
                        pltpu.VMEM((T, D), BF16),
                        pltpu.VMEM((T, ATTN_WIDTH), F32),
                        pltpu.VMEM((T, D), BF16),
                        pltpu.VMEM((T, FF_CHUNK), BF16),
                        pltpu.VMEM((T, D), F32)],
        compiler_params=pltpu.CompilerParams(
            dimension_semantics=("arbitrary",), vmem_limit_bytes=VMEM_LIMIT),
        name="prompt_layer",
    )(sinks, x, x1_sample, f0_sample, w_in, w_o, w_up, w_down, *norms, w_sp, b_sp)


def _sample_kernel(L, layer, sink_ref, w4_ref, b4_ref, x_ref, ckt_ref, cvt_ref, w_in_ref,
                   w_o_hbm, w_up_hbm, w_down_hbm,
                   g_pre, g_pm, g_pff, g_pf, ln_g, ln_b, g_oc, g_oa,
                   x1o_ref, f0o_ref, vn_ref, wkt_ref, wvt_ref,
                   vec_ref, q_s, kvt_s, a_s, b_s, w_o_ref, w_up_ref, w_down_ref, w_sem):
    g = pl.program_id(0)
    n_steps = pl.num_programs(0)
    bb, hd, W = ckt_ref.shape
    n, d = q_s.shape[0], x_ref.shape[-1]
    n_heads = N_ATTN_HEADS
    pair_rows = 2 * L
    assert pair_rows == 8 and hd == LANES and W == LANES
    batch_per_block = LANES // L

    def late_weight_copies():
        early = w_up_ref.shape[1]
        pairs = ((w_o_hbm, w_o_ref), (w_up_hbm.at[:, 0:early], w_up_ref),
                 (w_down_hbm.at[0:early, :], w_down_ref))
        return [pltpu.make_async_copy(src, dst, w_sem.at[i]) for i, (src, dst) in enumerate(pairs)]

    @pl.when(g == 0)
    def _project_and_gate():
        cw = CHUNK_WIDTH
        for copy in late_weight_copies():
            copy.start()
        _fill_vector_table(vec_ref, g_pre, g_pm, g_pff, g_pf, ln_g, ln_b, g_oc, g_oa)
        h = _rms(x_ref[...].reshape(n, d), vec_ref[ROW_G_PRE:ROW_G_PRE + 1, :d]).astype(BF16)
        u = jax.nn.gelu(_dot(h, w_in_ref[:, 0:cw]))
        v = _layer_norm(jax.nn.gelu(_dot(h, w_in_ref[:, cw:2 * cw])),
                        vec_ref[ROW_LN:ROW_LN + 1, 0:cw], vec_ref[ROW_LN:ROW_LN + 1, cw:2 * cw])
        vn_ref[...] = v.reshape(vn_ref.shape)
        q_s[...] = _dot(h, w_in_ref[:, 2 * cw:2 * cw + ATTN_WIDTH]) * ATTN_SCALE
        kv0 = 2 * cw + ATTN_WIDTH
        kvt_s[...] = _dot(h, w_in_ref[:, kv0:kv0 + 2 * KV_WIDTH]).T
        t_of_row = lax.broadcasted_iota(jnp.int32, (n, 1), 0) % L
        group_of_lane = lax.broadcasted_iota(jnp.int32, (1, cw), 1) // HEAD_DIM

        def per_group(value_of_group):
            row = jnp.zeros((1, cw), F32)
            for grp in range(N_CHUNK_HEADS):
                row = jnp.where(group_of_lane == grp, value_of_group(grp), row)
            return row

        s = jnp.zeros(v.shape, F32)
        for t in range(L):
            s = jnp.where(t_of_row == t, per_group(lambda grp: b4_ref[layer, grp, t]), s)
        for delta in range(L):
            coef = jnp.zeros(v.shape, F32)
            for t in range(delta, L):
                w_row = per_group(lambda grp: w4_ref[grp, t, t - delta])
                coef = jnp.where(t_of_row == t, w_row, coef)
            vs = v if delta == 0 else pltpu.roll(v, delta, axis=0)
            s = s + coef * vs
        a_s[...] = u * s

    rows_all = n_heads * pair_rows
    r_idx = lax.broadcasted_iota(jnp.int32, (rows_all, LANES), 0)
    l_idx = lax.broadcasted_iota(jnp.int32, (rows_all, LANES), 1)
    t_idx = r_idx % L
    odd = (r_idx % pair_rows) >= L
    head_of_row = lax.broadcasted_iota(jnp.int32, (rows_all, 1), 0) // pair_rows
    sink = jnp.zeros((rows_all, 1), F32)
    slope = jnp.zeros((rows_all, 1), F32)
    for hh in range(n_heads):
        sink = jnp.where(head_of_row == hh, sink_ref[layer, hh // Q_PER_KV, hh % Q_PER_KV], sink)
        slope = jnp.where(head_of_row == hh, _alibi_slope(hh), slope)
    dist_old = W + t_idx - l_idx
    bias_old = jnp.where(dist_old <= WINDOW, -(slope * dist_old.astype(F32)), -jnp.inf)
    lane8 = lax.broadcasted_iota(jnp.int32, (pair_rows, LANES), 1)
    lo8 = lane8 < HALF
    lane_w = lax.broadcasted_iota(jnp.int32, (hd, W), 1)

    first_batch = g * bb
    new_block = pl.multiple_of((first_batch // batch_per_block) * LANES, LANES)
    kt_new = kvt_s[0:KV_WIDTH, pl.ds(new_block, LANES)]
    vt_new = kvt_s[KV_WIDTH:2 * KV_WIDTH, pl.ds(new_block, LANES)]
    kt_new_b = kt_new.astype(BF16)
    vt_new_b = vt_new.astype(BF16)
    n_pairs = bb // 2
    offs = [((first_batch + 2 * p) % batch_per_block) * L for p in range(n_pairs)]
    rows0 = [pl.multiple_of((first_batch + 2 * p) * L, pair_rows) for p in range(n_pairs)]
    keep = lane_w < W - L

    scores = []
    for p in range(n_pairs):
        tiles = []
        for m in range(N_PAIRS):
            qb = q_s[pl.ds(rows0[p], pair_rows), m * LANES:(m + 1) * LANES]
            qb_sw = pltpu.roll(qb, HALF, axis=1)
            if (2 * m) // Q_PER_KV == 0:
                tiles += [jnp.where(lo8, qb, 0.0), jnp.where(lo8, qb_sw, 0.0)]
            else:
                tiles += [jnp.where(lo8, 0.0, qb_sw), jnp.where(lo8, 0.0, qb)]
        lhs = jnp.concatenate(tiles, axis=0).astype(BF16)
        s_pair = []
        for e in range(2):
            rhs = jnp.concatenate([ckt_ref[2 * p + e].astype(BF16), kt_new_b], axis=1)
            s_pair.append(_dot(lhs, rhs))
        scores.append(jnp.where(jnp.concatenate([odd, odd], axis=1), s_pair[1], s_pair[0]))
        for e in range(2):
            wkt_ref[2 * p + e] = jnp.where(
                keep, pltpu.roll(ckt_ref[2 * p + e], W - L, axis=1),
                pltpu.roll(kt_new, (W - L) - (offs[p] + e * L), axis=1))

    weights = []
    for p in range(n_pairs):
        rel = l_idx - offs[p] - jnp.where(odd, L, 0)
        bias_new = jnp.where((rel >= 0) & (rel <= t_idx),
                             -(slope * (t_idx - rel).astype(F32)), -jnp.inf)
        sc = scores[p] + jnp.concatenate([bias_old, bias_new], axis=1)
        mx = jnp.maximum(jnp.max(sc, axis=-1, keepdims=True), sink)
        p_un = jnp.exp(sc - mx)
        z = jnp.sum(p_un, axis=-1, keepdims=True) + jnp.exp(sink - mx)
        weights.append((p_un.astype(BF16), z))

    for p in range(n_pairs):
        pb, z = weights[p]
        o_pair = []
        for e in range(2):
            vt = jnp.concatenate([cvt_ref[2 * p + e].astype(BF16), vt_new_b], axis=1)
            o_pair.append(lax.dot_general(pb, vt, (((1,), (1,)), ((), ())),
                                          preferred_element_type=F32))
        o = jnp.where(odd, o_pair[1], o_pair[0]) * (1.0 / z)
        for m in range(N_PAIRS):
            kv_head = (2 * m) // Q_PER_KV
            o_even = o[(2 * m) * pair_rows:(2 * m + 1) * pair_rows]
            o_odd = o[(2 * m + 1) * pair_rows:(2 * m + 2) * pair_rows]
            if kv_head == 0:
                o_odd = pltpu.roll(o_odd, HALF, axis=1)
            else:
                o_even = pltpu.roll(o_even, HALF, axis=1)
            b_s[pl.ds(rows0[p], pair_rows), m * LANES:(m + 1) * LANES] = (
                jnp.where(lo8, o_even, o_odd))
        for e in range(2):
            wvt_ref[2 * p + e] = jnp.where(
                keep, pltpu.roll(cvt_ref[2 * p + e], W - L, axis=1),
                pltpu.roll(vt_new, (W - L) - (offs[p] + e * L), axis=1))

    @pl.when(g == n_steps - 1)
    def _merge_and_ffn():
        for copy in late_weight_copies():
            copy.wait()
        x1 = _merge(x_ref[...].reshape(n, d), a_s[...], b_s[...], w_o_ref, vec_ref)
        x1o_ref[...] = x1
        h2 = _rms(x1, vec_ref[ROW_G_PFF:ROW_G_PFF + 1, :d]).astype(BF16)
        f0 = jnp.zeros(x1.shape, F32)
        for c in range(EARLY_CHUNKS):
            up = _dot(h2, w_up_ref[:, c * FF_CHUNK:(c + 1) * FF_CHUNK])
            f0 = f0 + _dot(_relu_sq(up), w_down_ref[c * FF_CHUNK:(c + 1) * FF_CHUNK, :])
        f0o_ref[...] = f0


def _sample_call(x, layer, sinks, w4, b_sp, cache_kt, cache_vt, w_in, w_o, w_up, w_down,
                 norms):
    Bd, L, D = x.shape
    hd, W = cache_kt.shape[1:]
    n = Bd * L
    bb = SAMPLE_BB
    assert Bd % bb == 0 and bb % 2 == 0 and (LANES // L) % bb == 0
    smem = pl.BlockSpec(memory_space=pltpu.SMEM)
    blk = pl.BlockSpec((bb, hd, W), lambda g: (g, 0, 0))
    const = lambda shape: pl.BlockSpec(shape, lambda g: (0,) * len(shape))
    late_weights = (w_o, w_up, w_down)
    in_hbm = pl.BlockSpec(memory_space=pl.ANY)
    return pl.pallas_call(
        functools.partial(_sample_kernel, L, layer),
        grid=(Bd // bb,),
        in_specs=[smem, smem, smem, _resident(x.shape), blk, blk, _resident(w_in.shape)] +
                 [in_hbm for _ in late_weights] + [_layer_row(a, layer) for a in norms],
        out_specs=[const((n, D)), const((n, D)), const((Bd, L, CHUNK_WIDTH)), blk, blk],
        out_shape=[jax.ShapeDtypeStruct((n, D), F32),
                   jax.ShapeDtypeStruct((n, D), F32),
                   jax.ShapeDtypeStruct((Bd, L, CHUNK_WIDTH), F32),
                   jax.ShapeDtypeStruct(cache_kt.shape, F32),
                   jax.ShapeDtypeStruct(cache_vt.shape, F32)],
        scratch_shapes=[pltpu.VMEM((8, D), F32),
                        pltpu.VMEM((n, ATTN_WIDTH), F32),
                        pltpu.VMEM((2 * KV_WIDTH, n), F32),
                        pltpu.VMEM((n, CHUNK_WIDTH), F32),
                        pltpu.VMEM((n, ATTN_WIDTH), F32)] +
                       [pltpu.VMEM(w_o.shape, w_o.dtype),
                        pltpu.VMEM((D, EARLY_CHUNKS * FF_CHUNK), w_up.dtype),
                        pltpu.VMEM((EARLY_CHUNKS * FF_CHUNK, D), w_down.dtype)] +
                       [pltpu.SemaphoreType.DMA((len(late_weights),))],
        compiler_params=pltpu.CompilerParams(
            dimension_semantics=("arbitrary",), vmem_limit_bytes=VMEM_LIMIT),
        name="sample_layer",
    )(sinks, w4, b_sp, x, cache_kt, cache_vt, w_in, *late_weights, *norms)


def _heads_last(t):
    b, _, w = t.shape
    return t.reshape(b, N_KV_HEADS, HEAD_DIM, w).transpose(0, 3, 1, 2)


def kernel(x_prompt, x_sample, cache_win_k, cache_win_v, w_in, g_pre_mix, ln_v_g, ln_v_b,
           w_spatial, b_spatial, attn_sinks, g_out_chunk, g_out_attn, w_o, g_post_mix,
           g_pre_ffn, w_up, w_down, g_post_ffn):
    depth = w_in.shape[0]
    Bd, L, D = x_sample.shape
    W = cache_win_k.shape[2]
    norms = (g_pre_mix, g_post_mix, g_pre_ffn, g_post_ffn, ln_v_g, ln_v_b, g_out_chunk,
             g_out_attn)
    yp, ys = x_prompt, x_sample
    wk_p, wv_p, cv_p, wk_s, wv_s, cv_s = [], [], [], [], [], []
    for l in range(depth):
        w_in_b = w_in[l].astype(BF16)
        w_o_b = w_o[l].astype(BF16)
        w_up_b = w_up[l].astype(BF16)
        w_down_b = w_down[l].astype(BF16)

        w4 = w_spatial[l, :, :L, :L]
        cache_kt = cache_win_k[l].transpose(0, 2, 3, 1).reshape(Bd, KV_WIDTH, W)
        cache_vt = cache_win_v[l].transpose(0, 2, 3, 1).reshape(Bd, KV_WIDTH, W)
        x1_s, f0_s, cv, wk, wv = _sample_call(ys, l, attn_sinks, w4, b_spatial, cache_kt,
                                              cache_vt, w_in_b, w_o_b, w_up_b, w_down_b, norms)
        wk_s.append(_heads_last(wk))
        wv_s.append(_heads_last(wv))
        cv_s.append(cv)

        yp, ys, wk, wv, cv = _prompt_call(yp, x1_s, f0_s, ys.shape, l, attn_sinks, w_in_b, w_o_b,
                                          w_up_b, w_down_b, norms, w_spatial, b_spatial)
        wk_p.append(_heads_last(wk))
        wv_p.append(_heads_last(wv))
        cv_p.append(cv)
    return (yp, ys, jnp.stack(wk_p), jnp.stack(wv_p), jnp.stack(cv_p),
            jnp.stack(wk_s), jnp.stack(wv_s), jnp.stack(cv_s))
```

```python
import functools

import jax
import jax.numpy as jnp
from jax import lax
from jax.experimental import pallas as pl
from jax.experimental.pallas import tpu as pltpu

F32 = jnp.float32
BF16 = jnp.bfloat16

HEAD_DIM = 64
N_CHUNK_HEADS = 8
N_ATTN_HEADS = 8
N_KV_HEADS = 2
Q_PER_KV = N_ATTN_HEADS // N_KV_HEADS
CHUNK_WIDTH = N_CHUNK_HEADS * HEAD_DIM
ATTN_WIDTH = N_ATTN_HEADS * HEAD_DIM
KV_WIDTH = N_KV_HEADS * HEAD_DIM
CHUNK = 128
WINDOW = 128
EPS = 1e-6
ATTN_SCALE = HEAD_DIM ** -0.5

LANES = 128
HALF = LANES // 2
N_PAIRS = ATTN_WIDTH // LANES
VMEM_LIMIT = 56 * 1024 * 1024

PROMPT_TILE = 512
FF_CHUNK = 512
EARLY_CHUNKS = 1
SAMPLE_BB = 16

ROW_G_PRE, ROW_G_PM, ROW_G_PFF, ROW_G_PF, ROW_LN, ROW_GOUT = range(6)


def _dot(a, b):
    return jnp.dot(a, b, preferred_element_type=F32)


def _rms(x, g):
    return x * lax.rsqrt(jnp.mean(x * x, axis=-1, keepdims=True) + EPS) * g


def _layer_norm(x, g, b):
    xc = x - jnp.mean(x, axis=-1, keepdims=True)
    return xc * lax.rsqrt(jnp.mean(xc * xc, axis=-1, keepdims=True) + EPS) * g + b


def _alibi_slope(head):
    return 2.0 ** (-8.0 * (head + 1) / N_ATTN_HEADS)


def _merge(x, a_out, b_out, w_o_ref, vec_ref):
    d = x.shape[-1]
    cw = CHUNK_WIDTH
    ra = _rms(a_out, vec_ref[ROW_GOUT:ROW_GOUT + 1, 0:cw]).astype(BF16)
    rb = _rms(b_out, vec_ref[ROW_GOUT:ROW_GOUT + 1, cw:cw + ATTN_WIDTH]).astype(BF16)
    o = _dot(ra, w_o_ref[0:cw, :]) + _dot(rb, w_o_ref[cw:cw + ATTN_WIDTH, :])
    return x + _rms(o, vec_ref[ROW_G_PM:ROW_G_PM + 1, :d])


def _relu_sq(up):
    r = jnp.maximum(up, 0.0)
    return (r * r).astype(BF16)


def _ffn(x1, w_up_ref, w_down_ref, vec_ref):
    d = x1.shape[-1]
    h2 = _rms(x1, vec_ref[ROW_G_PFF:ROW_G_PFF + 1, :d]).astype(BF16)
    d_ff = w_up_ref.shape[1]
    f = jnp.zeros(x1.shape, F32)
    for c in range(d_ff // FF_CHUNK):
        up = _dot(h2, w_up_ref[:, c * FF_CHUNK:(c + 1) * FF_CHUNK])
        f = f + _dot(_relu_sq(up), w_down_ref[c * FF_CHUNK:(c + 1) * FF_CHUNK, :])
    return x1 + _rms(f, vec_ref[ROW_G_PF:ROW_G_PF + 1, :d])


def _fill_vector_table(vec_s, g_pre, g_pm, g_pff, g_pf, ln_g, ln_b, g_oc, g_oa):
    d = g_pre.shape[-1]
    cw = CHUNK_WIDTH
    vec_s[ROW_G_PRE:ROW_G_PRE + 1, 0:d] = g_pre[...]
    vec_s[ROW_G_PM:ROW_G_PM + 1, 0:d] = g_pm[...]
    vec_s[ROW_G_PFF:ROW_G_PFF + 1, 0:d] = g_pff[...]
    vec_s[ROW_G_PF:ROW_G_PF + 1, 0:d] = g_pf[...]
    vec_s[ROW_LN:ROW_LN + 1, 0:cw] = ln_g[...]
    vec_s[ROW_LN:ROW_LN + 1, cw:2 * cw] = ln_b[...]
    vec_s[ROW_GOUT:ROW_GOUT + 1, 0:cw] = g_oc[...]
    vec_s[ROW_GOUT:ROW_GOUT + 1, cw:cw + ATTN_WIDTH] = g_oa[...]


def _prompt_kernel(n_tiles, nj, layer, sink_ref, x_ref, x1in_ref, f0in_ref, w_in_ref, w_o_ref,
                   w_up_ref, w_down_ref,
                   g_pre, g_pm, g_pff, g_pf, ln_g, ln_b, g_oc, g_oa, wsp_ref, bsp_ref,
                   y_ref, ys_ref, wk_ref, wv_ref, cv_ref,
                   vec_ref, bias_s, wcat_s, tab_s, kt_s, vd_s, a_s, bo_s, m_s, x1_s, x1n_s,
                   h_s, q_s, h2_s, act_s, f_s, carry_sem):
    step = pl.program_id(0)
    j = jnp.minimum(step, n_tiles - 1) % nj
    T, d = x_ref.shape[1], x_ref.shape[2]
    nb = T // WINDOW
    cw = CHUNK_WIDTH
    n_chunks = w_up_ref.shape[1] // FF_CHUNK

    lane = lax.broadcasted_iota(jnp.int32, (WINDOW, LANES), 1)
    lo = lane < HALF

    @pl.when(step == 0)
    def _init_tables():
        _fill_vector_table(vec_ref, g_pre, g_pm, g_pff, g_pf, ln_g, ln_b, g_oc, g_oa)
        carry_copies = [pltpu.make_async_copy(x1in_ref, x1_s, carry_sem.at[0]),
                        pltpu.make_async_copy(f0in_ref, f_s, carry_sem.at[1])]
        for copy in carry_copies:
            copy.start(priority=1)
        t_idx = lax.broadcasted_iota(jnp.int32, (CHUNK, CHUNK), 0)
        s_idx = lax.broadcasted_iota(jnp.int32, (CHUNK, CHUNK), 1)
        for g in range(N_CHUNK_HEADS):
            wm = jnp.where(s_idx <= t_idx, wsp_ref[0, g], 0.0).astype(BF16)
            wcat_s[g // 2, :, (g % 2) * CHUNK:(g % 2 + 1) * CHUNK] = wm
        b_rows = jnp.concatenate(
            [bsp_ref[0], jnp.zeros((CHUNK - N_CHUNK_HEADS, CHUNK), F32)], axis=0)
        b_cols = b_rows.T
        for p in range(cw // LANES):
            bias_s[:, p * LANES:(p + 1) * LANES] = jnp.where(
                lo, jnp.broadcast_to(b_cols[:, 2 * p:2 * p + 1], (CHUNK, LANES)),
                jnp.broadcast_to(b_cols[:, 2 * p + 1:2 * p + 2], (CHUNK, LANES)))
        a_idx = lax.broadcasted_iota(jnp.int32, (WINDOW, 2 * WINDOW), 0)
        c_idx = lax.broadcasted_iota(jnp.int32, (WINDOW, 2 * WINDOW), 1)
        dist = WINDOW + a_idx - c_idx
        valid = (dist >= 0) & (dist <= WINDOW)
        for hd in range(N_ATTN_HEADS):
            bias = jnp.where(valid, -(_alibi_slope(hd) * dist.astype(F32)), -jnp.inf)
            r = hd % Q_PER_KV
            tab_s[hd // Q_PER_KV, r * WINDOW:(r + 1) * WINDOW, :] = bias
        for copy in carry_copies:
            copy.wait()
        h2_s[...] = _rms(x1_s[...], vec_ref[ROW_G_PFF:ROW_G_PFF + 1, :d]).astype(BF16)

    @pl.when(j == 0)
    def _clear_carry():
        kt_s[:, 0:WINDOW] = jnp.zeros((2 * LANES, WINDOW), BF16)
        vd_s[:, 0:WINDOW, :] = jnp.zeros((N_KV_HEADS, WINDOW, LANES), BF16)

    def ffn_up(c):
        act_s[...] = _relu_sq(_dot(h2_s[...], w_up_ref[:, c * FF_CHUNK:(c + 1) * FF_CHUNK]))

    def ffn_down(c):
        part = _dot(act_s[...], w_down_ref[c * FF_CHUNK:(c + 1) * FF_CHUNK, :])
        if c == 0:
            f_s[...] = part
        else:
            f_s[...] += part

    def finish_previous():
        y_ref[0] = x1_s[...] + _rms(f_s[...], vec_ref[ROW_G_PF:ROW_G_PF + 1, :d])

    def tile_step(with_late):
        late = iter([functools.partial(fn, c) for c in range(EARLY_CHUNKS, n_chunks)
                     for fn in (ffn_up, ffn_down)])

        def late_stage(count=1):
            if with_late:
                for _ in range(count):
                    next(late)()

        late_stage()
        h_s[...] = _rms(x_ref[0], vec_ref[ROW_G_PRE:ROW_G_PRE + 1, :d]).astype(BF16)
        u_pre = _dot(h_s[...], w_in_ref[:, 0:cw])
        v_pre = _dot(h_s[...], w_in_ref[:, cw:2 * cw])
        late_stage()
        u = jax.nn.gelu(u_pre)
        late_stage()
        v = _layer_norm(jax.nn.gelu(v_pre), vec_ref[ROW_LN:ROW_LN + 1, 0:cw],
                        vec_ref[ROW_LN:ROW_LN + 1, cw:2 * cw])
        kv0 = 2 * cw + ATTN_WIDTH
        kv = _dot(h_s[...], w_in_ref[:, kv0:kv0 + 2 * KV_WIDTH])
        k = kv[:, 0:KV_WIDTH]
        val = kv[:, KV_WIDTH:2 * KV_WIDTH]
        q0 = 2 * cw
        q_s[...] = _dot(h_s[...], w_in_ref[:, q0:q0 + ATTN_WIDTH]) * ATTN_SCALE
        late_stage()

        kt_f = k.T
        wk_ref[0] = kt_f[:, T - WINDOW:]
        wv_ref[0] = val[T - WINDOW:, :].T
        cv_ref[0] = v[T - CHUNK:, :]

        assert nb % 2 == 0
        for c in range(0, nb, 2):
            for p in range(cw // LANES):
                cols = slice(p * LANES, (p + 1) * LANES)
                halves = []
                for cc in (c, c + 1):
                    vcp = v[cc * CHUNK:(cc + 1) * CHUNK, cols]
                    halves.append(jnp.concatenate([jnp.where(lo, vcp, 0.0), jnp.where(lo, 0.0, vcp)],
                                                  axis=0))
                rhs = jnp.concatenate(halves, axis=1).astype(BF16)
                s_pair = _dot(wcat_s[p], rhs)
                for side, cc in enumerate((c, c + 1)):
                    rows = slice(cc * CHUNK, (cc + 1) * CHUNK)
                    s_cp = s_pair[:, side * LANES:(side + 1) * LANES] + bias_s[:, cols]
                    a_s[rows, cols] = u[rows, cols] * s_cp
        m_s[:, 0:cw] = _rms(a_s[...], vec_ref[ROW_GOUT:ROW_GOUT + 1, 0:cw]).astype(BF16)

        kt = kt_f.astype(BF16)
        for hk in range(N_KV_HEADS):
            kth = kt[hk * HEAD_DIM:(hk + 1) * HEAD_DIM, :]
            kt_s[hk * LANES:hk * LANES + HEAD_DIM, WINDOW:] = kth
            kt_s[hk * LANES + HEAD_DIM:(hk + 1) * LANES, WINDOW:] = kth
        lane_t = lax.broadcasted_iota(jnp.int32, (T, LANES), 1)
        lo_t = lane_t < HALF
        val_sw = pltpu.roll(val, HALF, axis=1)
        vd_s[0, WINDOW:, :] = jnp.where(lo_t, val, val_sw).astype(BF16)
        vd_s[1, WINDOW:, :] = jnp.where(lo_t, val_sw, val).astype(BF16)

        first_block_bias = jnp.where(j == 0, -jnp.inf, 0.0).astype(F32)
        head_of_row = lax.broadcasted_iota(jnp.int32, (Q_PER_KV * WINDOW, 1), 0) // WINDOW
        n_units = nb * N_KV_HEADS
        n_mid = 2 * (n_chunks - EARLY_CHUNKS) - 9
        late_after_scores = tuple(1 + (k * n_units) // n_mid for k in range(n_mid))
        for n, (i, hk) in enumerate((i, hk) for i in range(nb) for hk in range(N_KV_HEADS)):
            rows = slice(i * WINDOW, (i + 1) * WINDOW)
            keys = slice(i * WINDOW, (i + 2) * WINDOW)
            tiles = []
            for m in range(hk * 2, hk * 2 + 2):
                qb = q_s[rows, m * LANES:(m + 1) * LANES]
                tiles += [jnp.where(lo, qb, 0.0), jnp.where(lo, 0.0, qb)]
            qq = jnp.concatenate(tiles, axis=0).astype(BF16)
            s = _dot(qq, kt_s[hk * LANES:(hk + 1) * LANES, keys]) + tab_s[hk]
            if i == 0:
                s = jnp.concatenate([s[:, 0:WINDOW] + first_block_bias, s[:, WINDOW:]], axis=1)
            if n in late_after_scores:
                late_stage()
            sink = jnp.zeros((Q_PER_KV * WINDOW, 1), F32)
            for r in range(Q_PER_KV):
                sink = jnp.where(head_of_row == r, sink_ref[layer, hk, r], sink)
            mx = jnp.maximum(jnp.max(s, axis=-1, keepdims=True), sink)
            p_un = jnp.exp(s - mx)
            z = jnp.sum(p_un, axis=-1, keepdims=True) + jnp.exp(sink - mx)
            o = _dot(p_un.astype(BF16), vd_s[hk, keys, :]) * (1.0 / z)
            for pair in range(Q_PER_KV // 2):
                m = hk * 2 + pair
                o_even = o[(2 * pair) * WINDOW:(2 * pair + 1) * WINDOW]
                o_odd = o[(2 * pair + 1) * WINDOW:(2 * pair + 2) * WINDOW]
                bo_s[rows, m * LANES:(m + 1) * LANES] = jnp.where(lo, o_even, o_odd)

        kt_s[:, 0:WINDOW] = kt_s[:, T:T + WINDOW]
        vd_s[:, 0:WINDOW, :] = vd_s[:, T:T + WINDOW, :]

        late_stage()
        m_s[:, cw:cw + ATTN_WIDTH] = _rms(
            bo_s[...], vec_ref[ROW_GOUT:ROW_GOUT + 1, cw:cw + ATTN_WIDTH]).astype(BF16)
        o_proj = _dot(m_s[...], w_o_ref[...])
        late_stage(3)
        x1 = x_ref[0] + _rms(o_proj, vec_ref[ROW_G_PM:ROW_G_PM + 1, :d])
        x1n_s[...] = x1
        h2_s[...] = _rms(x1, vec_ref[ROW_G_PFF:ROW_G_PFF + 1, :d]).astype(BF16)
        late_stage()

        ffn_up(0)
        if with_late:
            finish_previous()
        x1_s[...] = x1n_s[...]
        ffn_down(0)
        for c in range(1, EARLY_CHUNKS):
            ffn_up(c)
            ffn_down(c)

    @pl.when(step < n_tiles)
    def _tile_step():
        tile_step(True)

    @pl.when(step == 0)
    def _emit_sample():
        ys_ref[...] = y_ref[0].reshape(ys_ref.shape)

    @pl.when(step == n_tiles)
    def _drain_step():
        for c in range(EARLY_CHUNKS, n_chunks):
            ffn_up(c)
            ffn_down(c)
        finish_previous()


def _resident(shape):
    return pl.BlockSpec(shape, lambda *_: (0,) * len(shape), pipeline_mode=pl.Buffered(1))


def _layer_row(arr, layer):
    block = (1,) + arr.shape[1:]
    return pl.BlockSpec(block, lambda *_: (layer,) + (0,) * (arr.ndim - 1),
                        pipeline_mode=pl.Buffered(1))


def _prompt_call(x, x1_sample, f0_sample, sample_shape, layer, sinks, w_in, w_o, w_up, w_down,
                 norms, w_sp, b_sp):
    B, S, D = x.shape
    T = PROMPT_TILE
    assert x1_sample.shape == (T, D) and f0_sample.shape == (T, D)
    assert S % T == 0 and T % WINDOW == 0 and WINDOW == CHUNK
    assert 1 <= EARLY_CHUNKS and 2 * (w_up.shape[1] // FF_CHUNK - EARLY_CHUNKS) >= 9
    nj = S // T
    n_tiles = B * nj
    cur = lambda s: jnp.minimum(s, n_tiles - 1)
    prev = lambda s: jnp.maximum(s - 1, 0)
    tile_in = pl.BlockSpec((1, T, D), lambda s: (cur(s) // nj, cur(s) % nj, 0))
    tile_out = pl.BlockSpec((1, T, D), lambda s: (prev(s) // nj, prev(s) % nj, 0))
    last = lambda r, c: pl.BlockSpec((1, r, c), lambda s: (cur(s) // nj, 0, 0))
    return pl.pallas_call(
        functools.partial(_prompt_kernel, n_tiles, nj, layer),
        grid=(n_tiles + 1,),
        in_specs=[pl.BlockSpec(memory_space=pltpu.SMEM), tile_in,
                  pl.BlockSpec(memory_space=pl.ANY), pl.BlockSpec(memory_space=pl.ANY),
                  _resident(w_in.shape), _resident(w_o.shape), _resident(w_up.shape),
                  _resident(w_down.shape)] + [_layer_row(a, layer) for a in norms] +
                 [_layer_row(w_sp, layer), _layer_row(b_sp, layer)],
        out_specs=[tile_out, pl.BlockSpec(sample_shape, lambda s: (0,) * len(sample_shape)),
                   last(KV_WIDTH, WINDOW), last(KV_WIDTH, WINDOW), last(CHUNK, CHUNK_WIDTH)],
        out_shape=[jax.ShapeDtypeStruct((B, S, D), F32),
                   jax.ShapeDtypeStruct(sample_shape, F32),
                   jax.ShapeDtypeStruct((B, KV_WIDTH, WINDOW), F32),
                   jax.ShapeDtypeStruct((B, KV_WIDTH, WINDOW), F32),
                   jax.ShapeDtypeStruct((B, CHUNK, CHUNK_WIDTH), F32)],
        scratch_shapes=[pltpu.VMEM((8, D), F32),
                        pltpu.VMEM((CHUNK, CHUNK_WIDTH), F32),
                        pltpu.VMEM((N_CHUNK_HEADS // 2, CHUNK, 2 * CHUNK), BF16),
                        pltpu.VMEM((N_KV_HEADS, Q_PER_KV * WINDOW, 2 * WINDOW), F32),
                        pltpu.VMEM((2 * LANES, T + WINDOW), BF16),
                        pltpu.VMEM((N_KV_HEADS, T + WINDOW, LANES), BF16),
                        pltpu.VMEM((T, CHUNK_WIDTH), F32),
                        pltpu.VMEM((T, ATTN_WIDTH), F32),
                        pltpu.VMEM((T, CHUNK_WIDTH + ATTN_WIDTH), BF16),
                        pltpu.VMEM((T, D), F32),
                        pltpu.VMEM((T, D), F32),
                        pltpu.VMEM((T, D), BF16),
                        pltpu.VMEM((T, ATTN_WIDTH), F32),
                        pltpu.VMEM((T, D), BF16),
                        pltpu.VMEM((T, FF_CHUNK), BF16),
                        pltpu.VMEM((T, D), F32),
                        pltpu.SemaphoreType.DMA((2,))],
        compiler_params=pltpu.CompilerParams(
            dimension_semantics=("arbitrary",), vmem_limit_bytes=VMEM_LIMIT),
        name="prompt_layer",
    )(sinks, x, x1_sample, f0_sample, w_in, w_o, w_up, w_down, *norms, w_sp, b_sp)


def _sample_kernel(L, layer, sink_ref, w4_ref, b4_ref, x_ref, ckt_ref, cvt_ref, w_in_ref,
                   w_o_hbm, w_up_hbm, w_down_hbm,
                   g_pre, g_pm, g_pff, g_pf, ln_g, ln_b, g_oc, g_oa,
                   x1o_ref, f0o_ref, vn_ref, wkt_ref, wvt_ref,
                   vec_ref, q_s, kvt_s, a_s, b_s, w_o_ref, w_up_ref, w_down_ref, w_sem):
    g = pl.program_id(0)
    n_steps = pl.num_programs(0)
    bb, hd, W = ckt_ref.shape
    n, d = q_s.shape[0], x_ref.shape[-1]
    n_heads = N_ATTN_HEADS
    pair_rows = 2 * L
    assert pair_rows == 8 and hd == LANES and W == LANES
    batch_per_block = LANES // L

    def late_weight_copies():
        early = w_up_ref.shape[1]
        pairs = ((w_o_hbm, w_o_ref), (w_up_hbm.at[:, 0:early], w_up_ref),
                 (w_down_hbm.at[0:early, :], w_down_ref))
        return [pltpu.make_async_copy(src, dst, w_sem.at[i]) for i, (src, dst) in enumerate(pairs)]

    @pl.when(g == 0)
    def _project_and_gate():
        cw = CHUNK_WIDTH
        for copy in late_weight_copies():
            copy.start(priority=1)
        _fill_vector_table(vec_ref, g_pre, g_pm, g_pff, g_pf, ln_g, ln_b, g_oc, g_oa)
        h = _rms(x_ref[...].reshape(n, d), vec_ref[ROW_G_PRE:ROW_G_PRE + 1, :d]).astype(BF16)
        u = jax.nn.gelu(_dot(h, w_in_ref[:, 0:cw]))
        v = _layer_norm(jax.nn.gelu(_dot(h, w_in_ref[:, cw:2 * cw])),
                        vec_ref[ROW_LN:ROW_LN + 1, 0:cw], vec_ref[ROW_LN:ROW_LN + 1, cw:2 * cw])
        vn_ref[...] = v.reshape(vn_ref.shape)
        q_s[...] = _dot(h, w_in_ref[:, 2 * cw:2 * cw + ATTN_WIDTH]) * ATTN_SCALE
        kv0 = 2 * cw + ATTN_WIDTH
        kvt_s[...] = _dot(h, w_in_ref[:, kv0:kv0 + 2 * KV_WIDTH]).T
        t_of_row = lax.broadcasted_iota(jnp.int32, (n, 1), 0) % L
        group_of_lane = lax.broadcasted_iota(jnp.int32, (1, cw), 1) // HEAD_DIM

        def per_group(value_of_group):
            row = jnp.zeros((1, cw), F32)
            for grp in range(N_CHUNK_HEADS):
                row = jnp.where(group_of_lane == grp, value_of_group(grp), row)
            return row

        s = jnp.zeros(v.shape, F32)
        for t in range(L):
            s = jnp.where(t_of_row == t, per_group(lambda grp: b4_ref[layer, grp, t]), s)
        for delta in range(L):
            coef = jnp.zeros(v.shape, F32)
            for t in range(delta, L):
                w_row = per_group(lambda grp: w4_ref[grp, t, t - delta])
                coef = jnp.where(t_of_row == t, w_row, coef)
            vs = v if delta == 0 else pltpu.roll(v, delta, axis=0)
            s = s + coef * vs
        a_s[...] = u * s

    rows_all = n_heads * pair_rows
    r_idx = lax.broadcasted_iota(jnp.int32, (rows_all, LANES), 0)
    l_idx = lax.broadcasted_iota(jnp.int32, (rows_all, LANES), 1)
    t_idx = r_idx % L
    odd = (r_idx % pair_rows) >= L
    head_of_row = lax.broadcasted_iota(jnp.int32, (rows_all, 1), 0) // pair_rows
    sink = jnp.zeros((rows_all, 1), F32)
    slope = jnp.zeros((rows_all, 1), F32)
    for hh in range(n_heads):
        sink = jnp.where(head_of_row == hh, sink_ref[layer, hh // Q_PER_KV, hh % Q_PER_KV], sink)
        slope = jnp.where(head_of_row == hh, _alibi_slope(hh), slope)
    dist_old = W + t_idx - l_idx
    bias_old = jnp.where(dist_old <= WINDOW, -(slope * dist_old.astype(F32)), -jnp.inf)
    lane8 = lax.broadcasted_iota(jnp.int32, (pair_rows, LANES), 1)
    lo8 = lane8 < HALF
    lane_w = lax.broadcasted_iota(jnp.int32, (hd, W), 1)

    first_batch = g * bb
    new_block = pl.multiple_of((first_batch // batch_per_block) * LANES, LANES)
    kt_new = kvt_s[0:KV_WIDTH, pl.ds(new_block, LANES)]
    vt_new = kvt_s[KV_WIDTH:2 * KV_WIDTH, pl.ds(new_block, LANES)]
    kt_new_b = kt_new.astype(BF16)
    vt_new_b = vt_new.astype(BF16)
    n_pairs = bb // 2
    offs = [((first_batch + 2 * p) % batch_per_block) * L for p in range(n_pairs)]
    rows0 = [pl.multiple_of((first_batch + 2 * p) * L, pair_rows) for p in range(n_pairs)]
    keep = lane_w < W - L

    scores = []
    for p in range(n_pairs):
        tiles = []
        for m in range(N_PAIRS):
            qb = q_s[pl.ds(rows0[p], pair_rows), m * LANES:(m + 1) * LANES]
            qb_sw = pltpu.roll(qb, HALF, axis=1)
            if (2 * m) // Q_PER_KV == 0:
                tiles += [jnp.where(lo8, qb, 0.0), jnp.where(lo8, qb_sw, 0.0)]
            else:
                tiles += [jnp.where(lo8, 0.0, qb_sw), jnp.where(lo8, 0.0, qb)]
        lhs = jnp.concatenate(tiles, axis=0).astype(BF16)
        s_pair = []
        for e in range(2):
            rhs = jnp.concatenate([ckt_ref[2 * p + e].astype(BF16), kt_new_b], axis=1)
            s_pair.append(_dot(lhs, rhs))
        scores.append(jnp.where(jnp.concatenate([odd, odd], axis=1), s_pair[1], s_pair[0]))
        for e in range(2):
            wkt_ref[2 * p + e] = jnp.where(
                keep, pltpu.roll(ckt_ref[2 * p + e], W - L, axis=1),
                pltpu.roll(kt_new, (W - L) - (offs[p] + e * L), axis=1))

    weights = []
    for p in range(n_pairs):
        rel = l_idx - offs[p] - jnp.where(odd, L, 0)
        bias_new = jnp.where((rel >= 0) & (rel <= t_idx),
                             -(slope * (t_idx - rel).astype(F32)), -jnp.inf)
        sc = scores[p] + jnp.concatenate([bias_old, bias_new], axis=1)
        mx = jnp.maximum(jnp.max(sc, axis=-1, keepdims=True), sink)
        p_un = jnp.exp(sc - mx)
        z = jnp.sum(p_un, axis=-1, keepdims=True) + jnp.exp(sink - mx)
        weights.append((p_un.astype(BF16), z))

    for p in range(n_pairs):
        pb, z = weights[p]
        o_pair = []
        for e in range(2):
            vt = jnp.concatenate([cvt_ref[2 * p + e].astype(BF16), vt_new_b], axis=1)
            o_pair.append(lax.dot_general(pb, vt, (((1,), (1,)), ((), ())),
                                          preferred_element_type=F32))
        o = jnp.where(odd, o_pair[1], o_pair[0]) * (1.0 / z)
        for m in range(N_PAIRS):
            kv_head = (2 * m) // Q_PER_KV
            o_even = o[(2 * m) * pair_rows:(2 * m + 1) * pair_rows]
            o_odd = o[(2 * m + 1) * pair_rows:(2 * m + 2) * pair_rows]
            if kv_head == 0:
                o_odd = pltpu.roll(o_odd, HALF, axis=1)
            else:
                o_even = pltpu.roll(o_even, HALF, axis=1)
            b_s[pl.ds(rows0[p], pair_rows), m * LANES:(m + 1) * LANES] = (
                jnp.where(lo8, o_even, o_odd))
        for e in range(2):
            wvt_ref[2 * p + e] = jnp.where(
                keep, pltpu.roll(cvt_ref[2 * p + e], W - L, axis=1),
                pltpu.roll(vt_new, (W - L) - (offs[p] + e * L), axis=1))

    @pl.when(g == n_steps - 1)
    def _merge_and_ffn():
        for copy in late_weight_copies():
            copy.wait()
        x1 = _merge(x_ref[...].reshape(n, d), a_s[...], b_s[...], w_o_ref, vec_ref)
        x1o_ref[...] = x1
        h2 = _rms(x1, vec_ref[ROW_G_PFF:ROW_G_PFF + 1, :d]).astype(BF16)
        f0 = jnp.zeros(x1.shape, F32)
        for c in range(EARLY_CHUNKS):
            up = _dot(h2, w_up_ref[:, c * FF_CHUNK:(c + 1) * FF_CHUNK])
            f0 = f0 + _dot(_relu_sq(up), w_down_ref[c * FF_CHUNK:(c + 1) * FF_CHUNK, :])
        f0o_ref[...] = f0


def _sample_call(x, layer, sinks, w4, b_sp, cache_kt, cache_vt, w_in, w_o, w_up, w_down,
                 norms):
    Bd, L, D = x.shape
    hd, W = cache_kt.shape[1:]
    n = Bd * L
    bb = SAMPLE_BB
    assert Bd % bb == 0 and bb % 2 == 0 and (LANES // L) % bb == 0
    smem = pl.BlockSpec(memory_space=pltpu.SMEM)
    blk = pl.BlockSpec((bb, hd, W), lambda g: (g, 0, 0))
    const = lambda shape: pl.BlockSpec(shape, lambda g: (0,) * len(shape))
    late_weights = (w_o, w_up, w_down)
    in_hbm = pl.BlockSpec(memory_space=pl.ANY)
    return pl.pallas_call(
        functools.partial(_sample_kernel, L, layer),
        grid=(Bd // bb,),
        in_specs=[smem, smem, smem, _resident(x.shape), blk, blk, _resident(w_in.shape)] +
                 [in_hbm for _ in late_weights] + [_layer_row(a, layer) for a in norms],
        out_specs=[const((n, D)), const((n, D)), const((Bd, L, CHUNK_WIDTH)), blk, blk],
        out_shape=[jax.ShapeDtypeStruct((n, D), F32),
                   jax.ShapeDtypeStruct((n, D), F32),
                   jax.ShapeDtypeStruct((Bd, L, CHUNK_WIDTH), F32),
                   jax.ShapeDtypeStruct(cache_kt.shape, F32),
                   jax.ShapeDtypeStruct(cache_vt.shape, F32)],
        scratch_shapes=[pltpu.VMEM((8, D), F32),
                        pltpu.VMEM((n, ATTN_WIDTH), F32),
                        pltpu.VMEM((2 * KV_WIDTH, n), F32),
                        pltpu.VMEM((n, CHUNK_WIDTH), F32),
                        pltpu.VMEM((n, ATTN_WIDTH), F32)] +
                       [pltpu.VMEM(w_o.shape, w_o.dtype),
                        pltpu.VMEM((D, EARLY_CHUNKS * FF_CHUNK), w_up.dtype),
                        pltpu.VMEM((EARLY_CHUNKS * FF_CHUNK, D), w_down.dtype)] +
                       [pltpu.SemaphoreType.DMA((len(late_weights),))],
        compiler_params=pltpu.CompilerParams(
            dimension_semantics=("arbitrary",), vmem_limit_bytes=VMEM_LIMIT),
        name="sample_layer",
    )(sinks, w4, b_sp, x, cache_kt, cache_vt, w_in, *late_weights, *norms)


def _heads_last(t):
    b, _, w = t.shape
    return t.reshape(b, N_KV_HEADS, HEAD_DIM, w).transpose(0, 3, 1, 2)


def kernel(x_prompt, x_sample, cache_win_k, cache_win_v, w_in, g_pre_mix, ln_v_g, ln_v_b,
           w_spatial, b_spatial, attn_sinks, g_out_chunk, g_out_attn, w_o, g_post_mix,
           g_pre_ffn, w_up, w_down, g_post_ffn):
    depth = w_in.shape[0]
    Bd, L, D = x_sample.shape
    W = cache_win_k.shape[2]
    norms = (g_pre_mix, g_post_mix, g_pre_ffn, g_post_ffn, ln_v_g, ln_v_b, g_out_chunk,
             g_out_attn)
    yp, ys = x_prompt, x_sample
    wk_p, wv_p, cv_p, wk_s, wv_s, cv_s = [], [], [], [], [], []
    for l in range(depth):
        w_in_b = w_in[l].astype(BF16)
        w_o_b = w_o[l].astype(BF16)
        w_up_b = w_up[l].astype(BF16)
        w_down_b = w_down[l].astype(BF16)

        w4 = w_spatial[l, :, :L, :L]
        cache_kt = cache_win_k[l].transpose(0, 2, 3, 1).reshape(Bd, KV_WIDTH, W)
        cache_vt = cache_win_v[l].transpose(0, 2, 3, 1).reshape(Bd, KV_WIDTH, W)
        x1_s, f0_s, cv, wk, wv = _sample_call(ys, l, attn_sinks, w4, b_spatial, cache_kt,
                                              cache_vt, w_in_b, w_o_b, w_up_b, w_down_b, norms)
        wk_s.append(_heads_last(wk))
        wv_s.append(_heads_last(wv))
        cv_s.append(cv)

        yp, ys, wk, wv, cv = _prompt_call(yp, x1_s, f0_s, ys.shape, l, attn_sinks, w_in_b, w_o_b,
                                          w_up_b, w_down_b, norms, w_spatial, b_spatial)
        wk_p.append(_heads_last(wk))
        wv_p.append(_heads_last(wv))
        cv_p.append(cv)
    return (yp, ys, jnp.stack(wk_p), jnp.stack(wv_p), jnp.stack(cv_p),
            jnp.stack(wk_s), jnp.stack(wv_s), jnp.stack(cv_s))
```

```python
import functools

import jax
import jax.numpy as jnp
from jax import lax
from jax.experimental import pallas as pl
from jax.experimental.pallas import tpu as pltpu

F32 = jnp.float32
BF16 = jnp.bfloat16

HEAD_DIM = 64
N_CHUNK_HEADS = 8
N_ATTN_HEADS = 8
N_KV_HEADS = 2
Q_PER_KV = N_ATTN_HEADS // N_KV_HEADS
CHUNK_WIDTH = N_CHUNK_HEADS * HEAD_DIM
ATTN_WIDTH = N_ATTN_HEADS * HEAD_DIM
KV_WIDTH = N_KV_HEADS * HEAD_DIM
CHUNK = 128
WINDOW = 128
EPS = 1e-6
ATTN_SCALE = HEAD_DIM ** -0.5

LANES = 128
HALF = LANES // 2
N_PAIRS = ATTN_WIDTH // LANES
VMEM_LIMIT = 56 * 1024 * 1024

PROMPT_TILE = 512
FF_CHUNK = 512
EARLY_CHUNKS = 1
SAMPLE_BB = 16
CACHE_BUFFERS = 3

ROW_G_PRE, ROW_G_PM, ROW_G_PFF, ROW_G_PF, ROW_LN, ROW_GOUT = range(6)


def _dot(a, b):
    return jnp.dot(a, b, preferred_element_type=F32)


def _rms(x, g):
    return x * lax.rsqrt(jnp.mean(x * x, axis=-1, keepdims=True) + EPS) * g


def _layer_norm(x, g, b):
    xc = x - jnp.mean(x, axis=-1, keepdims=True)
    return xc * lax.rsqrt(jnp.mean(xc * xc, axis=-1, keepdims=True) + EPS) * g + b


def _alibi_slope(head):
    return 2.0 ** (-8.0 * (head + 1) / N_ATTN_HEADS)


def _merge(x, a_out, b_out, w_o_ref, vec_ref):
    d = x.shape[-1]
    cw = CHUNK_WIDTH
    ra = _rms(a_out, vec_ref[ROW_GOUT:ROW_GOUT + 1, 0:cw]).astype(BF16)
    rb = _rms(b_out, vec_ref[ROW_GOUT:ROW_GOUT + 1, cw:cw + ATTN_WIDTH]).astype(BF16)
    o = _dot(ra, w_o_ref[0:cw, :]) + _dot(rb, w_o_ref[cw:cw + ATTN_WIDTH, :])
    return x + _rms(o, vec_ref[ROW_G_PM:ROW_G_PM + 1, :d])


def _relu_sq(up):
    r = jnp.maximum(up, 0.0)
    return (r * r).astype(BF16)


def _ffn(x1, w_up_ref, w_down_ref, vec_ref):
    d = x1.shape[-1]
    h2 = _rms(x1, vec_ref[ROW_G_PFF:ROW_G_PFF + 1, :d]).astype(BF16)
    d_ff = w_up_ref.shape[1]
    f = jnp.zeros(x1.shape, F32)
    for c in range(d_ff // FF_CHUNK):
        up = _dot(h2, w_up_ref[:, c * FF_CHUNK:(c + 1) * FF_CHUNK])
        f = f + _dot(_relu_sq(up), w_down_ref[c * FF_CHUNK:(c + 1) * FF_CHUNK, :])
    return x1 + _rms(f, vec_ref[ROW_G_PF:ROW_G_PF + 1, :d])


def _fill_vector_table(vec_s, g_pre, g_pm, g_pff, g_pf, ln_g, ln_b, g_oc, g_oa):
    d = g_pre.shape[-1]
    cw = CHUNK_WIDTH
    vec_s[ROW_G_PRE:ROW_G_PRE + 1, 0:d] = g_pre[...]
    vec_s[ROW_G_PM:ROW_G_PM + 1, 0:d] = g_pm[...]
    vec_s[ROW_G_PFF:ROW_G_PFF + 1, 0:d] = g_pff[...]
    vec_s[ROW_G_PF:ROW_G_PF + 1, 0:d] = g_pf[...]
    vec_s[ROW_LN:ROW_LN + 1, 0:cw] = ln_g[...]
    vec_s[ROW_LN:ROW_LN + 1, cw:2 * cw] = ln_b[...]
    vec_s[ROW_GOUT:ROW_GOUT + 1, 0:cw] = g_oc[...]
    vec_s[ROW_GOUT:ROW_GOUT + 1, cw:cw + ATTN_WIDTH] = g_oa[...]


def _prompt_kernel(n_tiles, nj, layer, sink_ref, x_ref, x1in_ref, f0in_ref, w_in_ref, w_o_ref,
                   w_up_ref, w_down_ref,
                   g_pre, g_pm, g_pff, g_pf, ln_g, ln_b, g_oc, g_oa, wsp_ref, bsp_ref,
                   y_ref, ys_ref, wk_ref, wv_ref, cv_ref,
                   vec_ref, bias_s, wcat_s, tab_s, kt_s, vd_s, a_s, bo_s, m_s, x1_s, x1n_s,
                   h_s, q_s, h2_s, act_s, f_s, carry_sem):
    step = pl.program_id(0)
    j = jnp.minimum(step, n_tiles - 1) % nj
    T, d = x_ref.shape[1], x_ref.shape[2]
    nb = T // WINDOW
    cw = CHUNK_WIDTH
    n_chunks = w_up_ref.shape[1] // FF_CHUNK

    lane = lax.broadcasted_iota(jnp.int32, (WINDOW, LANES), 1)
    lo = lane < HALF

    @pl.when(step == 0)
    def _init_tables():
        _fill_vector_table(vec_ref, g_pre, g_pm, g_pff, g_pf, ln_g, ln_b, g_oc, g_oa)
        carry_copies = [pltpu.make_async_copy(x1in_ref, x1_s, carry_sem.at[0]),
                        pltpu.make_async_copy(f0in_ref, f_s, carry_sem.at[1])]
        for copy in carry_copies:
            copy.start(priority=1)
        t_idx = lax.broadcasted_iota(jnp.int32, (CHUNK, CHUNK), 0)
        s_idx = lax.broadcasted_iota(jnp.int32, (CHUNK, CHUNK), 1)
        for g in range(N_CHUNK_HEADS):
            wm = jnp.where(s_idx <= t_idx, wsp_ref[0, g], 0.0).astype(BF16)
            wcat_s[g // 2, :, (g % 2) * CHUNK:(g % 2 + 1) * CHUNK] = wm
        b_rows = jnp.concatenate(
            [bsp_ref[0], jnp.zeros((CHUNK - N_CHUNK_HEADS, CHUNK), F32)], axis=0)
        b_cols = b_rows.T
        for p in range(cw // LANES):
            bias_s[:, p * LANES:(p + 1) * LANES] = jnp.where(
                lo, jnp.broadcast_to(b_cols[:, 2 * p:2 * p + 1], (CHUNK, LANES)),
                jnp.broadcast_to(b_cols[:, 2 * p + 1:2 * p + 2], (CHUNK, LANES)))
        a_idx = lax.broadcasted_iota(jnp.int32, (WINDOW, 2 * WINDOW), 0)
        c_idx = lax.broadcasted_iota(jnp.int32, (WINDOW, 2 * WINDOW), 1)
        dist = WINDOW + a_idx - c_idx
        valid = (dist >= 0) & (dist <= WINDOW)
        for hd in range(N_ATTN_HEADS):
            bias = jnp.where(valid, -(_alibi_slope(hd) * dist.astype(F32)), -jnp.inf)
            r = hd % Q_PER_KV
            tab_s[hd // Q_PER_KV, r * WINDOW:(r + 1) * WINDOW, :] = bias
        for copy in carry_copies:
            copy.wait()
        h2_s[...] = _rms(x1_s[...], vec_ref[ROW_G_PFF:ROW_G_PFF + 1, :d]).astype(BF16)

    @pl.when(j == 0)
    def _clear_carry():
        kt_s[:, 0:WINDOW] = jnp.zeros((2 * LANES, WINDOW), BF16)
        vd_s[:, 0:WINDOW, :] = jnp.zeros((N_KV_HEADS, WINDOW, LANES), BF16)

    def ffn_up(c):
        act_s[...] = _relu_sq(_dot(h2_s[...], w_up_ref[:, c * FF_CHUNK:(c + 1) * FF_CHUNK]))

    def ffn_down(c):
        part = _dot(act_s[...], w_down_ref[c * FF_CHUNK:(c + 1) * FF_CHUNK, :])
        if c == 0:
            f_s[...] = part
        else:
            f_s[...] += part

    def finish_previous():
        y_ref[0] = x1_s[...] + _rms(f_s[...], vec_ref[ROW_G_PF:ROW_G_PF + 1, :d])

    def tile_step(with_late):
        late = iter([functools.partial(fn, c) for c in range(EARLY_CHUNKS, n_chunks)
                     for fn in (ffn_up, ffn_down)])

        def late_stage(count=1):
            if with_late:
                for _ in range(count):
                    next(late)()

        late_stage()
        h_s[...] = _rms(x_ref[0], vec_ref[ROW_G_PRE:ROW_G_PRE + 1, :d]).astype(BF16)
        u_pre = _dot(h_s[...], w_in_ref[:, 0:cw])
        v_pre = _dot(h_s[...], w_in_ref[:, cw:2 * cw])
        late_stage()
        u = jax.nn.gelu(u_pre)
        late_stage()
        v = _layer_norm(jax.nn.gelu(v_pre), vec_ref[ROW_LN:ROW_LN + 1, 0:cw],
                        vec_ref[ROW_LN:ROW_LN + 1, cw:2 * cw])
        kv0 = 2 * cw + ATTN_WIDTH
        kv = _dot(h_s[...], w_in_ref[:, kv0:kv0 + 2 * KV_WIDTH])
        k = kv[:, 0:KV_WIDTH]
        val = kv[:, KV_WIDTH:2 * KV_WIDTH]
        q0 = 2 * cw
        q_s[...] = _dot(h_s[...], w_in_ref[:, q0:q0 + ATTN_WIDTH]) * ATTN_SCALE
        late_stage()

        kt_f = k.T
        wk_ref[0] = kt_f[:, T - WINDOW:]
        wv_ref[0] = val[T - WINDOW:, :].T
        cv_ref[0] = v[T - CHUNK:, :]

        assert nb % 2 == 0
        for c in range(0, nb, 2):
            for p in range(cw // LANES):
                cols = slice(p * LANES, (p + 1) * LANES)
                halves = []
                for cc in (c, c + 1):
                    vcp = v[cc * CHUNK:(cc + 1) * CHUNK, cols]
                    halves.append(jnp.concatenate([jnp.where(lo, vcp, 0.0), jnp.where(lo, 0.0, vcp)],
                                                  axis=0))
                rhs = jnp.concatenate(halves, axis=1).astype(BF16)
                s_pair = _dot(wcat_s[p], rhs)
                for side, cc in enumerate((c, c + 1)):
                    rows = slice(cc * CHUNK, (cc + 1) * CHUNK)
                    s_cp = s_pair[:, side * LANES:(side + 1) * LANES] + bias_s[:, cols]
                    a_s[rows, cols] = u[rows, cols] * s_cp
        m_s[:, 0:cw] = _rms(a_s[...], vec_ref[ROW_GOUT:ROW_GOUT + 1, 0:cw]).astype(BF16)

        kt = kt_f.astype(BF16)
        for hk in range(N_KV_HEADS):
            kth = kt[hk * HEAD_DIM:(hk + 1) * HEAD_DIM, :]
            kt_s[hk * LANES:hk * LANES + HEAD_DIM, WINDOW:] = kth
            kt_s[hk * LANES + HEAD_DIM:(hk + 1) * LANES, WINDOW:] = kth
        lane_t = lax.broadcasted_iota(jnp.int32, (T, LANES), 1)
        lo_t = lane_t < HALF
        val_sw = pltpu.roll(val, HALF, axis=1)
        vd_s[0, WINDOW:, :] = jnp.where(lo_t, val, val_sw).astype(BF16)
        vd_s[1, WINDOW:, :] = jnp.where(lo_t, val_sw, val).astype(BF16)

        first_block_bias = jnp.where(j == 0, -jnp.inf, 0.0).astype(F32)
        head_of_row = lax.broadcasted_iota(jnp.int32, (Q_PER_KV * WINDOW, 1), 0) // WINDOW
        n_units = nb * N_KV_HEADS
        n_mid = 2 * (n_chunks - EARLY_CHUNKS) - 9
        late_after_scores = tuple(1 + (k * n_units) // n_mid for k in range(n_mid))
        for n, (i, hk) in enumerate((i, hk) for i in range(nb) for hk in range(N_KV_HEADS)):
            rows = slice(i * WINDOW, (i + 1) * WINDOW)
            keys = slice(i * WINDOW, (i + 2) * WINDOW)
            tiles = []
            for m in range(hk * 2, hk * 2 + 2):
                qb = q_s[rows, m * LANES:(m + 1) * LANES]
                tiles += [jnp.where(lo, qb, 0.0), jnp.where(lo, 0.0, qb)]
            qq = jnp.concatenate(tiles, axis=0).astype(BF16)
            s = _dot(qq, kt_s[hk * LANES:(hk + 1) * LANES, keys]) + tab_s[hk]
            if i == 0:
                s = jnp.concatenate([s[:, 0:WINDOW] + first_block_bias, s[:, WINDOW:]], axis=1)
            if n in late_after_scores:
                late_stage()
            sink = jnp.zeros((Q_PER_KV * WINDOW, 1), F32)
            for r in range(Q_PER_KV):
                sink = jnp.where(head_of_row == r, sink_ref[layer, hk, r], sink)
            mx = jnp.maximum(jnp.max(s, axis=-1, keepdims=True), sink)
            p_un = jnp.exp(s - mx)
            z = jnp.sum(p_un, axis=-1, keepdims=True) + jnp.exp(sink - mx)
            o = _dot(p_un.astype(BF16), vd_s[hk, keys, :]) * (1.0 / z)
            for pair in range(Q_PER_KV // 2):
                m = hk * 2 + pair
                o_even = o[(2 * pair) * WINDOW:(2 * pair + 1) * WINDOW]
                o_odd = o[(2 * pair + 1) * WINDOW:(2 * pair + 2) * WINDOW]
                bo_s[rows, m * LANES:(m + 1) * LANES] = jnp.where(lo, o_even, o_odd)

        kt_s[:, 0:WINDOW] = kt_s[:, T:T + WINDOW]
        vd_s[:, 0:WINDOW, :] = vd_s[:, T:T + WINDOW, :]

        late_stage()
        m_s[:, cw:cw + ATTN_WIDTH] = _rms(
            bo_s[...], vec_ref[ROW_GOUT:ROW_GOUT + 1, cw:cw + ATTN_WIDTH]).astype(BF16)
        o_proj = _dot(m_s[...], w_o_ref[...])
        late_stage(3)
        x1 = x_ref[0] + _rms(o_proj, vec_ref[ROW_G_PM:ROW_G_PM + 1, :d])
        x1n_s[...] = x1
        h2_s[...] = _rms(x1, vec_ref[ROW_G_PFF:ROW_G_PFF + 1, :d]).astype(BF16)
        late_stage()

        ffn_up(0)
        if with_late:
            finish_previous()
        x1_s[...] = x1n_s[...]
        ffn_down(0)
        for c in range(1, EARLY_CHUNKS):
            ffn_up(c)
            ffn_down(c)

    @pl.when(step < n_tiles)
    def _tile_step():
        tile_step(True)

    @pl.when(step == 0)
    def _emit_sample():
        ys_ref[...] = y_ref[0].reshape(ys_ref.shape)

    @pl.when(step == n_tiles)
    def _drain_step():
        for c in range(EARLY_CHUNKS, n_chunks):
            ffn_up(c)
            ffn_down(c)
        finish_previous()


def _resident(shape):
    return pl.BlockSpec(shape, lambda *_: (0,) * len(shape), pipeline_mode=pl.Buffered(1))


def _layer_row(arr, layer):
    block = (1,) + arr.shape[1:]
    return pl.BlockSpec(block, lambda *_: (layer,) + (0,) * (arr.ndim - 1),
                        pipeline_mode=pl.Buffered(1))


def _prompt_call(x, x1_sample, f0_sample, sample_shape, layer, sinks, w_in, w_o, w_up, w_down,
                 norms, w_sp, b_sp):
    B, S, D = x.shape
    T = PROMPT_TILE
    assert x1_sample.shape == (T, D) and f0_sample.shape == (T, D)
    assert S % T == 0 and T % WINDOW == 0 and WINDOW == CHUNK
    assert 1 <= EARLY_CHUNKS and 2 * (w_up.shape[1] // FF_CHUNK - EARLY_CHUNKS) >= 9
    nj = S // T
    n_tiles = B * nj
    cur = lambda s: jnp.minimum(s, n_tiles - 1)
    prev = lambda s: jnp.maximum(s - 1, 0)
    tile_in = pl.BlockSpec((1, T, D), lambda s: (cur(s) // nj, cur(s) % nj, 0))
    tile_out = pl.BlockSpec((1, T, D), lambda s: (prev(s) // nj, prev(s) % nj, 0))
    last = lambda r, c: pl.BlockSpec((1, r, c), lambda s: (cur(s) // nj, 0, 0))
    return pl.pallas_call(
        functools.partial(_prompt_kernel, n_tiles, nj, layer),
        grid=(n_tiles + 1,),
        in_specs=[pl.BlockSpec(memory_space=pltpu.SMEM), tile_in,
                  pl.BlockSpec(memory_space=pl.ANY), pl.BlockSpec(memory_space=pl.ANY),
                  _resident(w_in.shape), _resident(w_o.shape), _resident(w_up.shape),
                  _resident(w_down.shape)] + [_layer_row(a, layer) for a in norms] +
                 [_layer_row(w_sp, layer), _layer_row(b_sp, layer)],
        out_specs=[tile_out, pl.BlockSpec(sample_shape, lambda s: (0,) * len(sample_shape)),
                   last(KV_WIDTH, WINDOW), last(KV_WIDTH, WINDOW), last(CHUNK, CHUNK_WIDTH)],
        out_shape=[jax.ShapeDtypeStruct((B, S, D), F32),
                   jax.ShapeDtypeStruct(sample_shape, F32),
                   jax.ShapeDtypeStruct((B, KV_WIDTH, WINDOW), F32),
                   jax.ShapeDtypeStruct((B, KV_WIDTH, WINDOW), F32),
                   jax.ShapeDtypeStruct((B, CHUNK, CHUNK_WIDTH), F32)],
        scratch_shapes=[pltpu.VMEM((8, D), F32),
                        pltpu.VMEM((CHUNK, CHUNK_WIDTH), F32),
                        pltpu.VMEM((N_CHUNK_HEADS // 2, CHUNK, 2 * CHUNK), BF16),
                        pltpu.VMEM((N_KV_HEADS, Q_PER_KV * WINDOW, 2 * WINDOW), F32),
                        pltpu.VMEM((2 * LANES, T + WINDOW), BF16),
                        pltpu.VMEM((N_KV_HEADS, T + WINDOW, LANES), BF16),
                        pltpu.VMEM((T, CHUNK_WIDTH), F32),
                        pltpu.VMEM((T, ATTN_WIDTH), F32),
                        pltpu.VMEM((T, CHUNK_WIDTH + ATTN_WIDTH), BF16),
                        pltpu.VMEM((T, D), F32),
                        pltpu.VMEM((T, D), F32),
                        pltpu.VMEM((T, D), BF16),
                        pltpu.VMEM((T, ATTN_WIDTH), F32),
                        pltpu.VMEM((T, D), BF16),
                        pltpu.VMEM((T, FF_CHUNK), BF16),
                        pltpu.VMEM((T, D), F32),
                        pltpu.SemaphoreType.DMA((2,))],
        compiler_params=pltpu.CompilerParams(
            dimension_semantics=("arbitrary",), vmem_limit_bytes=VMEM_LIMIT),
        name="prompt_layer",
    )(sinks, x, x1_sample, f0_sample, w_in, w_o, w_up, w_down, *norms, w_sp, b_sp)


def _sample_kernel(L, layer, n_steps_static, sink_ref, w4_ref, b4_ref, x_ref, ckt_hbm, cvt_hbm, w_in_ref,
                   w_o_hbm, w_up_hbm, w_down_hbm,
                   g_pre, g_pm, g_pff, g_pf, ln_g, ln_b, g_oc, g_oa,
                   x1o_ref, f0o_ref, vn_ref, wkt_ref, wvt_ref,
                   vec_ref, q_s, kvt_s, a_s, b_s, w_o_ref, w_up_ref, w_down_ref, w_sem,
                   ck_buf, cv_buf, cache_sem):
    g = pl.program_id(0)
    n_steps = pl.num_programs(0)
    n_buf, bb, hd, W = ck_buf.shape

    def cache_copies(block, slot):
        rows = pl.ds(block * bb, bb)
        return [pltpu.make_async_copy(ckt_hbm.at[rows], ck_buf.at[slot], cache_sem.at[0, slot]),
                pltpu.make_async_copy(cvt_hbm.at[rows], cv_buf.at[slot], cache_sem.at[1, slot])]

    @pl.when(g == 0)
    def _prime_ring():
        for ahead in range(min(n_buf - 1, n_steps_static)):
            for copy in cache_copies(ahead, ahead):
                copy.start()

    @pl.when(g + (n_buf - 1) < n_steps_static)
    def _prefetch():
        for copy in cache_copies(g + (n_buf - 1), (g + (n_buf - 1)) % n_buf):
            copy.start()

    slot = g % n_buf
    for copy in cache_copies(g, slot):
        copy.wait()
    n, d = q_s.shape[0], x_ref.shape[-1]
    n_heads = N_ATTN_HEADS
    pair_rows = 2 * L
    assert pair_rows == 8 and hd == LANES and W == LANES
    batch_per_block = LANES // L

    def late_weight_copies():
        early = w_up_ref.shape[1]
        pairs = ((w_o_hbm, w_o_ref), (w_up_hbm.at[:, 0:early], w_up_ref),
                 (w_down_hbm.at[0:early, :], w_down_ref))
        return [pltpu.make_async_copy(src, dst, w_sem.at[i]) for i, (src, dst) in enumerate(pairs)]

    @pl.when(g == 0)
    def _project_and_gate():
        cw = CHUNK_WIDTH
        for copy in late_weight_copies():
            copy.start(priority=1)
        _fill_vector_table(vec_ref, g_pre, g_pm, g_pff, g_pf, ln_g, ln_b, g_oc, g_oa)
        h = _rms(x_ref[...].reshape(n, d), vec_ref[ROW_G_PRE:ROW_G_PRE + 1, :d]).astype(BF16)
        u = jax.nn.gelu(_dot(h, w_in_ref[:, 0:cw]))
        v = _layer_norm(jax.nn.gelu(_dot(h, w_in_ref[:, cw:2 * cw])),
                        vec_ref[ROW_LN:ROW_LN + 1, 0:cw], vec_ref[ROW_LN:ROW_LN + 1, cw:2 * cw])
        vn_ref[...] = v.reshape(vn_ref.shape)
        q_s[...] = _dot(h, w_in_ref[:, 2 * cw:2 * cw + ATTN_WIDTH]) * ATTN_SCALE
        kv0 = 2 * cw + ATTN_WIDTH
        kvt_s[...] = _dot(h, w_in_ref[:, kv0:kv0 + 2 * KV_WIDTH]).T
        t_of_row = lax.broadcasted_iota(jnp.int32, (n, 1), 0) % L
        group_of_lane = lax.broadcasted_iota(jnp.int32, (1, cw), 1) // HEAD_DIM

        def per_group(value_of_group):
            row = jnp.zeros((1, cw), F32)
            for grp in range(N_CHUNK_HEADS):
                row = jnp.where(group_of_lane == grp, value_of_group(grp), row)
            return row

        s = jnp.zeros(v.shape, F32)
        for t in range(L):
            s = jnp.where(t_of_row == t, per_group(lambda grp: b4_ref[layer, grp, t]), s)
        for delta in range(L):
            coef = jnp.zeros(v.shape, F32)
            for t in range(delta, L):
                w_row = per_group(lambda grp: w4_ref[grp, t, t - delta])
                coef = jnp.where(t_of_row == t, w_row, coef)
            vs = v if delta == 0 else pltpu.roll(v, delta, axis=0)
            s = s + coef * vs
        a_s[...] = u * s

    rows_all = n_heads * pair_rows
    r_idx = lax.broadcasted_iota(jnp.int32, (rows_all, LANES), 0)
    l_idx = lax.broadcasted_iota(jnp.int32, (rows_all, LANES), 1)
    t_idx = r_idx % L
    odd = (r_idx % pair_rows) >= L
    head_of_row = lax.broadcasted_iota(jnp.int32, (rows_all, 1), 0) // pair_rows
    sink = jnp.zeros((rows_all, 1), F32)
    slope = jnp.zeros((rows_all, 1), F32)
    for hh in range(n_heads):
        sink = jnp.where(head_of_row == hh, sink_ref[layer, hh // Q_PER_KV, hh % Q_PER_KV], sink)
        slope = jnp.where(head_of_row == hh, _alibi_slope(hh), slope)
    dist_old = W + t_idx - l_idx
    bias_old = jnp.where(dist_old <= WINDOW, -(slope * dist_old.astype(F32)), -jnp.inf)
    lane8 = lax.broadcasted_iota(jnp.int32, (pair_rows, LANES), 1)
    lo8 = lane8 < HALF
    lane_w = lax.broadcasted_iota(jnp.int32, (hd, W), 1)

    first_batch = g * bb
    new_block = pl.multiple_of((first_batch // batch_per_block) * LANES, LANES)
    kt_new = kvt_s[0:KV_WIDTH, pl.ds(new_block, LANES)]
    vt_new = kvt_s[KV_WIDTH:2 * KV_WIDTH, pl.ds(new_block, LANES)]
    kt_new_b = kt_new.astype(BF16)
    vt_new_b = vt_new.astype(BF16)
    n_pairs = bb // 2
    offs = [((first_batch + 2 * p) % batch_per_block) * L for p in range(n_pairs)]
    rows0 = [pl.multiple_of((first_batch + 2 * p) * L, pair_rows) for p in range(n_pairs)]
    keep = lane_w < W - L

    scores = []
    for p in range(n_pairs):
        tiles = []
        for m in range(N_PAIRS):
            qb = q_s[pl.ds(rows0[p], pair_rows), m * LANES:(m + 1) * LANES]
            qb_sw = pltpu.roll(qb, HALF, axis=1)
            if (2 * m) // Q_PER_KV == 0:
                tiles += [jnp.where(lo8, qb, 0.0), jnp.where(lo8, qb_sw, 0.0)]
            else:
                tiles += [jnp.where(lo8, 0.0, qb_sw), jnp.where(lo8, 0.0, qb)]
        lhs = jnp.concatenate(tiles, axis=0).astype(BF16)
        s_pair = []
        for e in range(2):
            rhs = jnp.concatenate([ck_buf[slot, 2 * p + e].astype(BF16), kt_new_b], axis=1)
            s_pair.append(_dot(lhs, rhs))
        scores.append(jnp.where(jnp.concatenate([odd, odd], axis=1), s_pair[1], s_pair[0]))
        for e in range(2):
            wkt_ref[2 * p + e] = jnp.where(
                keep, pltpu.roll(ck_buf[slot, 2 * p + e], W - L, axis=1),
                pltpu.roll(kt_new, (W - L) - (offs[p] + e * L), axis=1))

    weights = []
    for p in range(n_pairs):
        rel = l_idx - offs[p] - jnp.where(odd, L, 0)
        bias_new = jnp.where((rel >= 0) & (rel <= t_idx),
                             -(slope * (t_idx - rel).astype(F32)), -jnp.inf)
        sc = scores[p] + jnp.concatenate([bias_old, bias_new], axis=1)
        mx = jnp.maximum(jnp.max(sc, axis=-1, keepdims=True), sink)
        p_un = jnp.exp(sc - mx)
        z = jnp.sum(p_un, axis=-1, keepdims=True) + jnp.exp(sink - mx)
        weights.append((p_un.astype(BF16), z))

    for p in range(n_pairs):
        pb, z = weights[p]
        o_pair = []
        for e in range(2):
            vt = jnp.concatenate([cv_buf[slot, 2 * p + e].astype(BF16), vt_new_b], axis=1)
            o_pair.append(lax.dot_general(pb, vt, (((1,), (1,)), ((), ())),
                                          preferred_element_type=F32))
        o = jnp.where(odd, o_pair[1], o_pair[0]) * (1.0 / z)
        for m in range(N_PAIRS):
            kv_head = (2 * m) // Q_PER_KV
            o_even = o[(2 * m) * pair_rows:(2 * m + 1) * pair_rows]
            o_odd = o[(2 * m + 1) * pair_rows:(2 * m + 2) * pair_rows]
            if kv_head == 0:
                o_odd = pltpu.roll(o_odd, HALF, axis=1)
            else:
                o_even = pltpu.roll(o_even, HALF, axis=1)
            b_s[pl.ds(rows0[p], pair_rows), m * LANES:(m + 1) * LANES] = (
                jnp.where(lo8, o_even, o_odd))
        for e in range(2):
            wvt_ref[2 * p + e] = jnp.where(
                keep, pltpu.roll(cv_buf[slot, 2 * p + e], W - L, axis=1),
                pltpu.roll(vt_new, (W - L) - (offs[p] + e * L), axis=1))

    @pl.when(g == n_steps - 1)
    def _merge_and_ffn():
        for copy in late_weight_copies():
            copy.wait()
        x1 = _merge(x_ref[...].reshape(n, d), a_s[...], b_s[...], w_o_ref, vec_ref)
        x1o_ref[...] = x1
        h2 = _rms(x1, vec_ref[ROW_G_PFF:ROW_G_PFF + 1, :d]).astype(BF16)
        f0 = jnp.zeros(x1.shape, F32)
        for c in range(EARLY_CHUNKS):
            up = _dot(h2, w_up_ref[:, c * FF_CHUNK:(c + 1) * FF_CHUNK])
            f0 = f0 + _dot(_relu_sq(up), w_down_ref[c * FF_CHUNK:(c + 1) * FF_CHUNK, :])
        f0o_ref[...] = f0


def _sample_call(x, layer, sinks, w4, b_sp, cache_kt, cache_vt, w_in, w_o, w_up, w_down,
                 norms):
    Bd, L, D = x.shape
    hd, W = cache_kt.shape[1:]
    n = Bd * L
    bb = SAMPLE_BB
    assert Bd % bb == 0 and bb % 2 == 0 and (LANES // L) % bb == 0
    smem = pl.BlockSpec(memory_space=pltpu.SMEM)
    blk = pl.BlockSpec((bb, hd, W), lambda g: (g, 0, 0))
    blk_in = pl.BlockSpec(memory_space=pl.ANY)
    const = lambda shape: pl.BlockSpec(shape, lambda g: (0,) * len(shape))
    late_weights = (w_o, w_up, w_down)
    in_hbm = pl.BlockSpec(memory_space=pl.ANY)
    return pl.pallas_call(
        functools.partial(_sample_kernel, L, layer, Bd // bb),
        grid=(Bd // bb,),
        in_specs=[smem, smem, smem, _resident(x.shape), blk_in, blk_in, _resident(w_in.shape)] +
                 [in_hbm for _ in late_weights] + [_layer_row(a, layer) for a in norms],
        out_specs=[const((n, D)), const((n, D)), const((Bd, L, CHUNK_WIDTH)), blk, blk],
        out_shape=[jax.ShapeDtypeStruct((n, D), F32),
                   jax.ShapeDtypeStruct((n, D), F32),
                   jax.ShapeDtypeStruct((Bd, L, CHUNK_WIDTH), F32),
                   jax.ShapeDtypeStruct(cache_kt.shape, F32),
                   jax.ShapeDtypeStruct(cache_vt.shape, F32)],
        scratch_shapes=[pltpu.VMEM((8, D), F32),
                        pltpu.VMEM((n, ATTN_WIDTH), F32),
                        pltpu.VMEM((2 * KV_WIDTH, n), F32),
                        pltpu.VMEM((n, CHUNK_WIDTH), F32),
                        pltpu.VMEM((n, ATTN_WIDTH), F32)] +
                       [pltpu.VMEM(w_o.shape, w_o.dtype),
                        pltpu.VMEM((D, EARLY_CHUNKS * FF_CHUNK), w_up.dtype),
                        pltpu.VMEM((EARLY_CHUNKS * FF_CHUNK, D), w_down.dtype)] +
                       [pltpu.SemaphoreType.DMA((len(late_weights),)),
                        pltpu.VMEM((CACHE_BUFFERS, bb, hd, W), F32),
                        pltpu.VMEM((CACHE_BUFFERS, bb, hd, W), F32),
                        pltpu.SemaphoreType.DMA((2, CACHE_BUFFERS))],
        compiler_params=pltpu.CompilerParams(
            dimension_semantics=("arbitrary",), vmem_limit_bytes=VMEM_LIMIT),
        name="sample_layer",
    )(sinks, w4, b_sp, x, cache_kt, cache_vt, w_in, *late_weights, *norms)


def _heads_last(t):
    b, _, w = t.shape
    return t.reshape(b, N_KV_HEADS, HEAD_DIM, w).transpose(0, 3, 1, 2)


def kernel(x_prompt, x_sample, cache_win_k, cache_win_v, w_in, g_pre_mix, ln_v_g, ln_v_b,
           w_spatial, b_spatial, attn_sinks, g_out_chunk, g_out_attn, w_o, g_post_mix,
           g_pre_ffn, w_up, w_down, g_post_ffn):
    depth = w_in.shape[0]
    Bd, L, D = x_sample.shape
    W = cache_win_k.shape[2]
    norms = (g_pre_mix, g_post_mix, g_pre_ffn, g_post_ffn, ln_v_g, ln_v_b, g_out_chunk,
             g_out_attn)
    yp, ys = x_prompt, x_sample
    wk_p, wv_p, cv_p, wk_s, wv_s, cv_s = [], [], [], [], [], []
    for l in range(depth):
        w_in_b = w_in[l].astype(BF16)
        w_o_b = w_o[l].astype(BF16)
        w_up_b = w_up[l].astype(BF16)
        w_down_b = w_down[l].astype(BF16)

        w4 = w_spatial[l, :, :L, :L]
        cache_kt = cache_win_k[l].transpose(0, 2, 3, 1).reshape(Bd, KV_WIDTH, W)
        cache_vt = cache_win_v[l].transpose(0, 2, 3, 1).reshape(Bd, KV_WIDTH, W)
        x1_s, f0_s, cv, wk, wv = _sample_call(ys, l, attn_sinks, w4, b_spatial, cache_kt,
                                              cache_vt, w_in_b, w_o_b, w_up_b, w_down_b, norms)
        wk_s.append(_heads_last(wk))
        wv_s.append(_heads_last(wv))
        cv_s.append(cv)

        yp, ys, wk, wv, cv = _prompt_call(yp, x1_s, f0_s, ys.shape, l, attn_sinks, w_in_b, w_o_b,
                                          w_up_b, w_down_b, norms, w_spatial, b_spatial)
        wk_p.append(_heads_last(wk))
        wv_p.append(_heads_last(wv))
        cv_p.append(cv)
    return (yp, ys, jnp.stack(wk_p), jnp.stack(wv_p), jnp.stack(cv_p),
            jnp.stack(wk_s), jnp.stack(wv_s), jnp.stack(cv_s))
```

```python
import functools

import jax
import jax.numpy as jnp
from jax import lax
from jax.experimental import pallas as pl
from jax.experimental.pallas import tpu as pltpu

F32 = jnp.float32
BF16 = jnp.bfloat16

HEAD_DIM = 64
N_CHUNK_HEADS = 8
N_ATTN_HEADS = 8
N_KV_HEADS = 2
Q_PER_KV = N_ATTN_HEADS // N_KV_HEADS
CHUNK_WIDTH = N_CHUNK_HEADS * HEAD_DIM
ATTN_WIDTH = N_ATTN_HEADS * HEAD_DIM
KV_WIDTH = N_KV_HEADS * HEAD_DIM
CHUNK = 128
WINDOW = 128
EPS = 1e-6
ATTN_SCALE = HEAD_DIM ** -0.5

LANES = 128
HALF = LANES // 2
N_PAIRS = ATTN_WIDTH // LANES
VMEM_LIMIT = 56 * 1024 * 1024

PROMPT_TILE = 512
FF_CHUNK = 512
EARLY_CHUNKS = 1
SAMPLE_BB = 16

ROW_G_PRE, ROW_G_PM, ROW_G_PFF, ROW_G_PF, ROW_LN, ROW_GOUT = range(6)


def _dot(a, b):
    return jnp.dot(a, b, preferred_element_type=F32)


def _rms(x, g):
    return x * lax.rsqrt(jnp.mean(x * x, axis=-1, keepdims=True) + EPS) * g


def _layer_norm(x, g, b):
    xc = x - jnp.mean(x, axis=-1, keepdims=True)
    return xc * lax.rsqrt(jnp.mean(xc * xc, axis=-1, keepdims=True) + EPS) * g + b


def _alibi_slope(head):
    return 2.0 ** (-8.0 * (head + 1) / N_ATTN_HEADS)


def _merge(x, a_out, b_out, w_o_ref, vec_ref):
    d = x.shape[-1]
    cw = CHUNK_WIDTH
    ra = _rms(a_out, vec_ref[ROW_GOUT:ROW_GOUT + 1, 0:cw]).astype(BF16)
    rb = _rms(b_out, vec_ref[ROW_GOUT:ROW_GOUT + 1, cw:cw + ATTN_WIDTH]).astype(BF16)
    o = _dot(ra, w_o_ref[0:cw, :]) + _dot(rb, w_o_ref[cw:cw + ATTN_WIDTH, :])
    return x + _rms(o, vec_ref[ROW_G_PM:ROW_G_PM + 1, :d])


def _relu_sq(up):
    r = jnp.maximum(up, 0.0)
    return (r * r).astype(BF16)


def _ffn(x1, w_up_ref, w_down_ref, vec_ref):
    d = x1.shape[-1]
    h2 = _rms(x1, vec_ref[ROW_G_PFF:ROW_G_PFF + 1, :d]).astype(BF16)
    d_ff = w_up_ref.shape[1]
    f = jnp.zeros(x1.shape, F32)
    for c in range(d_ff // FF_CHUNK):
        up = _dot(h2, w_up_ref[:, c * FF_CHUNK:(c + 1) * FF_CHUNK])
        f = f + _dot(_relu_sq(up), w_down_ref[c * FF_CHUNK:(c + 1) * FF_CHUNK, :])
    return x1 + _rms(f, vec_ref[ROW_G_PF:ROW_G_PF + 1, :d])


def _fill_vector_table(vec_s, g_pre, g_pm, g_pff, g_pf, ln_g, ln_b, g_oc, g_oa):
    d = g_pre.shape[-1]
    cw = CHUNK_WIDTH
    vec_s[ROW_G_PRE:ROW_G_PRE + 1, 0:d] = g_pre[...]
    vec_s[ROW_G_PM:ROW_G_PM + 1, 0:d] = g_pm[...]
    vec_s[ROW_G_PFF:ROW_G_PFF + 1, 0:d] = g_pff[...]
    vec_s[ROW_G_PF:ROW_G_PF + 1, 0:d] = g_pf[...]
    vec_s[ROW_LN:ROW_LN + 1, 0:cw] = ln_g[...]
    vec_s[ROW_LN:ROW_LN + 1, cw:2 * cw] = ln_b[...]
    vec_s[ROW_GOUT:ROW_GOUT + 1, 0:cw] = g_oc[...]
    vec_s[ROW_GOUT:ROW_GOUT + 1, cw:cw + ATTN_WIDTH] = g_oa[...]


def _prompt_kernel(n_tiles, nj, layer, sink_ref, x_ref, x1in_ref, f0in_ref, w_in_ref, w_o_ref,
                   w_up_ref, w_down_ref,
                   g_pre, g_pm, g_pff, g_pf, ln_g, ln_b, g_oc, g_oa, wsp_ref, bsp_ref,
                   y_ref, ys_ref, wk_ref, wv_ref, cv_ref,
                   vec_ref, bias_s, wcat_s, tab_s, kt_s, vd_s, a_s, bo_s, m_s, x1_s, x1n_s,
                   h_s, q_s, h2_s, act_s, f_s):
    step = pl.program_id(0)
    j = jnp.minimum(step, n_tiles - 1) % nj
    T, d = x_ref.shape[1], x_ref.shape[2]
    nb = T // WINDOW
    cw = CHUNK_WIDTH
    n_chunks = w_up_ref.shape[1] // FF_CHUNK

    lane = lax.broadcasted_iota(jnp.int32, (WINDOW, LANES), 1)
    lo = lane < HALF

    @pl.when(step == 0)
    def _init_tables():
        _fill_vector_table(vec_ref, g_pre, g_pm, g_pff, g_pf, ln_g, ln_b, g_oc, g_oa)
        x1_s[...] = x1in_ref[...]
        h2_s[...] = _rms(x1in_ref[...], vec_ref[ROW_G_PFF:ROW_G_PFF + 1, :d]).astype(BF16)
        f_s[...] = f0in_ref[...]
        t_idx = lax.broadcasted_iota(jnp.int32, (CHUNK, CHUNK), 0)
        s_idx = lax.broadcasted_iota(jnp.int32, (CHUNK, CHUNK), 1)
        for g in range(N_CHUNK_HEADS):
            wm = jnp.where(s_idx <= t_idx, wsp_ref[0, g], 0.0).astype(BF16)
            wcat_s[g // 2, :, (g % 2) * CHUNK:(g % 2 + 1) * CHUNK] = wm
        b_rows = jnp.concatenate(
            [bsp_ref[0], jnp.zeros((CHUNK - N_CHUNK_HEADS, CHUNK), F32)], axis=0)
        b_cols = b_rows.T
        for p in range(cw // LANES):
            bias_s[:, p * LANES:(p + 1) * LANES] = jnp.where(
                lo, jnp.broadcast_to(b_cols[:, 2 * p:2 * p + 1], (CHUNK, LANES)),
                jnp.broadcast_to(b_cols[:, 2 * p + 1:2 * p + 2], (CHUNK, LANES)))
        a_idx = lax.broadcasted_iota(jnp.int32, (WINDOW, 2 * WINDOW), 0)
        c_idx = lax.broadcasted_iota(jnp.int32, (WINDOW, 2 * WINDOW), 1)
        dist = WINDOW + a_idx - c_idx
        valid = (dist >= 0) & (dist <= WINDOW)
        for hd in range(N_ATTN_HEADS):
            bias = jnp.where(valid, -(_alibi_slope(hd) * dist.astype(F32)), -jnp.inf)
            r = hd % Q_PER_KV
            tab_s[hd // Q_PER_KV, r * WINDOW:(r + 1) * WINDOW, :] = bias

    @pl.when(j == 0)
    def _clear_carry():
        kt_s[:, 0:WINDOW] = jnp.zeros((2 * LANES, WINDOW), BF16)
        vd_s[:, 0:WINDOW, :] = jnp.zeros((N_KV_HEADS, WINDOW, LANES), BF16)

    def ffn_up(c):
        act_s[...] = _relu_sq(_dot(h2_s[...], w_up_ref[:, c * FF_CHUNK:(c + 1) * FF_CHUNK]))

    def ffn_down(c):
        part = _dot(act_s[...], w_down_ref[c * FF_CHUNK:(c + 1) * FF_CHUNK, :])
        if c == 0:
            f_s[...] = part
        else:
            f_s[...] += part

    def finish_previous():
        for blk in range(T // WINDOW):
            rws = slice(blk * WINDOW, (blk + 1) * WINDOW)
            y_ref[0, rws, :] = x1_s[rws, :] + _rms(f_s[rws, :], vec_ref[ROW_G_PF:ROW_G_PF + 1, :d])

    def tile_step(with_late):
        late = iter([functools.partial(fn, c) for c in range(EARLY_CHUNKS, n_chunks)
                     for fn in (ffn_up, ffn_down)])

        def late_stage(count=1):
            if with_late:
                for _ in range(count):
                    next(late)()

        late_stage()
        h_s[...] = _rms(x_ref[0], vec_ref[ROW_G_PRE:ROW_G_PRE + 1, :d]).astype(BF16)
        u_pre = _dot(h_s[...], w_in_ref[:, 0:cw])
        v_pre = _dot(h_s[...], w_in_ref[:, cw:2 * cw])
        late_stage()
        u = jax.nn.gelu(u_pre)
        late_stage()
        v = _layer_norm(jax.nn.gelu(v_pre), vec_ref[ROW_LN:ROW_LN + 1, 0:cw],
                        vec_ref[ROW_LN:ROW_LN + 1, cw:2 * cw])
        kv0 = 2 * cw + ATTN_WIDTH
        kv = _dot(h_s[...], w_in_ref[:, kv0:kv0 + 2 * KV_WIDTH])
        k = kv[:, 0:KV_WIDTH]
        val = kv[:, KV_WIDTH:2 * KV_WIDTH]
        q0 = 2 * cw
        q_s[...] = _dot(h_s[...], w_in_ref[:, q0:q0 + ATTN_WIDTH]) * ATTN_SCALE
        late_stage()

        kt_f = k.T
        wk_ref[0] = kt_f[:, T - WINDOW:]
        wv_ref[0] = val[T - WINDOW:, :].T
        cv_ref[0] = v[T - CHUNK:, :]

        assert nb % 2 == 0
        for c in range(0, nb, 2):
            for p in range(cw // LANES):
                cols = slice(p * LANES, (p + 1) * LANES)
                halves = []
                for cc in (c, c + 1):
                    vcp = v[cc * CHUNK:(cc + 1) * CHUNK, cols]
                    halves.append(jnp.concatenate([jnp.where(lo, vcp, 0.0), jnp.where(lo, 0.0, vcp)],
                                                  axis=0))
                rhs = jnp.concatenate(halves, axis=1).astype(BF16)
                s_pair = _dot(wcat_s[p], rhs)
                for side, cc in enumerate((c, c + 1)):
                    rows = slice(cc * CHUNK, (cc + 1) * CHUNK)
                    s_cp = s_pair[:, side * LANES:(side + 1) * LANES] + bias_s[:, cols]
                    a_s[rows, cols] = u[rows, cols] * s_cp
        m_s[:, 0:cw] = _rms(a_s[...], vec_ref[ROW_GOUT:ROW_GOUT + 1, 0:cw]).astype(BF16)

        kt = kt_f.astype(BF16)
        for hk in range(N_KV_HEADS):
            kth = kt[hk * HEAD_DIM:(hk + 1) * HEAD_DIM, :]
            kt_s[hk * LANES:hk * LANES + HEAD_DIM, WINDOW:] = kth
            kt_s[hk * LANES + HEAD_DIM:(hk + 1) * LANES, WINDOW:] = kth
        lane_t = lax.broadcasted_iota(jnp.int32, (T, LANES), 1)
        lo_t = lane_t < HALF
        val_sw = pltpu.roll(val, HALF, axis=1)
        vd_s[0, WINDOW:, :] = jnp.where(lo_t, val, val_sw).astype(BF16)
        vd_s[1, WINDOW:, :] = jnp.where(lo_t, val_sw, val).astype(BF16)

        first_block_bias = jnp.where(j == 0, -jnp.inf, 0.0).astype(F32)
        head_of_row = lax.broadcasted_iota(jnp.int32, (Q_PER_KV * WINDOW, 1), 0) // WINDOW
        n_units = nb * N_KV_HEADS
        n_mid = 2 * (n_chunks - EARLY_CHUNKS) - 9
        late_after_scores = tuple(1 + (k * n_units) // n_mid for k in range(n_mid))
        for n, (i, hk) in enumerate((i, hk) for i in range(nb) for hk in range(N_KV_HEADS)):
            rows = slice(i * WINDOW, (i + 1) * WINDOW)
            keys = slice(i * WINDOW, (i + 2) * WINDOW)
            tiles = []
            for m in range(hk * 2, hk * 2 + 2):
                qb = q_s[rows, m * LANES:(m + 1) * LANES]
                tiles += [jnp.where(lo, qb, 0.0), jnp.where(lo, 0.0, qb)]
            qq = jnp.concatenate(tiles, axis=0).astype(BF16)
            s = _dot(qq, kt_s[hk * LANES:(hk + 1) * LANES, keys]) + tab_s[hk]
            if i == 0:
                s = jnp.concatenate([s[:, 0:WINDOW] + first_block_bias, s[:, WINDOW:]], axis=1)
            if n in late_after_scores:
                late_stage()
            sink = jnp.zeros((Q_PER_KV * WINDOW, 1), F32)
            for r in range(Q_PER_KV):
                sink = jnp.where(head_of_row == r, sink_ref[layer, hk, r], sink)
            mx = jnp.maximum(jnp.max(s, axis=-1, keepdims=True), sink)
            p_un = jnp.exp(s - mx)
            z = jnp.sum(p_un, axis=-1, keepdims=True) + jnp.exp(sink - mx)
            o = _dot(p_un.astype(BF16), vd_s[hk, keys, :]) * (1.0 / z)
            for pair in range(Q_PER_KV // 2):
                m = hk * 2 + pair
                o_even = o[(2 * pair) * WINDOW:(2 * pair + 1) * WINDOW]
                o_odd = o[(2 * pair + 1) * WINDOW:(2 * pair + 2) * WINDOW]
                bo_s[rows, m * LANES:(m + 1) * LANES] = jnp.where(lo, o_even, o_odd)

        kt_s[:, 0:WINDOW] = kt_s[:, T:T + WINDOW]
        vd_s[:, 0:WINDOW, :] = vd_s[:, T:T + WINDOW, :]

        late_stage()
        m_s[:, cw:cw + ATTN_WIDTH] = _rms(
            bo_s[...], vec_ref[ROW_GOUT:ROW_GOUT + 1, cw:cw + ATTN_WIDTH]).astype(BF16)
        o_proj = _dot(m_s[...], w_o_ref[...])
        late_stage(3)
        for blk in range(T // WINDOW):
            rws = slice(blk * WINDOW, (blk + 1) * WINDOW)
            x1 = x_ref[0, rws, :] + _rms(o_proj[rws], vec_ref[ROW_G_PM:ROW_G_PM + 1, :d])
            x1n_s[rws, :] = x1
            h2_s[rws, :] = _rms(x1, vec_ref[ROW_G_PFF:ROW_G_PFF + 1, :d]).astype(BF16)
        late_stage()

        ffn_up(0)
        if with_late:
            finish_previous()
        x1_s[...] = x1n_s[...]
        ffn_down(0)
        for c in range(1, EARLY_CHUNKS):
            ffn_up(c)
            ffn_down(c)

    @pl.when(step < n_tiles)
    def _tile_step():
        tile_step(True)

    @pl.when(step == 0)
    def _emit_sample():
        ys_ref[...] = y_ref[0].reshape(ys_ref.shape)

    @pl.when(step == n_tiles)
    def _drain_step():
        for c in range(EARLY_CHUNKS, n_chunks):
            ffn_up(c)
            ffn_down(c)
        finish_previous()


def _resident(shape):
    return pl.BlockSpec(shape, lambda *_: (0,) * len(shape), pipeline_mode=pl.Buffered(1))


def _layer_row(arr, layer):
    block = (1,) + arr.shape[1:]
    return pl.BlockSpec(block, lambda *_: (layer,) + (0,) * (arr.ndim - 1),
                        pipeline_mode=pl.Buffered(1))


def _prompt_call(x, x1_sample, f0_sample, sample_shape, layer, sinks, w_in, w_o, w_up, w_down,
                 norms, w_sp, b_sp):
    B, S, D = x.shape
    T = PROMPT_TILE
    assert x1_sample.shape == (T, D) and f0_sample.shape == (T, D)
    assert S % T == 0 and T % WINDOW == 0 and WINDOW == CHUNK
    assert 1 <= EARLY_CHUNKS and 2 * (w_up.shape[1] // FF_CHUNK - EARLY_CHUNKS) >= 9
    nj = S // T
    n_tiles = B * nj
    cur = lambda s: jnp.minimum(s, n_tiles - 1)
    prev = lambda s: jnp.maximum(s - 1, 0)
    tile_in = pl.BlockSpec((1, T, D), lambda s: (cur(s) // nj, cur(s) % nj, 0))
    tile_out = pl.BlockSpec((1, T, D), lambda s: (prev(s) // nj, prev(s) % nj, 0))
    last = lambda r, c: pl.BlockSpec((1, r, c), lambda s: (cur(s) // nj, 0, 0))
    return pl.pallas_call(
        functools.partial(_prompt_kernel, n_tiles, nj, layer),
        grid=(n_tiles + 1,),
        in_specs=[pl.BlockSpec(memory_space=pltpu.SMEM), tile_in, _resident((T, D)),
                  _resident((T, D)),
                  _resident(w_in.shape), _resident(w_o.shape), _resident(w_up.shape),
                  _resident(w_down.shape)] + [_layer_row(a, layer) for a in norms] +
                 [_layer_row(w_sp, layer), _layer_row(b_sp, layer)],
        out_specs=[tile_out, pl.BlockSpec(sample_shape, lambda s: (0,) * len(sample_shape)),
                   last(KV_WIDTH, WINDOW), last(KV_WIDTH, WINDOW), last(CHUNK, CHUNK_WIDTH)],
        out_shape=[jax.ShapeDtypeStruct((B, S, D), F32),
                   jax.ShapeDtypeStruct(sample_shape, F32),
                   jax.ShapeDtypeStruct((B, KV_WIDTH, WINDOW), F32),
                   jax.ShapeDtypeStruct((B, KV_WIDTH, WINDOW), F32),
                   jax.ShapeDtypeStruct((B, CHUNK, CHUNK_WIDTH), F32)],
        scratch_shapes=[pltpu.VMEM((8, D), F32),
                        pltpu.VMEM((CHUNK, CHUNK_WIDTH), F32),
                        pltpu.VMEM((N_CHUNK_HEADS // 2, CHUNK, 2 * CHUNK), BF16),
                        pltpu.VMEM((N_KV_HEADS, Q_PER_KV * WINDOW, 2 * WINDOW), F32),
                        pltpu.VMEM((2 * LANES, T + WINDOW), BF16),
                        pltpu.VMEM((N_KV_HEADS, T + WINDOW, LANES), BF16),
                        pltpu.VMEM((T, CHUNK_WIDTH), F32),
                        pltpu.VMEM((T, ATTN_WIDTH), F32),
                        pltpu.VMEM((T, CHUNK_WIDTH + ATTN_WIDTH), BF16),
                        pltpu.VMEM((T, D), F32),
                        pltpu.VMEM((T, D), F32),
                        pltpu.VMEM((T, D), BF16),
                        pltpu.VMEM((T, ATTN_WIDTH), F32),
                        pltpu.VMEM((T, D), BF16),
                        pltpu.VMEM((T, FF_CHUNK), BF16),
                        pltpu.VMEM((T, D), F32)],
        compiler_params=pltpu.CompilerParams(
            dimension_semantics=("arbitrary",), vmem_limit_bytes=VMEM_LIMIT),
        name="prompt_layer",
    )(sinks, x, x1_sample, f0_sample, w_in, w_o, w_up, w_down, *norms, w_sp, b_sp)


def _sample_kernel(L, layer, sink_ref, w4_ref, b4_ref, x_ref, ckt_ref, cvt_ref, w_in_ref,
                   w_o_hbm, w_up_hbm, w_down_hbm,
                   g_pre, g_pm, g_pff, g_pf, ln_g, ln_b, g_oc, g_oa,
                   x1o_ref, f0o_ref, vn_ref, wkt_ref, wvt_ref,
                   vec_ref, q_s, kvt_s, a_s, b_s, w_o_ref, w_up_ref, w_down_ref, w_sem):
    g = pl.program_id(0)
    n_steps = pl.num_programs(0)
    bb, hd, W = ckt_ref.shape
    n, d = q_s.shape[0], x_ref.shape[-1]
    n_heads = N_ATTN_HEADS
    pair_rows = 2 * L
    assert pair_rows == 8 and hd == LANES and W == LANES
    batch_per_block = LANES // L

    def late_weight_copies():
        early = w_up_ref.shape[1]
        pairs = ((w_o_hbm, w_o_ref), (w_up_hbm.at[:, 0:early], w_up_ref),
                 (w_down_hbm.at[0:early, :], w_down_ref))
        return [pltpu.make_async_copy(src, dst, w_sem.at[i]) for i, (src, dst) in enumerate(pairs)]

    @pl.when(g == 0)
    def _project_and_gate():
        cw = CHUNK_WIDTH
        for copy in late_weight_copies():
            copy.start()
        _fill_vector_table(vec_ref, g_pre, g_pm, g_pff, g_pf, ln_g, ln_b, g_oc, g_oa)
        h = _rms(x_ref[...].reshape(n, d), vec_ref[ROW_G_PRE:ROW_G_PRE + 1, :d]).astype(BF16)
        u = jax.nn.gelu(_dot(h, w_in_ref[:, 0:cw]))
        v = _layer_norm(jax.nn.gelu(_dot(h, w_in_ref[:, cw:2 * cw])),
                        vec_ref[ROW_LN:ROW_LN + 1, 0:cw], vec_ref[ROW_LN:ROW_LN + 1, cw:2 * cw])
        vn_ref[...] = v.reshape(vn_ref.shape)
        q_s[...] = _dot(h, w_in_ref[:, 2 * cw:2 * cw + ATTN_WIDTH]) * ATTN_SCALE
        kv0 = 2 * cw + ATTN_WIDTH
        kvt_s[...] = _dot(h, w_in_ref[:, kv0:kv0 + 2 * KV_WIDTH]).T
        t_of_row = lax.broadcasted_iota(jnp.int32, (n, 1), 0) % L
        group_of_lane = lax.broadcasted_iota(jnp.int32, (1, cw), 1) // HEAD_DIM

        def per_group(value_of_group):
            row = jnp.zeros((1, cw), F32)
            for grp in range(N_CHUNK_HEADS):
                row = jnp.where(group_of_lane == grp, value_of_group(grp), row)
            return row

        s = jnp.zeros(v.shape, F32)
        for t in range(L):
            s = jnp.where(t_of_row == t, per_group(lambda grp: b4_ref[layer, grp, t]), s)
        for delta in range(L):
            coef = jnp.zeros(v.shape, F32)
            for t in range(delta, L):
                w_row = per_group(lambda grp: w4_ref[grp, t, t - delta])
                coef = jnp.where(t_of_row == t, w_row, coef)
            vs = v if delta == 0 else pltpu.roll(v, delta, axis=0)
            s = s + coef * vs
        a_s[...] = u * s

    rows_all = n_heads * pair_rows
    r_idx = lax.broadcasted_iota(jnp.int32, (rows_all, LANES), 0)
    l_idx = lax.broadcasted_iota(jnp.int32, (rows_all, LANES), 1)
    t_idx = r_idx % L
    odd = (r_idx % pair_rows) >= L
    head_of_row = lax.broadcasted_iota(jnp.int32, (rows_all, 1), 0) // pair_rows
    sink = jnp.zeros((rows_all, 1), F32)
    slope = jnp.zeros((rows_all, 1), F32)
    for hh in range(n_heads):
        sink = jnp.where(head_of_row == hh, sink_ref[layer, hh // Q_PER_KV, hh % Q_PER_KV], sink)
        slope = jnp.where(head_of_row == hh, _alibi_slope(hh), slope)
    dist_old = W + t_idx - l_idx
    bias_old = jnp.where(dist_old <= WINDOW, -(slope * dist_old.astype(F32)), -jnp.inf)
    lane8 = lax.broadcasted_iota(jnp.int32, (pair_rows, LANES), 1)
    lo8 = lane8 < HALF
    lane_w = lax.broadcasted_iota(jnp.int32, (hd, W), 1)

    first_batch = g * bb
    new_block = pl.multiple_of((first_batch // batch_per_block) * LANES, LANES)
    kt_new = kvt_s[0:KV_WIDTH, pl.ds(new_block, LANES)]
    vt_new = kvt_s[KV_WIDTH:2 * KV_WIDTH, pl.ds(new_block, LANES)]
    kt_new_b = kt_new.astype(BF16)
    vt_new_b = vt_new.astype(BF16)
    n_pairs = bb // 2
    offs = [((first_batch + 2 * p) % batch_per_block) * L for p in range(n_pairs)]
    rows0 = [pl.multiple_of((first_batch + 2 * p) * L, pair_rows) for p in range(n_pairs)]
    keep = lane_w < W - L

    scores = []
    for p in range(n_pairs):
        tiles = []
        for m in range(N_PAIRS):
            qb = q_s[pl.ds(rows0[p], pair_rows), m * LANES:(m + 1) * LANES]
            qb_sw = pltpu.roll(qb, HALF, axis=1)
            if (2 * m) // Q_PER_KV == 0:
                tiles += [jnp.where(lo8, qb, 0.0), jnp.where(lo8, qb_sw, 0.0)]
            else:
                tiles += [jnp.where(lo8, 0.0, qb_sw), jnp.where(lo8, 0.0, qb)]
        lhs = jnp.concatenate(tiles, axis=0).astype(BF16)
        s_pair = []
        for e in range(2):
            rhs = jnp.concatenate([ckt_ref[2 * p + e].astype(BF16), kt_new_b], axis=1)
            s_pair.append(_dot(lhs, rhs))
        scores.append(jnp.where(jnp.concatenate([odd, odd], axis=1), s_pair[1], s_pair[0]))
        for e in range(2):
            wkt_ref[2 * p + e] = jnp.where(
                keep, pltpu.roll(ckt_ref[2 * p + e], W - L, axis=1),
                pltpu.roll(kt_new, (W - L) - (offs[p] + e * L), axis=1))

    weights = []
    for p in range(n_pairs):
        rel = l_idx - offs[p] - jnp.where(odd, L, 0)
        bias_new = jnp.where((rel >= 0) & (rel <= t_idx),
                             -(slope * (t_idx - rel).astype(F32)), -jnp.inf)
        sc = scores[p] + jnp.concatenate([bias_old, bias_new], axis=1)
        mx = jnp.maximum(jnp.max(sc, axis=-1, keepdims=True), sink)
        p_un = jnp.exp(sc - mx)
        z = jnp.sum(p_un, axis=-1, keepdims=True) + jnp.exp(sink - mx)
        weights.append((p_un.astype(BF16), z))

    for p in range(n_pairs):
        pb, z = weights[p]
        o_pair = []
        for e in range(2):
            vt = jnp.concatenate([cvt_ref[2 * p + e].astype(BF16), vt_new_b], axis=1)
            o_pair.append(lax.dot_general(pb, vt, (((1,), (1,)), ((), ())),
                                          preferred_element_type=F32))
        o = jnp.where(odd, o_pair[1], o_pair[0]) * (1.0 / z)
        for m in range(N_PAIRS):
            kv_head = (2 * m) // Q_PER_KV
            o_even = o[(2 * m) * pair_rows:(2 * m + 1) * pair_rows]
            o_odd = o[(2 * m + 1) * pair_rows:(2 * m + 2) * pair_rows]
            if kv_head == 0:
                o_odd = pltpu.roll(o_odd, HALF, axis=1)
            else:
                o_even = pltpu.roll(o_even, HALF, axis=1)
            b_s[pl.ds(rows0[p], pair_rows), m * LANES:(m + 1) * LANES] = (
                jnp.where(lo8, o_even, o_odd))
        for e in range(2):
            wvt_ref[2 * p + e] = jnp.where(
                keep, pltpu.roll(cvt_ref[2 * p + e], W - L, axis=1),
                pltpu.roll(vt_new, (W - L) - (offs[p] + e * L), axis=1))

    @pl.when(g == n_steps - 1)
    def _merge_and_ffn():
        for copy in late_weight_copies():
            copy.wait()
        x1 = _merge(x_ref[...].reshape(n, d), a_s[...], b_s[...], w_o_ref, vec_ref)
        x1o_ref[...] = x1
        h2 = _rms(x1, vec_ref[ROW_G_PFF:ROW_G_PFF + 1, :d]).astype(BF16)
        f0 = jnp.zeros(x1.shape, F32)
        for c in range(EARLY_CHUNKS):
            up = _dot(h2, w_up_ref[:, c * FF_CHUNK:(c + 1) * FF_CHUNK])
            f0 = f0 + _dot(_relu_sq(up), w_down_ref[c * FF_CHUNK:(c + 1) * FF_CHUNK, :])
        f0o_ref[...] = f0


def _sample_call(x, layer, sinks, w4, b_sp, cache_kt, cache_vt, w_in, w_o, w_up, w_down,
                 norms):
    Bd, L, D = x.shape
    hd, W = cache_kt.shape[1:]
    n = Bd * L
    bb = SAMPLE_BB
    assert Bd % bb == 0 and bb % 2 == 0 and (LANES // L) % bb == 0
    smem = pl.BlockSpec(memory_space=pltpu.SMEM)
    blk = pl.BlockSpec((bb, hd, W), lambda g: (g, 0, 0))
    const = lambda shape: pl.BlockSpec(shape, lambda g: (0,) * len(shape))
    late_weights = (w_o, w_up, w_down)
    in_hbm = pl.BlockSpec(memory_space=pl.ANY)
    return pl.pallas_call(
        functools.partial(_sample_kernel, L, layer),
        grid=(Bd // bb,),
        in_specs=[smem, smem, smem, _resident(x.shape), blk, blk, _resident(w_in.shape)] +
                 [in_hbm for _ in late_weights] + [_layer_row(a, layer) for a in norms],
        out_specs=[const((n, D)), const((n, D)), const((Bd, L, CHUNK_WIDTH)), blk, blk],
        out_shape=[jax.ShapeDtypeStruct((n, D), F32),
                   jax.ShapeDtypeStruct((n, D), F32),
                   jax.ShapeDtypeStruct((Bd, L, CHUNK_WIDTH), F32),
                   jax.ShapeDtypeStruct(cache_kt.shape, F32),
                   jax.ShapeDtypeStruct(cache_vt.shape, F32)],
        scratch_shapes=[pltpu.VMEM((8, D), F32),
                        pltpu.VMEM((n, ATTN_WIDTH), F32),
                        pltpu.VMEM((2 * KV_WIDTH, n), F32),
                        pltpu.VMEM((n, CHUNK_WIDTH), F32),
                        pltpu.VMEM((n, ATTN_WIDTH), F32)] +
                       [pltpu.VMEM(w_o.shape, w_o.dtype),
                        pltpu.VMEM((D, EARLY_CHUNKS * FF_CHUNK), w_up.dtype),
                        pltpu.VMEM((EARLY_CHUNKS * FF_CHUNK, D), w_down.dtype)] +
                       [pltpu.SemaphoreType.DMA((len(late_weights),))],
        compiler_params=pltpu.CompilerParams(
            dimension_semantics=("arbitrary",), vmem_limit_bytes=VMEM_LIMIT),
        name="sample_layer",
    )(sinks, w4, b_sp, x, cache_kt, cache_vt, w_in, *late_weights, *norms)


def _heads_last(t):
    b, _, w = t.shape
    return t.reshape(b, N_KV_HEADS, HEAD_DIM, w).transpose(0, 3, 1, 2)


def kernel(x_prompt, x_sample, cache_win_k, cache_win_v, w_in, g_pre_mix, ln_v_g, ln_v_b,
           w_spatial, b_spatial, attn_sinks, g_out_chunk, g_out_attn, w_o, g_post_mix,
           g_pre_ffn, w_up, w_down, g_post_ffn):
    depth = w_in.shape[0]
    Bd, L, D = x_sample.shape
    W = cache_win_k.shape[2]
    norms = (g_pre_mix, g_post_mix, g_pre_ffn, g_post_ffn, ln_v_g, ln_v_b, g_out_chunk,
             g_out_attn)
    yp, ys = x_prompt, x_sample
    wk_p, wv_p, cv_p, wk_s, wv_s, cv_s = [], [], [], [], [], []
    for l in range(depth):
        w_in_b = w_in[l].astype(BF16)
        w_o_b = w_o[l].astype(BF16)
        w_up_b = w_up[l].astype(BF16)
        w_down_b = w_down[l].astype(BF16)

        w4 = w_spatial[l, :, :L, :L]
        cache_kt = cache_win_k[l].transpose(0, 2, 3, 1).reshape(Bd, KV_WIDTH, W)
        cache_vt = cache_win_v[l].transpose(0, 2, 3, 1).reshape(Bd, KV_WIDTH, W)
        x1_s, f0_s, cv, wk, wv = _sample_call(ys, l, attn_sinks, w4, b_spatial, cache_kt,
                                              cache_vt, w_in_b, w_o_b, w_up_b, w_down_b, norms)
        wk_s.append(_heads_last(wk))
        wv_s.append(_heads_last(wv))
        cv_s.append(cv)

        yp, ys, wk, wv, cv = _prompt_call(yp, x1_s, f0_s, ys.shape, l, attn_sinks, w_in_b, w_o_b,
                                          w_up_b, w_down_b, norms, w_spatial, b_spatial)
        wk_p.append(_heads_last(wk))
        wv_p.append(_heads_last(wv))
        cv_p.append(cv)
    return (yp, ys, jnp.stack(wk_p), jnp.stack(wv_p), jnp.stack(cv_p),
            jnp.stack(wk_s), jnp.stack(wv_s), jnp.stack(cv_s))
```
